```python
import numpy as np
import jax
import jax.numpy as jnp
from jax import lax

D_MODEL = 2048
BATCH = 8
SEQ = 2048
DEPTH = 2

HEAD_DIM = 128
ROPE_THETA = 10000.0
EPS = 1e-6
NEG = -1e30
BLOCK = 128

CONV_CH = D_MODEL // 2
CONV_WIDTH = 31
NSA_Q_HEADS = (D_MODEL // 2) // HEAD_DIM
NSA_KV_GROUPS = 2
NSA_CMP_LEN = 32
NSA_CMP_STRIDE = 16
NSA_CMP_HIDDEN = 256
NSA_SEL_LEN = 64
NSA_TOPK = 8
NSA_WINDOW = 256
NSA_SEL_QBLOCK = 64
NSA_FORCE = 1e6
SWA_Q_HEADS = (D_MODEL // 2) // HEAD_DIM
SWA_KV_HEADS = 2
SWA_WINDOW = 128
GMLP_CH = D_MODEL // 2
GMLP_GROUPS = GMLP_CH // HEAD_DIM
GMLP_CHUNK = 128
D_FF = 11 * D_MODEL // 4
FFN_CONV_WIDTH = 3

N_EVEN = (DEPTH + 1) // 2
N_ODD = DEPTH // 2
NSA_KV_W = NSA_KV_GROUPS * HEAD_DIM
SWA_KV_W = SWA_KV_HEADS * HEAD_DIM
EVEN_SPLITS = (2 * CONV_CH, NSA_Q_HEADS * HEAD_DIM) + (NSA_KV_W,) * 6 + (3 * NSA_Q_HEADS,)
ODD_SPLITS = (SWA_Q_HEADS * HEAD_DIM, SWA_KV_W, SWA_KV_W, 2 * GMLP_CH)
EVEN_IN = sum(EVEN_SPLITS)
ODD_IN = sum(ODD_SPLITS)
EVEN_MIX = CONV_CH + NSA_Q_HEADS * HEAD_DIM
ODD_MIX = SWA_Q_HEADS * HEAD_DIM + GMLP_CH

kernel_name = 'hybrid_conv_nsa_swa_gmlp_trunk'


def split_cols(z, sizes):
    return jnp.split(z, [int(s) for s in np.cumsum(sizes)[:-1]], axis=-1)


def rms_norm(x, g):
    xf = x.astype(jnp.float32)
    y = xf * lax.rsqrt(jnp.mean(xf * xf, axis=-1, keepdims=True) + EPS)
    return (y * g.astype(jnp.float32)).astype(x.dtype)


def layer_norm(x, g, b):
    xf = x.astype(jnp.float32)
    mu = jnp.mean(xf, axis=-1, keepdims=True)
    var = jnp.mean(jnp.square(xf - mu), axis=-1, keepdims=True)
    y = (xf - mu) * lax.rsqrt(var + EPS) * g.astype(jnp.float32) + b.astype(jnp.float32)
    return y.astype(x.dtype)


def rope_angles(pos):
    inv = ROPE_THETA ** (-jnp.arange(0, HEAD_DIM, 2, dtype=jnp.float32) / HEAD_DIM)
    ang = pos.astype(jnp.float32)[..., None] * inv
    return jnp.cos(ang), jnp.sin(ang)


def apply_rope(x, cos, sin):
    shp = (cos.shape[0],) + (1,) * (x.ndim - 3) + cos.shape[1:]
    cs, sn = cos.reshape(shp), sin.reshape(shp)
    x1, x2 = jnp.split(x.astype(jnp.float32), 2, axis=-1)
    return jnp.concatenate([x1 * cs - x2 * sn, x2 * cs + x1 * sn], axis=-1).astype(x.dtype)


def causal_dwconv(x, w, b):
    k = w.shape[0]
    y = lax.conv_general_dilated(x, w[:, None, :], window_strides=(1,), padding=[(k - 1, 0)],
                                 dimension_numbers=('NWC', 'WIO', 'NWC'),
                                 feature_group_count=x.shape[-1])
    return y + b


def banded_attention(q, k, v, window, sink=None):
    B, G, R, T, dh = q.shape
    nb, nw = T // BLOCK, window // BLOCK

    def band(a):
        ap = jnp.pad(a, ((0, 0), (0, 0), (nw * BLOCK, 0), (0, 0))).reshape(B, G, nb + nw, BLOCK, dh)
        return jnp.concatenate([ap[:, :, i:i + nb] for i in range(nw + 1)], axis=3)

    kb, vb = band(k), band(v)
    qb = q.reshape(B, G, R, nb, BLOCK, dh)
    s = jnp.einsum('bgrnqd,bgnkd->bgrnqk', qb, kb).astype(jnp.float32) * (dh ** -0.5)
    qi = np.arange(BLOCK)[:, None]
    ki = np.arange((nw + 1) * BLOCK)[None, :]
    rel = qi - ki + nw * BLOCK
    kpos = np.arange(nb)[:, None, None] * BLOCK + ki[None] - nw * BLOCK
    mask = (rel >= 0) & (rel < window) & (kpos >= 0)
    s = jnp.where(mask, s, NEG)
    if sink is None:
        p = jax.nn.softmax(s, axis=-1)
    else:
        sk = sink.astype(jnp.float32)[None, :, :, None, None, None]
        m = jnp.maximum(jnp.max(s, axis=-1, keepdims=True), sk)
        e = jnp.exp(s - m)
        p = e / (jnp.sum(e, axis=-1, keepdims=True) + jnp.exp(sk - m))
    o = jnp.einsum('bgrnqk,bgnkd->bgrnqd', p.astype(v.dtype), vb)
    return o.reshape(B, G, R, T, dh)


def conformer_conv(a, w_dw, b_dw, ln_g, ln_b):
    a1, a2 = jnp.split(a, 2, axis=-1)
    y = causal_dwconv(a1 * jax.nn.sigmoid(a2), w_dw, b_dw)
    return jax.nn.silu(layer_norm(y, ln_g, ln_b))


def nsa_attention(q, k_cmp, v_cmp, k_sel, v_sel, k_win, v_win, gate_logits, cos, sin, positions,
                  q_norm, k_norm, cmp_k_pos, cmp_k_w1, cmp_k_w2, cmp_v_pos, cmp_v_w1, cmp_v_w2):
    B, T, _ = q.shape
    G, R, dh = NSA_KV_GROUPS, NSA_Q_HEADS // NSA_KV_GROUPS, HEAD_DIM
    scale = dh ** -0.5
    q = q.reshape(B, T, G, R, dh).transpose(0, 2, 3, 1, 4)
    q = apply_rope(rms_norm(q, q_norm), cos, sin)

    def heads(a):
        return a.reshape(B, T, G, dh).transpose(0, 2, 1, 3)

    k_cmp, v_cmp, k_sel, v_sel, k_win, v_win = [heads(a) for a in (k_cmp, v_cmp, k_sel, v_sel, k_win, v_win)]
    t_idx = np.arange(T)

    nc = (T - NSA_CMP_LEN) // NSA_CMP_STRIDE + 1
    starts = np.arange(nc) * NSA_CMP_STRIDE
    ends = starts + NSA_CMP_LEN - 1
    bidx = starts[:, None] + np.arange(NSA_CMP_LEN)[None, :]

    def compress(a, pos_emb, w1, w2):
        ab = (a[:, :, bidx] + pos_emb).reshape(B, G, nc, NSA_CMP_LEN * dh)
        return jax.nn.silu(ab @ w1) @ w2

    kc = compress(k_cmp, cmp_k_pos, cmp_k_w1, cmp_k_w2)
    vc = compress(v_cmp, cmp_v_pos, cmp_v_w1, cmp_v_w2)
    cc, sc = rope_angles(positions[:, ends])
    kc = apply_rope(rms_norm(kc, k_norm[0]), cc, sc)
    s_c = jnp.einsum('bgrtd,bgnd->bgrtn', q, kc).astype(jnp.float32) * scale
    valid_c = ends[None, :] <= t_idx[:, None]
    p_c = jax.nn.softmax(jnp.where(valid_c, s_c, NEG), axis=-1) * valid_c
    o_cmp = jnp.einsum('bgrtn,bgnd->bgrtd', p_c.astype(vc.dtype), vc)

    ns = T // NSA_SEL_LEN
    sel_start = np.arange(ns) * NSA_SEL_LEN
    overlap = ((starts[:, None] <= sel_start[None, :] + NSA_SEL_LEN - 1) &
               (ends[:, None] >= sel_start[None, :])).astype(np.float32)
    imp = jnp.einsum('bgrtn,ns->bgts', p_c, jnp.asarray(overlap))
    cur = t_idx // NSA_SEL_LEN
    jb = np.arange(ns)[None, :]
    forced = (jb == 0) | (jb == cur[:, None]) | (jb == cur[:, None] - 1)
    valid_s = sel_start[None, :] <= t_idx[:, None]
    score = jnp.where(forced, NSA_FORCE, jnp.where(valid_s, imp, -1.0))
    n_top = min(NSA_TOPK, ns)
    _, sel = lax.top_k(score, n_top)

    k_sel = apply_rope(rms_norm(k_sel, k_norm[1]), cos, sin)
    ks_blk = k_sel.reshape(B, G, ns, NSA_SEL_LEN, dh)
    vs_blk = v_sel.reshape(B, G, ns, NSA_SEL_LEN, dh)
    qbs = NSA_SEL_QBLOCK
    nq = T // qbs
    q_ch = q.reshape(B, G, R, nq, qbs, dh).transpose(3, 0, 1, 2, 4, 5)
    sel_ch = sel.reshape(B, G, nq, qbs, n_top).transpose(2, 0, 1, 3, 4)
    t_ch = jnp.arange(T, dtype=jnp.int32).reshape(nq, qbs)
    bi = jnp.arange(B)[:, None, None, None]
    gi = jnp.arange(G)[None, :, None, None]

    def sel_block(args):
        qc, ic, tc = args
        kg = ks_blk[bi, gi, ic]
        vg = vs_blk[bi, gi, ic]
        s = jnp.einsum('bgrqd,bgqnld->bgrqnl', qc, kg).astype(jnp.float32) * scale
        kpos = ic[..., None] * NSA_SEL_LEN + jnp.arange(NSA_SEL_LEN, dtype=jnp.int32)
        m = kpos <= tc[:, None, None]
        s = jnp.where(m[:, :, None], s, NEG).reshape(B, G, R, qbs, n_top * NSA_SEL_LEN)
        p = jax.nn.softmax(s, axis=-1).reshape(B, G, R, qbs, n_top, NSA_SEL_LEN)
        return jnp.einsum('bgrqnl,bgqnld->bgrqd', p.astype(vg.dtype), vg)

    o_sel = lax.map(sel_block, (q_ch, sel_ch, t_ch))
    o_sel = o_sel.transpose(1, 2, 3, 0, 4, 5).reshape(B, G, R, T, dh)

    k_win = apply_rope(rms_norm(k_win, k_norm[2]), cos, sin)
    o_win = banded_attention(q, k_win, v_win, NSA_WINDOW)

    g = jax.nn.sigmoid(gate_logits.astype(jnp.float32)).reshape(B, T, G, R, 3)
    g = g.transpose(0, 2, 3, 1, 4).astype(q.dtype)
    o = g[..., 0:1] * o_cmp + g[..., 1:2] * o_sel + g[..., 2:3] * o_win
    return o.transpose(0, 3, 1, 2, 4).reshape(B, T, NSA_Q_HEADS * dh)


def chunked_gmlp(uv, ln_g, ln_b, w_s, b_s):
    B, T, _ = uv.shape
    u, v = jnp.split(jax.nn.gelu(uv), 2, axis=-1)
    v = layer_norm(v, ln_g, ln_b)
    nch = T // GMLP_CHUNK
    v = v.reshape(B, nch, GMLP_CHUNK, GMLP_GROUPS, HEAD_DIM)
    tril = np.tril(np.ones((GMLP_CHUNK, GMLP_CHUNK), dtype=bool))
    w = jnp.where(tril, w_s, jnp.zeros_like(w_s))
    sp = jnp.einsum('gts,bnsgc->bntgc', w, v) + b_s.T[:, :, None]
    return u * sp.reshape(B, T, GMLP_CH)


def even_mixer(h, cos, sin, positions, w_in, w_out, conv_w, conv_b, conv_ln_g, conv_ln_b, q_norm, k_norm,
               cmp_k_pos, cmp_k_w1, cmp_k_w2, cmp_v_pos, cmp_v_w1, cmp_v_w2):
    z = h @ w_in
    a, q, kc, vc, ks, vs, kw, vw, gl = split_cols(z, EVEN_SPLITS)
    y_a = conformer_conv(a, conv_w, conv_b, conv_ln_g, conv_ln_b)
    y_b = nsa_attention(q, kc, vc, ks, vs, kw, vw, gl, cos, sin, positions, q_norm, k_norm,
                        cmp_k_pos, cmp_k_w1, cmp_k_w2, cmp_v_pos, cmp_v_w1, cmp_v_w2)
    return jnp.concatenate([y_a, y_b], axis=-1) @ w_out


def odd_mixer(h, cos, sin, w_in, w_out, q_norm, k_norm, sinks, ln_g, ln_b, w_s, b_s):
    B, T, _ = h.shape
    G, R = SWA_KV_HEADS, SWA_Q_HEADS // SWA_KV_HEADS
    z = h @ w_in
    q, k, v, uv = split_cols(z, ODD_SPLITS)
    q = q.reshape(B, T, G, R, HEAD_DIM).transpose(0, 2, 3, 1, 4)
    k = k.reshape(B, T, G, HEAD_DIM).transpose(0, 2, 1, 3)
    v = v.reshape(B, T, G, HEAD_DIM).transpose(0, 2, 1, 3)
    q = apply_rope(rms_norm(q, q_norm), cos, sin)
    k = apply_rope(rms_norm(k, k_norm), cos, sin)
    o = banded_attention(q, k, v, SWA_WINDOW, sinks.reshape(G, R))
    y_c = o.transpose(0, 3, 1, 2, 4).reshape(B, T, SWA_Q_HEADS * HEAD_DIM)
    y_d = chunked_gmlp(uv, ln_g, ln_b, w_s, b_s)
    return jnp.concatenate([y_c, y_d], axis=-1) @ w_out


def conv_ffn(h, w_up, conv_w, conv_b, w_down):
    u = causal_dwconv(h @ w_up, conv_w, conv_b)
    a, b = jnp.split(u, 2, axis=-1)
    return (jax.nn.silu(a) * b) @ w_down


def setup_inputs(seed: int = 0) -> dict:
    key = jax.random.key(seed)
    ks = iter(jax.random.split(key, 48))

    def nrm(shape, scale):
        return scale * jax.random.normal(next(ks), shape, jnp.float32)

    def gain(shape):
        return 1.0 + nrm(shape, 0.02)

    D, dh = D_MODEL, HEAD_DIM
    L, H = NSA_CMP_LEN, NSA_CMP_HIDDEN
    inp = {}
    inp['x'] = nrm((BATCH, SEQ, D), 1.0)
    inp['c'] = nrm((BATCH, D), 1.0)
    off = jax.random.randint(next(ks), (BATCH, 1), 0, 4096, dtype=jnp.int32)
    inp['positions'] = (off + jnp.arange(SEQ, dtype=jnp.int32)[None, :]).astype(jnp.int32)
    inp['ada_w'] = nrm((DEPTH, D, 6 * D), 0.5 * D ** -0.5)
    inp['ada_b'] = nrm((DEPTH, 6 * D), 0.01)
    inp['norm_g'] = gain((DEPTH, 2, D))
    inp['ffn_w_up'] = nrm((DEPTH, D, 2 * D_FF), D ** -0.5)
    inp['ffn_conv_w'] = nrm((DEPTH, FFN_CONV_WIDTH, 2 * D_FF), FFN_CONV_WIDTH ** -0.5)
    inp['ffn_conv_b'] = nrm((DEPTH, 2 * D_FF), 0.01)
    inp['ffn_w_down'] = nrm((DEPTH, D_FF, D), D_FF ** -0.5)
    inp['ev_w_in'] = nrm((N_EVEN, D, EVEN_IN), D ** -0.5)
    inp['ev_w_out'] = nrm((N_EVEN, EVEN_MIX, D), EVEN_MIX ** -0.5)
    inp['ev_conv_w'] = nrm((N_EVEN, CONV_WIDTH, CONV_CH), CONV_WIDTH ** -0.5)
    inp['ev_conv_b'] = nrm((N_EVEN, CONV_CH), 0.01)
    inp['ev_conv_ln_g'] = gain((N_EVEN, CONV_CH))
    inp['ev_conv_ln_b'] = nrm((N_EVEN, CONV_CH), 0.01)
    inp['ev_q_norm'] = gain((N_EVEN, dh))
    inp['ev_k_norm'] = gain((N_EVEN, 3, dh))
    inp['ev_cmp_k_pos'] = nrm((N_EVEN, L, dh), 0.02)
    inp['ev_cmp_k_w1'] = nrm((N_EVEN, L * dh, H), (L * dh) ** -0.5)
    inp['ev_cmp_k_w2'] = nrm((N_EVEN, H, dh), H ** -0.5)
    inp['ev_cmp_v_pos'] = nrm((N_EVEN, L, dh), 0.02)
    inp['ev_cmp_v_w1'] = nrm((N_EVEN, L * dh, H), (L * dh) ** -0.5)
    inp['ev_cmp_v_w2'] = nrm((N_EVEN, H, dh), H ** -0.5)
    inp['od_w_in'] = nrm((N_ODD, D, ODD_IN), D ** -0.5)
    inp['od_w_out'] = nrm((N_ODD, ODD_MIX, D), ODD_MIX ** -0.5)
    inp['od_q_norm'] = gain((N_ODD, dh))
    inp['od_k_norm'] = gain((N_ODD, dh))
    inp['od_sinks'] = nrm((N_ODD, SWA_Q_HEADS), 0.5)
    inp['od_gmlp_ln_g'] = gain((N_ODD, GMLP_CH))
    inp['od_gmlp_ln_b'] = nrm((N_ODD, GMLP_CH), 0.01)
    inp['od_gmlp_w_s'] = nrm((N_ODD, GMLP_GROUPS, GMLP_CHUNK, GMLP_CHUNK), GMLP_CHUNK ** -0.5)
    inp['od_gmlp_b_s'] = gain((N_ODD, GMLP_GROUPS, GMLP_CHUNK))
    return inp


def reference(x, c, positions, ada_w, ada_b, norm_g, ffn_w_up, ffn_conv_w, ffn_conv_b, ffn_w_down,
              ev_w_in, ev_w_out, ev_conv_w, ev_conv_b, ev_conv_ln_g, ev_conv_ln_b, ev_q_norm, ev_k_norm,
              ev_cmp_k_pos, ev_cmp_k_w1, ev_cmp_k_w2, ev_cmp_v_pos, ev_cmp_v_w1, ev_cmp_v_w2,
              od_w_in, od_w_out, od_q_norm, od_k_norm, od_sinks, od_gmlp_ln_g, od_gmlp_ln_b,
              od_gmlp_w_s, od_gmlp_b_s):
    cos, sin = rope_angles(positions)
    c_act = jax.nn.silu(c)
    for i in range(DEPTH):
        mod = (c_act @ ada_w[i] + ada_b[i])[:, None, :]
        sh1, sc1, g1, sh2, sc2, g2 = jnp.split(mod, 6, axis=-1)
        h = rms_norm(x, norm_g[i, 0]) * (1.0 + sc1) + sh1
        if i % 2 == 0:
            j = i // 2
            y = even_mixer(h, cos, sin, positions, ev_w_in[j], ev_w_out[j], ev_conv_w[j], ev_conv_b[j],
                           ev_conv_ln_g[j], ev_conv_ln_b[j], ev_q_norm[j], ev_k_norm[j],
                           ev_cmp_k_pos[j], ev_cmp_k_w1[j], ev_cmp_k_w2[j],
                           ev_cmp_v_pos[j], ev_cmp_v_w1[j], ev_cmp_v_w2[j])
        else:
            j = i // 2
            y = odd_mixer(h, cos, sin, od_w_in[j], od_w_out[j], od_q_norm[j], od_k_norm[j], od_sinks[j],
                          od_gmlp_ln_g[j], od_gmlp_ln_b[j], od_gmlp_w_s[j], od_gmlp_b_s[j])
        x = x + g1 * y
        h = rms_norm(x, norm_g[i, 1]) * (1.0 + sc2) + sh2
        x = x + g2 * conv_ffn(h, ffn_w_up[i], ffn_conv_w[i], ffn_conv_b[i], ffn_w_down[i])
    return x
```

```python
import functools

import numpy as np
import jax
import jax.numpy as jnp
from jax import lax
from jax.experimental import pallas as pl
from jax.experimental.pallas import tpu as pltpu

F32 = jnp.float32
BF16 = jnp.bfloat16

HEAD_DIM = 128
ROPE_THETA = 10000.0
EPS = 1e-6
NEG = -1e30

CONV_WIDTH = 31
NSA_KV_GROUPS = 2
NSA_CMP_LEN = 32
NSA_CMP_STRIDE = 16
NSA_CMP_HIDDEN = 256
NSA_SEL_LEN = 64
NSA_TOPK = 8
NSA_WINDOW = 256
NSA_FORCE = 1e6
SWA_KV_HEADS = 2
SWA_WINDOW = 128
GMLP_CHUNK = 128
FFN_CONV_WIDTH = 3

V7X_VMEM_BYTES = 64 * 1024 * 1024
VMEM_LIMIT = V7X_VMEM_BYTES - 8 * 1024 * 1024
LANES = 128
BF16_SUBLANES = 16


def _params(sem):
    return pltpu.CompilerParams(dimension_semantics=sem, vmem_limit_bytes=VMEM_LIMIT)


def _dot(a, b):
    return jnp.dot(a, b, preferred_element_type=F32)


def _dot_nt(a, b):
    return lax.dot_general(a, b, (((1,), (1,)), ((), ())), preferred_element_type=F32)


def _sigmoid(x):
    return 1.0 / (1.0 + jnp.exp(-x))


def _silu(x):
    return x * _sigmoid(x)


def _gelu_tanh(x):
    return 0.5 * x * (1.0 + jnp.tanh(np.sqrt(2.0 / np.pi).astype(np.float32) * (x + 0.044715 * (x * x * x))))


def _rope_kernel(pos_ref, inv_ref, cos_ref, sin_ref):
    ang = pos_ref[0].astype(F32) * inv_ref[0:1, :]
    cos_ref[0] = jnp.cos(ang)
    sin_ref[0] = jnp.sin(ang) * inv_ref[1:2, :]


def rope_tables(positions):
    B, T = positions.shape
    inv = ROPE_THETA ** (-jnp.arange(0, HEAD_DIM, 2, dtype=F32) / HEAD_DIM)
    half = HEAD_DIM // 2
    sign = jnp.concatenate([-jnp.ones((half,), F32), jnp.ones((half,), F32)])
    tab = jnp.stack([jnp.concatenate([inv, inv]), sign])
    return pl.pallas_call(
        _rope_kernel,
        grid=(B,),
        in_specs=[pl.BlockSpec((1, T, 1), lambda b: (b, 0, 0)),
                  pl.BlockSpec((2, HEAD_DIM), lambda b: (0, 0))],
        out_specs=[pl.BlockSpec((1, T, HEAD_DIM), lambda b: (b, 0, 0))] * 2,
        out_shape=[jax.ShapeDtypeStruct((B, T, HEAD_DIM), F32)] * 2,
        compiler_params=_params(("parallel",)),
        name="rope_tables",
    )(positions.reshape(B, T, 1), tab)


def _adaln_kernel(c_ref, w_ref, b_ref, o_ref):
    ca = _silu(c_ref[...]).astype(BF16)
    o_ref[0] = _dot(ca, w_ref[0].astype(BF16)) + b_ref[0]


def adaln(c, ada_w, ada_b, tn=1024):
    L, D, N6 = ada_w.shape
    B = c.shape[0]
    return pl.pallas_call(
        _adaln_kernel,
        grid=(L, N6 // tn),
        in_specs=[pl.BlockSpec((B, D), lambda l, j: (0, 0)),
                  pl.BlockSpec((1, D, tn), lambda l, j: (l, 0, j)),
                  pl.BlockSpec((1, 1, tn), lambda l, j: (l, 0, j))],
        out_specs=pl.BlockSpec((1, B, tn), lambda l, j: (l, 0, j)),
        out_shape=jax.ShapeDtypeStruct((L, B, N6), F32),
        compiler_params=_params(("parallel", "parallel")),
        name="adaln",
    )(c, ada_w, ada_b.reshape(L, 1, N6))


def _norm_mod_rows(x_ref, g, scale1, shift, out_ref, out_row0, nrows, chunk):
    def body(ci, carry):
        r = pl.multiple_of(ci * chunk, chunk)
        x = x_ref[pl.ds(r, chunk), :]
        ms = jnp.mean(x * x, axis=-1, keepdims=True)
        y = x * lax.rsqrt(ms + EPS) * g
        out_ref[pl.ds(out_row0 + r, chunk), :] = (y * scale1 + shift).astype(BF16)
        return carry
    lax.fori_loop(0, nrows // chunk, body, 0)


def _inproj_kernel(x_ref, g_ref, sc_ref, sh_ref, w_ref, *rest, has_gate, chunk):
    if has_gate:
        wg_ref, o_ref, og_ref, h_ref = rest
    else:
        o_ref, h_ref = rest

    @pl.when(pl.program_id(1) == 0)
    def _():
        _norm_mod_rows(x_ref, g_ref[...], 1.0 + sc_ref[0], sh_ref[0], h_ref, 0, x_ref.shape[0], chunk)
        if has_gate:
            og_ref[...] = _dot(h_ref[...], wg_ref[...])

    o_ref[...] = _dot(h_ref[...], w_ref[...])


def in_proj(x, g, sc, sh, w, wg, T, tm=1024, tn=512):
    N, D = x.shape
    Nout = w.shape[1]
    tm = min(tm, T)
    assert N % tm == 0 and T % tm == 0 and Nout % tn == 0
    bmap = lambda i, j: ((i * tm) // T, 0, 0)
    in_specs = [pl.BlockSpec((tm, D), lambda i, j: (i, 0)),
                pl.BlockSpec((1, D), lambda i, j: (0, 0)),
                pl.BlockSpec((1, 1, D), bmap),
                pl.BlockSpec((1, 1, D), bmap),
                pl.BlockSpec((D, tn), lambda i, j: (0, j))]
    out_specs = [pl.BlockSpec((tm, tn), lambda i, j: (i, j))]
    out_shape = [jax.ShapeDtypeStruct((N, Nout), F32)]
    args = [x, g.reshape(1, D), sc, sh, w]
    if wg is not None:
        ng = wg.shape[1]
        in_specs.append(pl.BlockSpec((D, ng), lambda i, j: (0, 0)))
        out_specs.append(pl.BlockSpec((tm, ng), lambda i, j: (i, 0)))
        out_shape.append(jax.ShapeDtypeStruct((N, ng), F32))
        args.append(wg)
    return pl.pallas_call(
        functools.partial(_inproj_kernel, has_gate=wg is not None, chunk=min(128, tm)),
        grid=(N // tm, Nout // tn),
        in_specs=in_specs, out_specs=out_specs, out_shape=out_shape,
        scratch_shapes=[pltpu.VMEM((tm, D), BF16)],
        compiler_params=_params(("parallel", "arbitrary")),
        name="in_proj",
    )(*args)


def _outproj_kernel(ya_ref, yb_ref, wa_ref, wb_ref, x_ref, gate_ref, o_ref):
    y = _dot(ya_ref[...], wa_ref[...]) + _dot(yb_ref[...], wb_ref[...])
    o_ref[...] = x_ref[...] + gate_ref[0] * y


def out_proj(ya, yb, w_out, x, gate, T, tm=1024, tn=512):
    N, D = x.shape
    Ka, Kb = ya.shape[1], yb.shape[1]
    tm = min(tm, T)
    assert N % tm == 0 and T % tm == 0 and D % tn == 0 and Ka % tn == 0
    return pl.pallas_call(
        _outproj_kernel,
        grid=(N // tm, D // tn),
        in_specs=[pl.BlockSpec((tm, Ka), lambda i, j: (i, 0)),
                  pl.BlockSpec((tm, Kb), lambda i, j: (i, 0)),
                  pl.BlockSpec((Ka, tn), lambda i, j: (0, j)),
                  pl.BlockSpec((Kb, tn), lambda i, j: (Ka // Kb, j)),
                  pl.BlockSpec((tm, tn), lambda i, j: (i, j)),
                  pl.BlockSpec((1, 1, tn), lambda i, j: ((i * tm) // T, 0, j))],
        out_specs=pl.BlockSpec((tm, tn), lambda i, j: (i, j)),
        out_shape=jax.ShapeDtypeStruct((N, D), F32),
        compiler_params=_params(("parallel", "parallel")),
        name="out_proj",
    )(ya, yb, w_out, w_out, x, gate)


FFN_HALO = BF16_SUBLANES


def _ffn_kernel(x_ref, xh_ref, g_ref, sc_ref, sh_ref, gate_ref, wa_ref, wb_ref, cwa_ref, cwb_ref,
                wd_ref, o_ref, h_ref, acc_ref, *, blocks_per_seq, chunk):
    i = pl.program_id(0)
    j = pl.program_id(1)
    tm = x_ref.shape[0]

    @pl.when(j == 0)
    def _():
        g = g_ref[...]
        scale1 = 1.0 + sc_ref[0]
        shift = sh_ref[0]
        _norm_mod_rows(x_ref, g, scale1, shift, h_ref, FFN_HALO, tm, chunk)
        xh = xh_ref[...]
        ms = jnp.mean(xh * xh, axis=-1, keepdims=True)
        hh = (xh * lax.rsqrt(ms + EPS) * g) * scale1 + shift
        first = (i % blocks_per_seq) == 0
        h_ref[0:FFN_HALO, :] = jnp.where(first, 0.0, hh).astype(BF16)
        acc_ref[...] = jnp.zeros_like(acc_ref)

    h = h_ref[...]

    def conv(y, cw_ref):
        cw = cw_ref[0]
        out = cw[FFN_CONV_WIDTH:FFN_CONV_WIDTH + 1, :]
        for k in range(FFN_CONV_WIDTH):
            off = FFN_HALO - (FFN_CONV_WIDTH - 1) + k
            out = out + cw[k:k + 1, :] * y[off:off + tm, :]
        return out

    ua = conv(_dot(h, wa_ref[...]), cwa_ref)
    ub = conv(_dot(h, wb_ref[...]), cwb_ref)
    act = (_silu(ua) * ub).astype(BF16)
    acc_ref[...] += _dot(act, wd_ref[...])

    @pl.when(j == pl.num_programs(1) - 1)
    def _():
        o_ref[...] = x_ref[...] + gate_ref[0] * acc_ref[...]


def conv_ffn(x, g, sc, sh, gate, w_up, cw, w_down, T, tm=512, tf=512):
    N, D = x.shape
    DFF = w_down.shape[0]
    tm = min(tm, T)
    assert N % tm == 0 and T % tm == 0 and DFF % tf == 0 and tm % FFN_HALO == 0
    nff = DFF // tf
    hb = tm // FFN_HALO
    bmap = lambda i, j: ((i * tm) // T, 0, 0)
    return pl.pallas_call(
        functools.partial(_ffn_kernel, blocks_per_seq=T // tm, chunk=min(128, tm)),
        grid=(N // tm, nff),
        in_specs=[pl.BlockSpec((tm, D), lambda i, j: (i, 0)),
                  pl.BlockSpec((FFN_HALO, D), lambda i, j: (jnp.maximum(i * hb - 1, 0), 0)),
                  pl.BlockSpec((1, D), lambda i, j: (0, 0)),
                  pl.BlockSpec((1, 1, D), bmap),
                  pl.BlockSpec((1, 1, D), bmap),
                  pl.BlockSpec((1, 1, D), bmap),
                  pl.BlockSpec((D, tf), lambda i, j: (0, j)),
                  pl.BlockSpec((D, tf), lambda i, j: (0, j + nff)),
                  pl.BlockSpec((1, 8, tf), lambda i, j: (0, 0, j)),
                  pl.BlockSpec((1, 8, tf), lambda i, j: (0, 0, j + nff)),
                  pl.BlockSpec((tf, D), lambda i, j: (j, 0))],
        out_specs=pl.BlockSpec((tm, D), lambda i, j: (i, 0)),
        out_shape=jax.ShapeDtypeStruct((N, D), F32),
        scratch_shapes=[pltpu.VMEM((tm + FFN_HALO, D), BF16), pltpu.VMEM((tm, D), F32)],
        compiler_params=_params(("parallel", "arbitrary")),
        name="conv_ffn",
    )(x, x, g.reshape(1, D), sc, sh, gate, w_up, w_up, cw, cw, w_down)


def _rope(y, cos, sin):
    return y * cos + pltpu.roll(y, HEAD_DIM // 2, axis=1) * sin


def _head_norm(x, g):
    ms = jnp.mean(x * x, axis=-1, keepdims=True)
    return x * lax.rsqrt(ms + EPS) * g


def _prep_kernel(*refs, n_q_heads, kv_kinds, n_groups):
    nkv = len(kv_kinds)
    zq_ref = refs[0]
    kv_refs = refs[1:1 + nkv]
    cos_ref, sin_ref, qn_ref, kn_ref = refs[1 + nkv:5 + nkv]
    q_out = refs[5 + nkv]
    kv_out = refs[6 + nkv:]
    cos = cos_ref[0]
    sin = sin_ref[0]
    scale = HEAD_DIM ** -0.5
    for hd in range(n_q_heads):
        sl = slice(hd * HEAD_DIM, (hd + 1) * HEAD_DIM)
        y = _rope(_head_norm(zq_ref[0, :, sl], qn_ref[...]), cos, sin)
        q_out[0, :, sl] = (y * scale).astype(BF16)
    for idx, kind in enumerate(kv_kinds):
        for gi in range(n_groups):
            sl = slice(gi * HEAD_DIM, (gi + 1) * HEAD_DIM)
            a = kv_refs[idx][0, :, sl]
            if kind >= 0:
                a = _rope(_head_norm(a, kn_ref[kind:kind + 1, :]), cos, sin)
            kv_out[idx][0, gi] = a.astype(BF16)


def attn_prep(z3, cosf, sinf, q_norm, k_norm, q_col, kv_cols, kv_kinds, n_groups, tm=512):
    B, T, _ = z3.shape
    tm = min(tm, T)
    QW = 8 * HEAD_DIM
    KW = n_groups * HEAD_DIM
    assert q_col % QW == 0 and all(c % KW == 0 for c in kv_cols) and T % tm == 0
    nkv = len(kv_cols)
    in_specs = [pl.BlockSpec((1, tm, QW), lambda b, t: (b, t, q_col // QW))]
    for c in kv_cols:
        in_specs.append(pl.BlockSpec((1, tm, KW), functools.partial(lambda b, t, cb: (b, t, cb), cb=c // KW)))
    in_specs += [pl.BlockSpec((1, tm, HEAD_DIM), lambda b, t: (b, t, 0))] * 2
    in_specs += [pl.BlockSpec((1, HEAD_DIM), lambda b, t: (0, 0)),
                 pl.BlockSpec(k_norm.shape, lambda b, t: (0, 0))]
    out_specs = [pl.BlockSpec((1, tm, QW), lambda b, t: (b, t, 0))]
    out_specs += [pl.BlockSpec((1, n_groups, tm, HEAD_DIM), lambda b, t: (b, 0, t, 0))] * nkv
    out_shape = [jax.ShapeDtypeStruct((B, T, QW), BF16)]
    out_shape += [jax.ShapeDtypeStruct((B, n_groups, T, HEAD_DIM), BF16)] * nkv
    return pl.pallas_call(
        functools.partial(_prep_kernel, n_q_heads=QW // HEAD_DIM, kv_kinds=tuple(kv_kinds), n_groups=n_groups),
        grid=(B, T // tm),
        in_specs=in_specs, out_specs=out_specs, out_shape=out_shape,
        compiler_params=_params(("parallel", "parallel")),
        name="attn_prep",
    )(*([z3] * (1 + nkv)), cosf, sinf, q_norm.reshape(1, HEAD_DIM), k_norm)


def _compress_kernel(ak_ref, av_ref, w1k_ref, w1v_ref, pk_ref, pv_ref, w2k_ref, w2v_ref,
                     kn_ref, cos_ref, sin_ref, kc_ref, vc_ref):
    hid = NSA_CMP_HIDDEN

    def mlp(a_ref, w1_ref, p_ref, w2_ref):
        P = _dot(a_ref[0, 0], w1_ref[...])
        Q = _dot(p_ref[...], w1_ref[...])
        pb = Q[0:1, :hid] + Q[1:2, hid:]
        nxt = pltpu.roll(P[:, hid:], P.shape[0] - 1, axis=0)
        hdn = _silu(P[:, :hid] + nxt + pb)
        return _dot(hdn.astype(BF16), w2_ref[...])

    kc = mlp(ak_ref, w1k_ref, pk_ref, w2k_ref)
    kc = _rope(_head_norm(kc, kn_ref[...]), cos_ref[0], sin_ref[0])
    kc_ref[0, 0] = kc.astype(BF16)
    vc_ref[0, 0] = mlp(av_ref, w1v_ref, pv_ref, w2v_ref).astype(BF16)


def nsa_compress(kcr, vcr, w1k, w1v, pk, pv, w2k, w2v, kn0, cos_end, sin_end):
    B, G, T, dh = kcr.shape
    nch = T // NSA_CMP_STRIDE
    cw = NSA_CMP_STRIDE * dh
    a_spec = pl.BlockSpec((1, 1, nch, cw), lambda b, g: (b, g, 0, 0))
    full = lambda arr: pl.BlockSpec(arr.shape, lambda b, g: (0,) * arr.ndim)
    tab_spec = pl.BlockSpec((1, nch, dh), lambda b, g: (b, 0, 0))
    o_spec = pl.BlockSpec((1, 1, nch, dh), lambda b, g: (b, g, 0, 0))
    return pl.pallas_call(
        _compress_kernel,
        grid=(B, G),
        in_specs=[a_spec, a_spec, full(w1k), full(w1v), full(pk), full(pv), full(w2k), full(w2v),
                  full(kn0), tab_spec, tab_spec],
        out_specs=[o_spec, o_spec],
        out_shape=[jax.ShapeDtypeStruct((B, G, nch, dh), BF16)] * 2,
        compiler_params=_params(("parallel", "parallel")),
        name="nsa_compress",
    )(kcr.reshape(B, G, nch, cw), vcr.reshape(B, G, nch, cw), w1k, w1v, pk, pv, w2k, w2v,
      kn0, cos_end, sin_end)


def _nsa_kernel(q_ref, kc_ref, vc_ref, ks_ref, vs_ref, kw_ref, vw_ref, gl_ref, ov_ref, ex_ref,
                o_ref, m_ref, l_ref, acc_ref, oc_ref, *, n_rep, n_cmp, n_sel, n_top, tk):
    tq = q_ref.shape[1]
    nrow = kc_ref.shape[2]
    t0 = pl.program_id(2) * tq
    gates = _sigmoid(gl_ref[0])

    def q_head(r):
        return q_ref[0, :, r * HEAD_DIM:(r + 1) * HEAD_DIM]

    kc = kc_ref[0, 0]
    vc = vc_ref[0, 0]
    jb = lax.broadcasted_iota(jnp.int32, (nrow, tq), 0)
    tcol = t0 + lax.broadcasted_iota(jnp.int32, (nrow, tq), 1)
    valid_c = (jb * NSA_CMP_STRIDE + (NSA_CMP_LEN - 1) <= tcol) & (jb < n_cmp)
    psum = jnp.zeros((nrow, tq), F32)
    for r in range(n_rep):
        sT = jnp.where(valid_c, _dot_nt(kc, q_head(r)), NEG)
        e = jnp.exp(sT - jnp.max(sT, axis=0, keepdims=True))
        p = jnp.where(valid_c, e / jnp.sum(e, axis=0, keepdims=True), 0.0)
        psum = psum + p
        oc_ref[r] = _dot(p.T.astype(BF16), vc)

    hi = psum.astype(BF16)
    lo = (psum - hi.astype(F32)).astype(BF16)
    imp = _dot(ov_ref[...], hi) + _dot(ov_ref[...], lo)
    cur = lax.shift_right_logical(tcol, int(np.log2(NSA_SEL_LEN)))
    forced = (jb == 0) | (jb == cur) | (jb == cur - 1)
    score = jnp.where(forced, NSA_FORCE, jnp.where(jb * NSA_SEL_LEN <= tcol, imp, -1.0))
    score = jnp.where(jb < n_sel, score, -2.0)
    selT = jnp.zeros((nrow, tq), F32)
    for _ in range(n_top):
        best = jnp.max(score, axis=0, keepdims=True)
        idx = jnp.min(jnp.where(score == best, jb, nrow), axis=0, keepdims=True)
        hit = jb == idx
        selT = jnp.where(hit, 1.0, selT)
        score = jnp.where(hit, -3.0, score)
    sel = selT.T.astype(BF16)

    m_ref[...] = jnp.full(m_ref.shape, NEG, F32)
    l_ref[...] = jnp.zeros(l_ref.shape, F32)
    acc_ref[...] = jnp.zeros(acc_ref.shape, F32)
    trow = t0 + lax.broadcasted_iota(jnp.int32, (tq, tk), 0)
    kcol = lax.broadcasted_iota(jnp.int32, (tq, tk), 1)

    def sel_body(kt, carry):
        k0 = pl.multiple_of(kt * tk, tk)
        k = ks_ref[0, 0, pl.ds(k0, tk), :]
        v = vs_ref[0, 0, pl.ds(k0, tk), :]
        mask = (_dot(sel, ex_ref[kt]) > 0.5) & (k0 + kcol <= trow)
        for r in range(n_rep):
            s = jnp.where(mask, _dot_nt(q_head(r), k), NEG)
            m_old = m_ref[r][:, 0:1]
            m_new = jnp.maximum(m_old, jnp.max(s, axis=-1, keepdims=True))
            alpha = jnp.exp(m_old - m_new)
            p = jnp.exp(s - m_new)
            l_ref[r] = alpha * l_ref[r] + jnp.sum(p, axis=-1, keepdims=True)
            acc_ref[r] = alpha * acc_ref[r] + _dot(p.astype(BF16), v)
            m_ref[r] = jnp.broadcast_to(m_new, (tq, LANES))
        return carry

    lax.fori_loop(0, (t0 + tq + tk - 1) // tk, sel_body, 0)

    W = NSA_WINDOW
    kstart = pl.multiple_of(jnp.maximum(t0 - W, 0), tq)
    kw = kw_ref[0, 0, pl.ds(kstart, W + tq), :]
    vw = vw_ref[0, 0, pl.ds(kstart, W + tq), :]
    rel = (t0 + lax.broadcasted_iota(jnp.int32, (tq, W + tq), 0)
           - kstart - lax.broadcasted_iota(jnp.int32, (tq, W + tq), 1))
    wmask = (rel >= 0) & (rel < W)
    for r in range(n_rep):
        s = jnp.where(wmask, _dot_nt(q_head(r), kw), NEG)
        e = jnp.exp(s - jnp.max(s, axis=-1, keepdims=True))
        o_win = _dot(e.astype(BF16), vw) / jnp.sum(e, axis=-1, keepdims=True)
        o_sel = acc_ref[r] / l_ref[r][:, 0:1]
        c = 3 * r
        o = gates[:, c:c + 1] * oc_ref[r] + gates[:, c + 1:c + 2] * o_sel + gates[:, c + 2:c + 3] * o_win
        o_ref[0, :, r * HEAD_DIM:(r + 1) * HEAD_DIM] = o.astype(BF16)


def nsa_attention(qn, kc, vc, ks, vs, kw, vw, gl3, tq=256, tk=512):
    B, T, QW = qn.shape
    G = kc.shape[1]
    n_rep = QW // HEAD_DIM // G
    nrow = kc.shape[2]
    n_cmp = (T - NSA_CMP_LEN) // NSA_CMP_STRIDE + 1
    n_sel = T // NSA_SEL_LEN
    tk = min(tk, T)
    assert T % tq == 0 and T % tk == 0 and n_sel <= nrow and T >= NSA_WINDOW + tq and tq == NSA_WINDOW
    starts = np.arange(nrow) * NSA_CMP_STRIDE
    sel_start = np.arange(nrow) * NSA_SEL_LEN
    ov = ((starts[None, :] <= sel_start[:, None] + NSA_SEL_LEN - 1)
          & (starts[None, :] + NSA_CMP_LEN - 1 >= sel_start[:, None])
          & (np.arange(nrow)[:, None] < n_sel) & (np.arange(nrow)[None, :] < n_cmp))
    ex = (np.arange(T)[None, :] // NSA_SEL_LEN == np.arange(nrow)[:, None])
    ex = ex.reshape(nrow, T // tk, tk).transpose(1, 0, 2)
    kv_spec = pl.BlockSpec((1, 1, T, HEAD_DIM), lambda b, g, i: (b, g, 0, 0))
    c_spec = pl.BlockSpec((1, 1, nrow, HEAD_DIM), lambda b, g, i: (b, g, 0, 0))
    hw = n_rep * HEAD_DIM
    return pl.pallas_call(
        functools.partial(_nsa_kernel, n_rep=n_rep, n_cmp=n_cmp, n_sel=n_sel,
                          n_top=min(NSA_TOPK, n_sel), tk=tk),
        grid=(B, G, T // tq),
        in_specs=[pl.BlockSpec((1, tq, hw), lambda b, g, i: (b, i, g)),
                  c_spec, c_spec, kv_spec, kv_spec, kv_spec, kv_spec,
                  pl.BlockSpec((1, tq, LANES), lambda b, g, i: (b, i, g)),
                  pl.BlockSpec((nrow, nrow), lambda b, g, i: (0, 0)),
                  pl.BlockSpec((T // tk, nrow, tk), lambda b, g, i: (0, 0, 0))],
        out_specs=pl.BlockSpec((1, tq, hw), lambda b, g, i: (b, i, g)),
        out_shape=jax.ShapeDtypeStruct((B, T, QW), BF16),
        scratch_shapes=[pltpu.VMEM((n_rep, tq, LANES), F32), pltpu.VMEM((n_rep, tq, LANES), F32),
                        pltpu.VMEM((n_rep, tq, HEAD_DIM), F32), pltpu.VMEM((n_rep, tq, HEAD_DIM), F32)],
        compiler_params=_params(("parallel", "parallel", "arbitrary")),
        name="nsa_attention",
    )(qn, kc, vc, ks, vs, kw, vw, gl3, jnp.asarray(ov, BF16), jnp.asarray(ex, BF16))


def _swa_kernel(sink_ref, q_ref, k_ref, v_ref, o_ref, *, n_rep):
    tq = q_ref.shape[1]
    W = SWA_WINDOW
    g = pl.program_id(1)
    t0 = pl.program_id(2) * tq
    kstart = pl.multiple_of(jnp.maximum(t0 - W, 0), W)
    k = k_ref[0, 0, pl.ds(kstart, W + tq), :]
    v = v_ref[0, 0, pl.ds(kstart, W + tq), :]
    rel = (t0 + lax.broadcasted_iota(jnp.int32, (tq, W + tq), 0)
           - kstart - lax.broadcasted_iota(jnp.int32, (tq, W + tq), 1))
    wmask = (rel >= 0) & (rel < W)
    for r in range(n_rep):
        q = q_ref[0, :, r * HEAD_DIM:(r + 1) * HEAD_DIM]
        sk = sink_ref[g * n_rep + r]
        s = jnp.where(wmask, _dot_nt(q, k), NEG)
        m = jnp.maximum(jnp.max(s, axis=-1, keepdims=True), sk)
        e = jnp.exp(s - m)
        den = jnp.sum(e, axis=-1, keepdims=True) + jnp.exp(sk - m)
        o_ref[0, :, r * HEAD_DIM:(r + 1) * HEAD_DIM] = (_dot(e.astype(BF16), v) / den).astype(BF16)


def swa_attention(qn, kn, v, sinks, tq=256):
    B, T, QW = qn.shape
    G = kn.shape[1]
    n_rep = QW // HEAD_DIM // G
    tq = min(tq, T - SWA_WINDOW)
    assert T % tq == 0 and tq % SWA_WINDOW == 0
    hw = n_rep * HEAD_DIM
    kv_spec = pl.BlockSpec((1, 1, T, HEAD_DIM), lambda b, g, i: (b, g, 0, 0))
    return pl.pallas_call(
        functools.partial(_swa_kernel, n_rep=n_rep),
        grid=(B, G, T // tq),
        in_specs=[pl.BlockSpec(memory_space=pltpu.SMEM),
                  pl.BlockSpec((1, tq, hw), lambda b, g, i: (b, i, g)), kv_spec, kv_spec],
        out_specs=pl.BlockSpec((1, tq, hw), lambda b, g, i: (b, i, g)),
        out_shape=jax.ShapeDtypeStruct((B, T, QW), BF16),
        compiler_params=_params(("parallel", "parallel", "parallel")),
        name="swa_attention",
    )(sinks, qn, kn, v)


CONF_HALO = 32


def _conformer_kernel(a1_ref, a2_ref, h1_ref, h2_ref, w_ref, b_ref, g_ref, be_ref, o_ref,
                      glu_ref, conv_ref, *, blocks_per_seq):
    tm, C = a1_ref.shape
    first = (pl.program_id(0) % blocks_per_seq) == 0
    gh = h1_ref[...] * _sigmoid(h2_ref[...])
    glu_ref[0:CONF_HALO, :] = jnp.where(first, 0.0, gh)
    glu_ref[CONF_HALO:, :] = a1_ref[...] * _sigmoid(a2_ref[...])
    base = CONF_HALO - (CONV_WIDTH - 1)
    for cs in range(C // LANES):
        sl = slice(cs * LANES, (cs + 1) * LANES)
        acc = jnp.broadcast_to(b_ref[:, sl], (tm, LANES))
        for k in range(CONV_WIDTH):
            acc = acc + w_ref[k:k + 1, sl] * glu_ref[base + k:base + k + tm, sl]
        conv_ref[:, sl] = acc
    y = conv_ref[...]
    mu = jnp.mean(y, axis=-1, keepdims=True)
    d = y - mu
    var = jnp.mean(d * d, axis=-1, keepdims=True)
    yn = d * lax.rsqrt(var + EPS) * g_ref[...] + be_ref[...]
    o_ref[...] = _silu(yn).astype(BF16)


def conformer_conv(z, conv_w, conv_b, ln_g, ln_b, T, tm=128):
    N = z.shape[0]
    C = conv_w.shape[1]
    tm = min(tm, T)
    assert N % tm == 0 and T % tm == 0 and tm % CONF_HALO == 0
    hb = tm // CONF_HALO
    w = jnp.zeros((CONF_HALO, C), F32).at[:CONV_WIDTH].set(conv_w)
    hmap = lambda c: (lambda i: (jnp.maximum(i * hb - 1, 0), c))
    vec = pl.BlockSpec((1, C), lambda i: (0, 0))
    return pl.pallas_call(
        functools.partial(_conformer_kernel, blocks_per_seq=T // tm),
        grid=(N // tm,),
        in_specs=[pl.BlockSpec((tm, C), lambda i: (i, 0)), pl.BlockSpec((tm, C), lambda i: (i, 1)),
                  pl.BlockSpec((CONF_HALO, C), hmap(0)), pl.BlockSpec((CONF_HALO, C), hmap(1)),
                  pl.BlockSpec((CONF_HALO, C), lambda i: (0, 0)), vec, vec, vec],
        out_specs=pl.BlockSpec((tm, C), lambda i: (i, 0)),
        out_shape=jax.ShapeDtypeStruct((N, C), BF16),
        scratch_shapes=[pltpu.VMEM((tm + CONF_HALO, C), F32), pltpu.VMEM((tm, C), F32)],
        compiler_params=_params(("parallel",)),
        name="conformer_conv",
    )(z, z, z, z, w, conv_b.reshape(1, C), ln_g.reshape(1, C), ln_b.reshape(1, C))


def _gmlp_kernel(u0_ref, u1_ref, v0_ref, v1_ref, g_ref, be_ref, ws_ref, bs_ref, o_ref, *, n_groups):
    tm, hw = v0_ref.shape
    C = 2 * hw
    ch = GMLP_CHUNK
    gv = [_gelu_tanh(v0_ref[...]), _gelu_tanh(v1_ref[...])]
    mu = (jnp.sum(gv[0], axis=-1, keepdims=True) + jnp.sum(gv[1], axis=-1, keepdims=True)) / C
    d = [gv[0] - mu, gv[1] - mu]
    var = (jnp.sum(d[0] * d[0], axis=-1, keepdims=True) + jnp.sum(d[1] * d[1], axis=-1, keepdims=True)) / C
    rs = lax.rsqrt(var + EPS)
    vn = [(d[hf] * rs * g_ref[:, hf * hw:(hf + 1) * hw] + be_ref[:, hf * hw:(hf + 1) * hw]).astype(BF16)
          for hf in range(2)]
    u_refs = [u0_ref, u1_ref]
    tril = lax.broadcasted_iota(jnp.int32, (ch, ch), 0) >= lax.broadcasted_iota(jnp.int32, (ch, ch), 1)
    gph = n_groups // 2
    for gi in range(n_groups):
        hf, col = gi // gph, (gi % gph) * HEAD_DIM
        w = jnp.where(tril, ws_ref[gi], 0.0).astype(BF16)
        for c in range(tm // ch):
            rows = slice(c * ch, (c + 1) * ch)
            sp = _dot(w, vn[hf][rows, col:col + HEAD_DIM]) + bs_ref[gi]
            u = _gelu_tanh(u_refs[hf][rows, col:col + HEAD_DIM])
            o_ref[rows, gi * HEAD_DIM:(gi + 1) * HEAD_DIM] = (u * sp).astype(BF16)


def chunked_gmlp(z, u_col, ln_g, ln_b, w_s, b_s, T, tm=256):
    N = z.shape[0]
    n_groups, ch, _ = w_s.shape
    C = n_groups * HEAD_DIM
    hw = C // 2
    tm = min(tm, T)
    assert N % tm == 0 and T % tm == 0 and tm % ch == 0 and u_col % hw == 0
    cb = u_col // hw
    bsb = jnp.broadcast_to(b_s[:, :, None], (n_groups, ch, HEAD_DIM))
    zspec = lambda k: pl.BlockSpec((tm, hw), lambda i: (i, cb + k))
    vec = pl.BlockSpec((1, C), lambda i: (0, 0))
    return pl.pallas_call(
        functools.partial(_gmlp_kernel, n_groups=n_groups),
        grid=(N // tm,),
        in_specs=[zspec(0), zspec(1), zspec(2), zspec(3), vec, vec,
                  pl.BlockSpec((n_groups, ch, ch), lambda i: (0, 0, 0)),
                  pl.BlockSpec((n_groups, ch, HEAD_DIM), lambda i: (0, 0, 0))],
        out_specs=pl.BlockSpec((tm, C), lambda i: (i, 0)),
        out_shape=jax.ShapeDtypeStruct((N, C), BF16),
        compiler_params=_params(("parallel",)),
        name="chunked_gmlp",
    )(z, z, z, z, ln_g.reshape(1, C), ln_b.reshape(1, C), w_s, bsb)


def _even_mixer(xf, B, T, g, sc, sh, cosf, sinf, w_in, w_out, conv_w, conv_b, conv_ln_g, conv_ln_b,
                q_norm, k_norm, cmp_k_pos, cmp_k_w1, cmp_k_w2, cmp_v_pos, cmp_v_w1, cmp_v_w2):
    D = xf.shape[1]
    G = NSA_KV_GROUPS
    C = conv_w.shape[1]
    QW = D // 2
    KW = G * HEAD_DIM
    n_rep = QW // HEAD_DIM // G
    main = 2 * C + QW + 6 * KW
    wg = w_in[:, main:].reshape(D, G, n_rep * 3)
    wg = jnp.pad(wg, ((0, 0), (0, 0), (0, LANES - n_rep * 3))).reshape(D, G * LANES)
    z, gl = in_proj(xf, g, sc, sh, w_in[:, :main].astype(BF16), wg.astype(BF16), T)
    ya = conformer_conv(z, conv_w, conv_b, conv_ln_g, conv_ln_b, T)
    q_col = 2 * C
    kv_cols = [q_col + QW + k * KW for k in range(6)]
    qn, kcr, vcr, ks, vs, kw, vw = attn_prep(z.reshape(B, T, main), cosf, sinf, q_norm, k_norm,
                                             q_col, kv_cols, (-1, -1, 1, -1, 2, -1), G)
    half = NSA_CMP_STRIDE * HEAD_DIM
    hid = NSA_CMP_HIDDEN
    nch = T // NSA_CMP_STRIDE

    def w1cat(w1):
        return jnp.concatenate([w1[:half], w1[half:]], axis=1).astype(BF16)

    def posrows(pos):
        return jnp.zeros((8, half), F32).at[0].set(pos[:NSA_CMP_STRIDE].reshape(half)) \
                  .at[1].set(pos[NSA_CMP_STRIDE:].reshape(half)).astype(BF16)

    ends = np.arange(nch) * NSA_CMP_STRIDE + NSA_CMP_LEN - 1
    ends = np.minimum(ends, T - 1)
    kc, vc = nsa_compress(kcr, vcr, w1cat(cmp_k_w1), w1cat(cmp_v_w1), posrows(cmp_k_pos), posrows(cmp_v_pos),
                          cmp_k_w2.astype(BF16), cmp_v_w2.astype(BF16), k_norm[0:1],
                          cosf[:, ends], sinf[:, ends])
    yb = nsa_attention(qn, kc, vc, ks, vs, kw, vw, gl.reshape(B, T, G * LANES))
    return ya, yb.reshape(B * T, QW), w_out.astype(BF16)


def _odd_mixer(xf, B, T, g, sc, sh, cosf, sinf, w_in, w_out, q_norm, k_norm, sinks, ln_g, ln_b, w_s, b_s):
    D = xf.shape[1]
    G = SWA_KV_HEADS
    QW = D // 2
    KW = G * HEAD_DIM
    z, = in_proj(xf, g, sc, sh, w_in.astype(BF16), None, T)
    qn, kn, v = attn_prep(z.reshape(B, T, z.shape[1]), cosf, sinf, q_norm, k_norm.reshape(1, HEAD_DIM),
                          0, [QW, QW + KW], (0, -1), G)
    yc = swa_attention(qn, kn, v, sinks)
    yd = chunked_gmlp(z, QW + 2 * KW, ln_g, ln_b, w_s, b_s, T)
    return yc.reshape(B * T, QW), yd, w_out.astype(BF16)


def kernel(x, c, positions, ada_w, ada_b, norm_g, ffn_w_up, ffn_conv_w, ffn_conv_b, ffn_w_down, ev_w_in, ev_w_out, ev_conv_w, ev_conv_b, ev_conv_ln_g, ev_conv_ln_b, ev_q_norm, ev_k_norm, ev_cmp_k_pos, ev_cmp_k_w1, ev_cmp_k_w2, ev_cmp_v_pos, ev_cmp_v_w1, ev_cmp_v_w2, od_w_in, od_w_out, od_q_norm, od_k_norm, od_sinks, od_gmlp_ln_g, od_gmlp_ln_b, od_gmlp_w_s, od_gmlp_b_s):
    B, T, D = x.shape
    depth = ada_w.shape[0]
    cosf, sinf = rope_tables(positions)
    mod = adaln(c, ada_w, ada_b)
    xf = x.reshape(B * T, D)
    for i in range(depth):
        sh1, sc1, g1, sh2, sc2, g2 = [m.reshape(B, 1, D) for m in jnp.split(mod[i], 6, axis=-1)]
        j = i // 2
        if i % 2 == 0:
            ya, yb, w_out = _even_mixer(xf, B, T, norm_g[i, 0], sc1, sh1, cosf, sinf, ev_w_in[j], ev_w_out[j],
                                        ev_conv_w[j], ev_conv_b[j], ev_conv_ln_g[j], ev_conv_ln_b[j],
                                        ev_q_norm[j], ev_k_norm[j], ev_cmp_k_pos[j], ev_cmp_k_w1[j],
                                        ev_cmp_k_w2[j], ev_cmp_v_pos[j], ev_cmp_v_w1[j], ev_cmp_v_w2[j])
        else:
            ya, yb, w_out = _odd_mixer(xf, B, T, norm_g[i, 0], sc1, sh1, cosf, sinf, od_w_in[j], od_w_out[j],
                                       od_q_norm[j], od_k_norm[j], od_sinks[j], od_gmlp_ln_g[j],
                                       od_gmlp_ln_b[j], od_gmlp_w_s[j], od_gmlp_b_s[j])
        xf = out_proj(ya, yb, w_out, xf, g1, T)
        cw = jnp.zeros((1, 8, ffn_conv_w.shape[2]), F32).at[0, :FFN_CONV_WIDTH].set(ffn_conv_w[i]) \
                .at[0, FFN_CONV_WIDTH].set(ffn_conv_b[i])
        xf = conv_ffn(xf, norm_g[i, 1], sc2, sh2, g2, ffn_w_up[i].astype(BF16), cw,
                      ffn_w_down[i].astype(BF16), T)
    return xf.reshape(B, T, D)
```

```python
import functools

import numpy as np
import jax
import jax.numpy as jnp
from jax import lax
from jax.experimental import pallas as pl
from jax.experimental.pallas import tpu as pltpu

F32 = jnp.float32
BF16 = jnp.bfloat16

HEAD_DIM = 128
ROPE_THETA = 10000.0
EPS = 1e-6
NEG = -1e30

CONV_WIDTH = 31
NSA_KV_GROUPS = 2
NSA_CMP_LEN = 32
NSA_CMP_STRIDE = 16
NSA_CMP_HIDDEN = 256
NSA_SEL_LEN = 64
NSA_TOPK = 8
NSA_WINDOW = 256
NSA_FORCE = 1e6
SWA_KV_HEADS = 2
SWA_WINDOW = 128
GMLP_CHUNK = 128
FFN_CONV_WIDTH = 3

V7X_VMEM_BYTES = 64 * 1024 * 1024
VMEM_LIMIT = V7X_VMEM_BYTES - 8 * 1024 * 1024
LANES = 128
BF16_SUBLANES = 16


def _params(sem):
    return pltpu.CompilerParams(dimension_semantics=sem, vmem_limit_bytes=VMEM_LIMIT)


def _dot(a, b):
    return jnp.dot(a, b, preferred_element_type=F32)


def _dot_nt(a, b):
    return lax.dot_general(a, b, (((1,), (1,)), ((), ())), preferred_element_type=F32)


def _sigmoid(x):
    return 1.0 / (1.0 + jnp.exp(-x))


def _silu(x):
    return x * _sigmoid(x)


def _gelu_tanh(x):
    return 0.5 * x * (1.0 + jnp.tanh(np.sqrt(2.0 / np.pi).astype(np.float32) * (x + 0.044715 * (x * x * x))))


def _rope_kernel(pos_ref, inv_ref, cos_ref, sin_ref):
    ang = pos_ref[0].astype(F32) * inv_ref[0:1, :]
    cos_ref[0] = jnp.cos(ang)
    sin_ref[0] = jnp.sin(ang) * inv_ref[1:2, :]


def rope_tables(positions):
    B, T = positions.shape
    inv = ROPE_THETA ** (-jnp.arange(0, HEAD_DIM, 2, dtype=F32) / HEAD_DIM)
    half = HEAD_DIM // 2
    sign = jnp.concatenate([-jnp.ones((half,), F32), jnp.ones((half,), F32)])
    tab = jnp.stack([jnp.concatenate([inv, inv]), sign])
    return pl.pallas_call(
        _rope_kernel,
        grid=(B,),
        in_specs=[pl.BlockSpec((1, T, 1), lambda b: (b, 0, 0)),
                  pl.BlockSpec((2, HEAD_DIM), lambda b: (0, 0))],
        out_specs=[pl.BlockSpec((1, T, HEAD_DIM), lambda b: (b, 0, 0))] * 2,
        out_shape=[jax.ShapeDtypeStruct((B, T, HEAD_DIM), F32)] * 2,
        compiler_params=_params(("parallel",)),
        name="rope_tables",
    )(positions.reshape(B, T, 1), tab)


def _adaln_kernel(c_ref, w_ref, b_ref, o_ref):
    ca = _silu(c_ref[...]).astype(BF16)
    o_ref[0] = _dot(ca, w_ref[0].astype(BF16)) + b_ref[0]


def adaln(c, ada_w, ada_b, tn=1024):
    L, D, N6 = ada_w.shape
    B = c.shape[0]
    return pl.pallas_call(
        _adaln_kernel,
        grid=(L, N6 // tn),
        in_specs=[pl.BlockSpec((B, D), lambda l, j: (0, 0)),
                  pl.BlockSpec((1, D, tn), lambda l, j: (l, 0, j)),
                  pl.BlockSpec((1, 1, tn), lambda l, j: (l, 0, j))],
        out_specs=pl.BlockSpec((1, B, tn), lambda l, j: (l, 0, j)),
        out_shape=jax.ShapeDtypeStruct((L, B, N6), F32),
        compiler_params=_params(("parallel", "parallel")),
        name="adaln",
    )(c, ada_w, ada_b.reshape(L, 1, N6))


def _norm_mod_rows(x_ref, g, scale1, shift, out_ref, out_row0, nrows, chunk):
    def body(ci, carry):
        r = pl.multiple_of(ci * chunk, chunk)
        x = x_ref[pl.ds(r, chunk), :]
        ms = jnp.mean(x * x, axis=-1, keepdims=True)
        y = x * lax.rsqrt(ms + EPS) * g
        out_ref[pl.ds(out_row0 + r, chunk), :] = (y * scale1 + shift).astype(BF16)
        return carry
    lax.fori_loop(0, nrows // chunk, body, 0)


def _inproj_kernel(x_ref, g_ref, sc_ref, sh_ref, w_ref, *rest, has_gate, chunk):
    if has_gate:
        wg_ref, o_ref, og_ref, h_ref = rest
    else:
        o_ref, h_ref = rest

    @pl.when(pl.program_id(1) == 0)
    def _():
        _norm_mod_rows(x_ref, g_ref[...], 1.0 + sc_ref[0], sh_ref[0], h_ref, 0, x_ref.shape[0], chunk)
        if has_gate:
            og_ref[...] = _dot(h_ref[...], wg_ref[...])

    o_ref[...] = _dot(h_ref[...], w_ref[...])


def in_proj(x, g, sc, sh, w, wg, T, tm=1024, tn=512):
    N, D = x.shape
    Nout = w.shape[1]
    tm = min(tm, T)
    assert N % tm == 0 and T % tm == 0 and Nout % tn == 0
    bmap = lambda i, j: ((i * tm) // T, 0, 0)
    in_specs = [pl.BlockSpec((tm, D), lambda i, j: (i, 0)),
                pl.BlockSpec((1, D), lambda i, j: (0, 0)),
                pl.BlockSpec((1, 1, D), bmap),
                pl.BlockSpec((1, 1, D), bmap),
                pl.BlockSpec((D, tn), lambda i, j: (0, j))]
    out_specs = [pl.BlockSpec((tm, tn), lambda i, j: (i, j))]
    out_shape = [jax.ShapeDtypeStruct((N, Nout), F32)]
    args = [x, g.reshape(1, D), sc, sh, w]
    if wg is not None:
        ng = wg.shape[1]
        in_specs.append(pl.BlockSpec((D, ng), lambda i, j: (0, 0)))
        out_specs.append(pl.BlockSpec((tm, ng), lambda i, j: (i, 0)))
        out_shape.append(jax.ShapeDtypeStruct((N, ng), F32))
        args.append(wg)
    return pl.pallas_call(
        functools.partial(_inproj_kernel, has_gate=wg is not None, chunk=min(128, tm)),
        grid=(N // tm, Nout // tn),
        in_specs=in_specs, out_specs=out_specs, out_shape=out_shape,
        scratch_shapes=[pltpu.VMEM((tm, D), BF16)],
        compiler_params=_params(("parallel", "arbitrary")),
        name="in_proj",
    )(*args)


def _outproj_kernel(ya_ref, yb_ref, wa_ref, wb_ref, x_ref, gate_ref, o_ref):
    y = _dot(ya_ref[...], wa_ref[...]) + _dot(yb_ref[...], wb_ref[...])
    o_ref[...] = x_ref[...] + gate_ref[0] * y


def out_proj(ya, yb, w_out, x, gate, T, tm=1024, tn=512):
    N, D = x.shape
    Ka, Kb = ya.shape[1], yb.shape[1]
    tm = min(tm, T)
    assert N % tm == 0 and T % tm == 0 and D % tn == 0 and Ka % tn == 0
    return pl.pallas_call(
        _outproj_kernel,
        grid=(N // tm, D // tn),
        in_specs=[pl.BlockSpec((tm, Ka), lambda i, j: (i, 0)),
                  pl.BlockSpec((tm, Kb), lambda i, j: (i, 0)),
                  pl.BlockSpec((Ka, tn), lambda i, j: (0, j)),
                  pl.BlockSpec((Kb, tn), lambda i, j: (Ka // Kb, j)),
                  pl.BlockSpec((tm, tn), lambda i, j: (i, j)),
                  pl.BlockSpec((1, 1, tn), lambda i, j: ((i * tm) // T, 0, j))],
        out_specs=pl.BlockSpec((tm, tn), lambda i, j: (i, j)),
        out_shape=jax.ShapeDtypeStruct((N, D), F32),
        compiler_params=_params(("parallel", "parallel")),
        name="out_proj",
    )(ya, yb, w_out, w_out, x, gate)


FFN_HALO = BF16_SUBLANES


def _ffn_kernel(x_ref, xh_ref, g_ref, sc_ref, sh_ref, gate_ref, perm_ref, wa_ref, wb_ref, cwa_ref, cwb_ref,
                wd_ref, o_ref, h_ref, hn_ref, acc_ref, ya_ref, yb_ref, *, blocks_per_seq, chunk, sub):
    i = pl.program_id(0)
    j = pl.program_id(1)
    tm = x_ref.shape[0]

    K = tm // 8

    def norm_mod(xr, g, scale1, shift):
        ms = jnp.mean(xr * xr, axis=-1, keepdims=True)
        return (xr * lax.rsqrt(ms + EPS) * g) * scale1 + shift

    @pl.when(j == 0)
    def _():
        g = g_ref[...]
        scale1 = 1.0 + sc_ref[0]
        shift = sh_ref[0]
        _norm_mod_rows(x_ref, g, scale1, shift, hn_ref, 0, tm, chunk)
        D = x_ref.shape[1]
        for c0 in range(0, D, 512):
            h_ref[FFN_HALO:, c0:c0 + 512] = _dot(perm_ref[...], hn_ref[:, c0:c0 + 512]).astype(BF16)
        first = (i % blocks_per_seq) == 0
        h_ref[0:FFN_HALO, :] = jnp.where(first, 0.0, norm_mod(xh_ref[...], g, scale1, shift)).astype(BF16)
        acc_ref[...] = jnp.zeros_like(acc_ref)

    h = h_ref[...]
    tf = wa_ref.shape[1]
    nsub = tf // sub
    sub0 = lax.broadcasted_iota(jnp.int32, (8, sub), 0) == 0

    def conv(y_ref, cw_ref, cs):
        H = FFN_HALO
        last = pltpu.roll(y_ref[H + tm - 8:H + tm, :], 1, axis=0)
        last2 = pltpu.roll(y_ref[H + tm - 16:H + tm - 8, :], 1, axis=0)
        m1 = jnp.where(sub0, y_ref[H - 1:H, :], last)
        m2 = jnp.where(sub0, y_ref[H - 2:H - 1, :], last2)
        y_ref[H - 8:H, :] = m1
        y_ref[H - 16:H - 8, :] = m2
        cw = cw_ref[0, :, cs]
        out = cw[FFN_CONV_WIDTH:FFN_CONV_WIDTH + 1, :]
        for k in range(FFN_CONV_WIDTH):
            off = H - 8 * (FFN_CONV_WIDTH - 1 - k)
            out = out + cw[k:k + 1, :] * y_ref[off:off + tm, :]
        return out

    for c in range(nsub):
        cs = slice(c * sub, (c + 1) * sub)
        ya_ref[c] = _dot(h, wa_ref[:, cs])
        yb_ref[c] = _dot(h, wb_ref[:, cs])
    for c in range(nsub):
        cs = slice(c * sub, (c + 1) * sub)
        act = (_silu(conv(ya_ref.at[c], cwa_ref, cs)) * conv(yb_ref.at[c], cwb_ref, cs)).astype(BF16)
        d = _dot(act, wd_ref[cs, :])
        for cb in range(acc_ref.shape[0]):
            acc_ref[cb] += d[:, cb * LANES:(cb + 1) * LANES]

    @pl.when(j == pl.num_programs(1) - 1)
    def _():
        for m in range(tm // 8):
            rows = slice(8 * m, 8 * m + 8)
            start = 8 * ((8 * m) % K) + (8 * m) // K
            for cb in range(acc_ref.shape[0]):
                cols = slice(cb * LANES, (cb + 1) * LANES)
                o_ref[rows, cols] = (x_ref[rows, cols]
                                     + gate_ref[0, :, cols] * acc_ref[cb, pl.ds(start, 8, stride=8), :])


def conv_ffn(x, g, sc, sh, gate, w_up, cw, w_down, T, tm=512, tf=512, sub=256):
    N, D = x.shape
    DFF = w_down.shape[0]
    tm = min(tm, T)
    assert N % tm == 0 and T % tm == 0 and DFF % tf == 0 and tm % FFN_HALO == 0
    sub = min(sub, tf)
    nff = DFF // tf
    hb = tm // FFN_HALO
    bmap = lambda i, j: ((i * tm) // T, 0, 0)
    p = np.arange(tm)
    perm = (p % 8) * (tm // 8) + p // 8
    perm = perm[:, None] == np.arange(tm)[None, :]
    return pl.pallas_call(
        functools.partial(_ffn_kernel, blocks_per_seq=T // tm, chunk=min(128, tm), sub=min(sub, tf)),
        grid=(N // tm, nff),
        in_specs=[pl.BlockSpec((tm, D), lambda i, j: (i, 0)),
                  pl.BlockSpec((FFN_HALO, D), lambda i, j: (jnp.maximum(i * hb - 1, 0), 0)),
                  pl.BlockSpec((1, D), lambda i, j: (0, 0)),
                  pl.BlockSpec((1, 1, D), bmap),
                  pl.BlockSpec((1, 1, D), bmap),
                  pl.BlockSpec((1, 1, D), bmap),
                  pl.BlockSpec((tm, tm), lambda i, j: (0, 0)),
                  pl.BlockSpec((D, tf), lambda i, j: (0, j)),
                  pl.BlockSpec((D, tf), lambda i, j: (0, j + nff)),
                  pl.BlockSpec((1, 8, tf), lambda i, j: (0, 0, j)),
                  pl.BlockSpec((1, 8, tf), lambda i, j: (0, 0, j + nff)),
                  pl.BlockSpec((tf, D), lambda i, j: (j, 0))],
        out_specs=pl.BlockSpec((tm, D), lambda i, j: (i, 0)),
        out_shape=jax.ShapeDtypeStruct((N, D), F32),
        scratch_shapes=[pltpu.VMEM((tm + FFN_HALO, D), BF16), pltpu.VMEM((tm, D), BF16),
                        pltpu.VMEM((D // LANES, tm, LANES), F32),
                        pltpu.VMEM((tf // sub, tm + FFN_HALO, sub), F32),
                        pltpu.VMEM((tf // sub, tm + FFN_HALO, sub), F32)],
        compiler_params=_params(("parallel", "arbitrary")),
        name="conv_ffn",
    )(x, x, g.reshape(1, D), sc, sh, gate, jnp.asarray(perm, BF16), w_up, w_up, cw, cw, w_down)


KV_PLAIN = -1
KV_TRANSPOSED = -2
KV_TILE = 256
SEL_UNROLL = 2


def _rope(y, cos, sin):
    return y * cos + pltpu.roll(y, HEAD_DIM // 2, axis=1) * sin


def _head_norm(x, g):
    ms = jnp.mean(x * x, axis=-1, keepdims=True)
    return x * lax.rsqrt(ms + EPS) * g


def _prep_kernel(*refs, n_q_heads, kv_kinds, n_groups):
    nkv = len(kv_kinds)
    zq_ref = refs[0]
    kv_refs = refs[1:1 + nkv]
    cos_ref, sin_ref, qn_ref, kn_ref = refs[1 + nkv:5 + nkv]
    q_out = refs[5 + nkv]
    kv_out = refs[6 + nkv:]
    cos = cos_ref[0]
    sin = sin_ref[0]
    scale = HEAD_DIM ** -0.5
    for hd in range(n_q_heads):
        sl = slice(hd * HEAD_DIM, (hd + 1) * HEAD_DIM)
        y = _rope(_head_norm(zq_ref[0, :, sl], qn_ref[...]), cos, sin)
        q_out[0, :, sl] = (y * scale).astype(BF16)
    for idx, kind in enumerate(kv_kinds):
        for gi in range(n_groups):
            sl = slice(gi * HEAD_DIM, (gi + 1) * HEAD_DIM)
            a = kv_refs[idx][0, :, sl]
            if kind >= 0:
                a = _rope(_head_norm(a, kn_ref[kind:kind + 1, :]), cos, sin)
            if kind == KV_TRANSPOSED:
                for c in range(a.shape[0] // KV_TILE):
                    kv_out[idx][0, gi, c] = a[c * KV_TILE:(c + 1) * KV_TILE, :].T.astype(BF16)
            else:
                kv_out[idx][0, gi] = a.astype(BF16)


def attn_prep(z3, cosf, sinf, q_norm, k_norm, q_col, kv_cols, kv_kinds, n_groups, tm=512):
    B, T, _ = z3.shape
    tm = min(tm, T)
    QW = 8 * HEAD_DIM
    KW = n_groups * HEAD_DIM
    assert q_col % QW == 0 and all(c % KW == 0 for c in kv_cols) and T % tm == 0
    nkv = len(kv_cols)
    in_specs = [pl.BlockSpec((1, tm, QW), lambda b, t: (b, t, q_col // QW))]
    for c in kv_cols:
        in_specs.append(pl.BlockSpec((1, tm, KW), functools.partial(lambda b, t, cb: (b, t, cb), cb=c // KW)))
    in_specs += [pl.BlockSpec((1, tm, HEAD_DIM), lambda b, t: (b, t, 0))] * 2
    in_specs += [pl.BlockSpec((1, HEAD_DIM), lambda b, t: (0, 0)),
                 pl.BlockSpec(k_norm.shape, lambda b, t: (0, 0))]
    out_specs = [pl.BlockSpec((1, tm, QW), lambda b, t: (b, t, 0))]
    out_shape = [jax.ShapeDtypeStruct((B, T, QW), BF16)]
    for kind in kv_kinds:
        if kind == KV_TRANSPOSED:
            assert tm % KV_TILE == 0
            out_specs.append(pl.BlockSpec((1, n_groups, tm // KV_TILE, HEAD_DIM, KV_TILE),
                                          lambda b, t: (b, 0, t, 0, 0)))
            out_shape.append(jax.ShapeDtypeStruct((B, n_groups, T // KV_TILE, HEAD_DIM, KV_TILE), BF16))
        else:
            out_specs.append(pl.BlockSpec((1, n_groups, tm, HEAD_DIM), lambda b, t: (b, 0, t, 0)))
            out_shape.append(jax.ShapeDtypeStruct((B, n_groups, T, HEAD_DIM), BF16))
    return pl.pallas_call(
        functools.partial(_prep_kernel, n_q_heads=QW // HEAD_DIM, kv_kinds=tuple(kv_kinds), n_groups=n_groups),
        grid=(B, T // tm),
        in_specs=in_specs, out_specs=out_specs, out_shape=out_shape,
        compiler_params=_params(("parallel", "parallel")),
        name="attn_prep",
    )(*([z3] * (1 + nkv)), cosf, sinf, q_norm.reshape(1, HEAD_DIM), k_norm)


def _compress_kernel(ak_ref, av_ref, w1k_ref, w1v_ref, pk_ref, pv_ref, w2k_ref, w2v_ref,
                     kn_ref, cos_ref, sin_ref, kc_ref, vc_ref):
    hid = NSA_CMP_HIDDEN

    def mlp(a_ref, w1_ref, p_ref, w2_ref):
        P = _dot(a_ref[0, 0], w1_ref[...])
        Q = _dot(p_ref[...], w1_ref[...])
        pb = Q[0:1, :hid] + Q[1:2, hid:]
        nxt = pltpu.roll(P[:, hid:], P.shape[0] - 1, axis=0)
        hdn = _silu(P[:, :hid] + nxt + pb)
        return _dot(hdn.astype(BF16), w2_ref[...])

    kc = mlp(ak_ref, w1k_ref, pk_ref, w2k_ref)
    kc = _rope(_head_norm(kc, kn_ref[...]), cos_ref[0], sin_ref[0])
    kc_ref[0, 0] = kc.astype(BF16)
    vc_ref[0, 0] = mlp(av_ref, w1v_ref, pv_ref, w2v_ref).T.astype(BF16)


def nsa_compress(kcr, vcr, w1k, w1v, pk, pv, w2k, w2v, kn0, cos_end, sin_end):
    B, G, T, dh = kcr.shape
    nch = T // NSA_CMP_STRIDE
    cw = NSA_CMP_STRIDE * dh
    a_spec = pl.BlockSpec((1, 1, nch, cw), lambda b, g: (b, g, 0, 0))
    full = lambda arr: pl.BlockSpec(arr.shape, lambda b, g: (0,) * arr.ndim)
    tab_spec = pl.BlockSpec((1, nch, dh), lambda b, g: (b, 0, 0))
    o_spec = pl.BlockSpec((1, 1, nch, dh), lambda b, g: (b, g, 0, 0))
    return pl.pallas_call(
        _compress_kernel,
        grid=(B, G),
        in_specs=[a_spec, a_spec, full(w1k), full(w1v), full(pk), full(pv), full(w2k), full(w2v),
                  full(kn0), tab_spec, tab_spec],
        out_specs=[o_spec, pl.BlockSpec((1, 1, dh, nch), lambda b, g: (b, g, 0, 0))],
        out_shape=[jax.ShapeDtypeStruct((B, G, nch, dh), BF16), jax.ShapeDtypeStruct((B, G, dh, nch), BF16)],
        compiler_params=_params(("parallel", "parallel")),
        name="nsa_compress",
    )(kcr.reshape(B, G, nch, cw), vcr.reshape(B, G, nch, cw), w1k, w1v, pk, pv, w2k, w2v,
      kn0, cos_end, sin_end)


def _nsa_kernel(q_ref, kc_ref, vct_ref, ks_ref, vst_ref, kw_ref, vwt_ref, gl_ref, ov_ref, ext_ref,
                o_ref, m_ref, l_ref, acc_ref, *, n_rep, n_cmp, n_sel, n_top):
    tq = q_ref.shape[1]
    tk = KV_TILE
    nrow = kc_ref.shape[2]
    t0 = pl.program_id(2) * tq
    q4 = jnp.concatenate([q_ref[0, :, r * HEAD_DIM:(r + 1) * HEAD_DIM] for r in range(n_rep)], axis=0)

    def heads(a):
        return jnp.concatenate([a] * n_rep, axis=1)

    jb = lax.broadcasted_iota(jnp.int32, (nrow, tq), 0)
    tcol = t0 + lax.broadcasted_iota(jnp.int32, (nrow, tq), 1)
    valid_c = (jb * NSA_CMP_STRIDE + (NSA_CMP_LEN - 1) <= tcol) & (jb < n_cmp)
    validf = heads(jnp.where(valid_c, 1.0, 0.0))
    s = _dot_nt(kc_ref[0, 0], q4) + heads(jnp.where(valid_c, 0.0, NEG))
    e = jnp.exp(s - jnp.max(s, axis=0, keepdims=True))
    p = e / jnp.sum(e, axis=0, keepdims=True) * validf
    o_cmp = _dot(vct_ref[0, 0], p.astype(BF16))
    psum = p[:, 0:tq]
    for r in range(1, n_rep):
        psum = psum + p[:, r * tq:(r + 1) * tq]

    hi = psum.astype(BF16)
    lo = (psum - hi.astype(F32)).astype(BF16)
    imp = _dot(ov_ref[...], hi) + _dot(ov_ref[...], lo)
    cur = lax.shift_right_logical(tcol, int(np.log2(NSA_SEL_LEN)))
    forced = (jb == 0) | (jb == cur) | (jb == cur - 1)
    score = jnp.where(forced, NSA_FORCE, jnp.where(jb * NSA_SEL_LEN <= tcol, imp, -1.0))
    score = jnp.where(jb < n_sel, score, -2.0)
    selT = jnp.zeros((nrow, tq), F32)
    for _ in range(n_top):
        best = jnp.max(score, axis=0, keepdims=True)
        idx = jnp.min(jnp.where(score == best, jb, nrow), axis=0, keepdims=True)
        hit = jb == idx
        selT = jnp.where(hit, 1.0, selT)
        score = jnp.where(hit, -3.0, score)
    sel = selT.astype(BF16)

    m_ref[...] = jnp.full(m_ref.shape, NEG, F32)
    l_ref[...] = jnp.zeros(l_ref.shape, F32)
    acc_ref[...] = jnp.zeros(acc_ref.shape, F32)
    krow = lax.broadcasted_iota(jnp.int32, (tk, tq), 0)
    tq_col = t0 + lax.broadcasted_iota(jnp.int32, (tk, tq), 1)

    def sel_body(it, carry):
        s = []
        for u in range(SEL_UNROLL):
            kt = it * SEL_UNROLL + u
            k0 = pl.multiple_of(kt * tk, tk)
            mask = (_dot(ext_ref[kt], sel) > 0.5) & (k0 + krow <= tq_col)
            s.append(_dot_nt(ks_ref[0, 0, pl.ds(k0, tk), :], q4) + heads(jnp.where(mask, 0.0, NEG)))
        m_old = m_ref[...]
        m_new = m_old
        for u in range(SEL_UNROLL):
            m_new = jnp.maximum(m_new, jnp.max(s[u], axis=0, keepdims=True))
        alpha = jnp.exp(m_old - m_new)
        lsum = alpha * l_ref[...]
        pv = alpha * acc_ref[...]
        for u in range(SEL_UNROLL):
            p = jnp.exp(s[u] - m_new)
            lsum = lsum + jnp.sum(p, axis=0, keepdims=True)
            pv = pv + _dot(vst_ref[0, 0, it * SEL_UNROLL + u], p.astype(BF16))
        l_ref[...] = lsum
        acc_ref[...] = pv
        m_ref[...] = m_new
        return carry

    n_tiles = (t0 + tq) // tk
    lax.fori_loop(0, (n_tiles + SEL_UNROLL - 1) // SEL_UNROLL, sel_body, 0)
    o_sel = acc_ref[...] / l_ref[...]

    W = NSA_WINDOW
    kstart = pl.multiple_of(jnp.maximum(t0 - W, 0), tk)
    jt = kstart // tk
    sw = []
    for c in range(2):
        rel = tq_col - (kstart + c * tk) - krow
        bias = heads(jnp.where((rel >= 0) & (rel < W), 0.0, NEG))
        sw.append(_dot_nt(kw_ref[0, 0, pl.ds(kstart + c * tk, tk), :], q4) + bias)
    m = jnp.maximum(jnp.max(sw[0], axis=0, keepdims=True), jnp.max(sw[1], axis=0, keepdims=True))
    ew = [jnp.exp(sw[c] - m) for c in range(2)]
    den = jnp.sum(ew[0], axis=0, keepdims=True) + jnp.sum(ew[1], axis=0, keepdims=True)
    o_win = (_dot(vwt_ref[0, 0, jt], ew[0].astype(BF16)) + _dot(vwt_ref[0, 0, jt + 1], ew[1].astype(BF16))) / den

    gT = _sigmoid(gl_ref[0]).T
    for r in range(n_rep):
        hs = slice(r * tq, (r + 1) * tq)
        c3 = 3 * r
        oT = (gT[c3:c3 + 1, :] * o_cmp[:, hs] + gT[c3 + 1:c3 + 2, :] * o_sel[:, hs]
              + gT[c3 + 2:c3 + 3, :] * o_win[:, hs])
        o_ref[0, :, r * HEAD_DIM:(r + 1) * HEAD_DIM] = oT.T.astype(BF16)


def _nsa_kernel_v2(q_ref, kc_ref, vct_ref, ks_ref, vst_ref, kw_ref, vwt_ref, gl_ref, ov_ref, ext_ref,
                o_ref, m_ref, l_ref, acc_ref, oc_ref, *, n_rep, n_cmp, n_sel, n_top):
    tq = q_ref.shape[1]
    tk = KV_TILE
    nrow = kc_ref.shape[2]
    t0 = pl.program_id(2) * tq

    def q_head(r):
        return q_ref[0, :, r * HEAD_DIM:(r + 1) * HEAD_DIM]

    kc = kc_ref[0, 0]
    vct = vct_ref[0, 0]
    jb = lax.broadcasted_iota(jnp.int32, (nrow, tq), 0)
    tcol = t0 + lax.broadcasted_iota(jnp.int32, (nrow, tq), 1)
    valid_c = (jb * NSA_CMP_STRIDE + (NSA_CMP_LEN - 1) <= tcol) & (jb < n_cmp)
    psum = jnp.zeros((nrow, tq), F32)
    for r in range(n_rep):
        sT = jnp.where(valid_c, _dot_nt(kc, q_head(r)), NEG)
        e = jnp.exp(sT - jnp.max(sT, axis=0, keepdims=True))
        p = jnp.where(valid_c, e / jnp.sum(e, axis=0, keepdims=True), 0.0)
        psum = psum + p
        oc_ref[r] = _dot(vct, p.astype(BF16))

    hi = psum.astype(BF16)
    lo = (psum - hi.astype(F32)).astype(BF16)
    imp = _dot(ov_ref[...], hi) + _dot(ov_ref[...], lo)
    cur = lax.shift_right_logical(tcol, int(np.log2(NSA_SEL_LEN)))
    forced = (jb == 0) | (jb == cur) | (jb == cur - 1)
    score = jnp.where(forced, NSA_FORCE, jnp.where(jb * NSA_SEL_LEN <= tcol, imp, -1.0))
    score = jnp.where(jb < n_sel, score, -2.0)
    selT = jnp.zeros((nrow, tq), F32)
    for _ in range(n_top):
        best = jnp.max(score, axis=0, keepdims=True)
        idx = jnp.min(jnp.where(score == best, jb, nrow), axis=0, keepdims=True)
        hit = jb == idx
        selT = jnp.where(hit, 1.0, selT)
        score = jnp.where(hit, -3.0, score)
    sel = selT.astype(BF16)

    m_ref[...] = jnp.full(m_ref.shape, NEG, F32)
    l_ref[...] = jnp.zeros(l_ref.shape, F32)
    acc_ref[...] = jnp.zeros(acc_ref.shape, F32)
    krow = lax.broadcasted_iota(jnp.int32, (tk, tq), 0)
    tq_col = t0 + lax.broadcasted_iota(jnp.int32, (tk, tq), 1)

    def sel_body(kt, carry):
        k0 = pl.multiple_of(kt * tk, tk)
        k = ks_ref[0, 0, pl.ds(k0, tk), :]
        vt = vst_ref[0, 0, kt]
        mask = (_dot(ext_ref[kt], sel) > 0.5) & (k0 + krow <= tq_col)
        for r in range(n_rep):
            s = jnp.where(mask, _dot_nt(k, q_head(r)), NEG)
            m_old = m_ref[r]
            m_new = jnp.maximum(m_old, jnp.max(s, axis=0, keepdims=True))
            alpha = jnp.exp(m_old - m_new)
            p = jnp.exp(s - m_new)
            l_ref[r] = alpha * l_ref[r] + jnp.sum(p, axis=0, keepdims=True)
            acc_ref[r] = alpha * acc_ref[r] + _dot(vt, p.astype(BF16))
            m_ref[r] = m_new
        return carry

    lax.fori_loop(0, (t0 + tq) // tk, sel_body, 0)

    W = NSA_WINDOW
    kstart = pl.multiple_of(jnp.maximum(t0 - W, 0), tk)
    jt = kstart // tk
    wk = [kw_ref[0, 0, pl.ds(kstart + c * tk, tk), :] for c in range(2)]
    wvt = [vwt_ref[0, 0, jt + c] for c in range(2)]
    wmask = []
    for c in range(2):
        rel = tq_col - (kstart + c * tk) - krow
        wmask.append((rel >= 0) & (rel < W))
    gT = _sigmoid(gl_ref[0]).T
    for r in range(n_rep):
        s = [jnp.where(wmask[c], _dot_nt(wk[c], q_head(r)), NEG) for c in range(2)]
        m = jnp.maximum(jnp.max(s[0], axis=0, keepdims=True), jnp.max(s[1], axis=0, keepdims=True))
        e = [jnp.exp(s[c] - m) for c in range(2)]
        den = jnp.sum(e[0], axis=0, keepdims=True) + jnp.sum(e[1], axis=0, keepdims=True)
        o_win = (_dot(wvt[0], e[0].astype(BF16)) + _dot(wvt[1], e[1].astype(BF16))) / den
        o_sel = acc_ref[r] / l_ref[r]
        c3 = 3 * r
        oT = gT[c3:c3 + 1, :] * oc_ref[r] + gT[c3 + 1:c3 + 2, :] * o_sel + gT[c3 + 2:c3 + 3, :] * o_win
        o_ref[0, :, r * HEAD_DIM:(r + 1) * HEAD_DIM] = oT.T.astype(BF16)


def nsa_attention(qn, kc, vct, ks, vst, kw, vwt, gl3, tq=256):
    B, T, QW = qn.shape
    G = kc.shape[1]
    n_rep = QW // HEAD_DIM // G
    nrow = kc.shape[2]
    tk = KV_TILE
    n_cmp = (T - NSA_CMP_LEN) // NSA_CMP_STRIDE + 1
    n_sel = T // NSA_SEL_LEN
    assert T % tq == 0 and n_sel <= nrow and T >= 2 * tk and tq == NSA_WINDOW == tk
    assert (T // tk) % SEL_UNROLL == 0
    starts = np.arange(nrow) * NSA_CMP_STRIDE
    sel_start = np.arange(nrow) * NSA_SEL_LEN
    ov = ((starts[None, :] <= sel_start[:, None] + NSA_SEL_LEN - 1)
          & (starts[None, :] + NSA_CMP_LEN - 1 >= sel_start[:, None])
          & (np.arange(nrow)[:, None] < n_sel) & (np.arange(nrow)[None, :] < n_cmp))
    ext = (np.arange(T)[:, None] // NSA_SEL_LEN == np.arange(nrow)[None, :]).reshape(T // tk, tk, nrow)
    k_spec = pl.BlockSpec((1, 1, T, HEAD_DIM), lambda b, g, i: (b, g, 0, 0))
    vt_spec = pl.BlockSpec((1, 1, T // tk, HEAD_DIM, tk), lambda b, g, i: (b, g, 0, 0, 0))
    c_spec = pl.BlockSpec((1, 1, nrow, HEAD_DIM), lambda b, g, i: (b, g, 0, 0))
    hw = n_rep * HEAD_DIM
    return pl.pallas_call(
        functools.partial(_nsa_kernel, n_rep=n_rep, n_cmp=n_cmp, n_sel=n_sel, n_top=min(NSA_TOPK, n_sel)),
        grid=(B, G, T // tq),
        in_specs=[pl.BlockSpec((1, tq, hw), lambda b, g, i: (b, i, g)),
                  c_spec, pl.BlockSpec((1, 1, HEAD_DIM, nrow), lambda b, g, i: (b, g, 0, 0)),
                  k_spec, vt_spec, k_spec, vt_spec,
                  pl.BlockSpec((1, tq, LANES), lambda b, g, i: (b, i, g)),
                  pl.BlockSpec((nrow, nrow), lambda b, g, i: (0, 0)),
                  pl.BlockSpec((T // tk, tk, nrow), lambda b, g, i: (0, 0, 0))],
        out_specs=pl.BlockSpec((1, tq, hw), lambda b, g, i: (b, i, g)),
        out_shape=jax.ShapeDtypeStruct((B, T, QW), BF16),
        scratch_shapes=[pltpu.VMEM((1, n_rep * tq), F32), pltpu.VMEM((1, n_rep * tq), F32),
                        pltpu.VMEM((HEAD_DIM, n_rep * tq), F32)],
        compiler_params=_params(("parallel", "parallel", "arbitrary")),
        name="nsa_attention",
    )(qn, kc, vct, ks, vst, kw, vwt, gl3, jnp.asarray(ov, BF16), jnp.asarray(ext, BF16))


def _nsa_kernel_old(q_ref, kc_ref, vc_ref, ks_ref, vs_ref, kw_ref, vw_ref, gl_ref, ov_ref, ex_ref,
                o_ref, m_ref, l_ref, acc_ref, oc_ref, *, n_rep, n_cmp, n_sel, n_top, tk):
    tq = q_ref.shape[1]
    nrow = kc_ref.shape[2]
    t0 = pl.program_id(2) * tq
    gates = _sigmoid(gl_ref[0])

    def q_head(r):
        return q_ref[0, :, r * HEAD_DIM:(r + 1) * HEAD_DIM]

    kc = kc_ref[0, 0]
    vc = vc_ref[0, 0]
    jb = lax.broadcasted_iota(jnp.int32, (nrow, tq), 0)
    tcol = t0 + lax.broadcasted_iota(jnp.int32, (nrow, tq), 1)
    valid_c = (jb * NSA_CMP_STRIDE + (NSA_CMP_LEN - 1) <= tcol) & (jb < n_cmp)
    psum = jnp.zeros((nrow, tq), F32)
    for r in range(n_rep):
        sT = jnp.where(valid_c, _dot_nt(kc, q_head(r)), NEG)
        e = jnp.exp(sT - jnp.max(sT, axis=0, keepdims=True))
        p = jnp.where(valid_c, e / jnp.sum(e, axis=0, keepdims=True), 0.0)
        psum = psum + p
        oc_ref[r] = _dot(p.T.astype(BF16), vc)

    hi = psum.astype(BF16)
    lo = (psum - hi.astype(F32)).astype(BF16)
    imp = _dot(ov_ref[...], hi) + _dot(ov_ref[...], lo)
    cur = lax.shift_right_logical(tcol, int(np.log2(NSA_SEL_LEN)))
    forced = (jb == 0) | (jb == cur) | (jb == cur - 1)
    score = jnp.where(forced, NSA_FORCE, jnp.where(jb * NSA_SEL_LEN <= tcol, imp, -1.0))
    score = jnp.where(jb < n_sel, score, -2.0)
    selT = jnp.zeros((nrow, tq), F32)
    for _ in range(n_top):
        best = jnp.max(score, axis=0, keepdims=True)
        idx = jnp.min(jnp.where(score == best, jb, nrow), axis=0, keepdims=True)
        hit = jb == idx
        selT = jnp.where(hit, 1.0, selT)
        score = jnp.where(hit, -3.0, score)
    sel = selT.T.astype(BF16)

    m_ref[...] = jnp.full(m_ref.shape, NEG, F32)
    l_ref[...] = jnp.zeros(l_ref.shape, F32)
    acc_ref[...] = jnp.zeros(acc_ref.shape, F32)
    trow = t0 + lax.broadcasted_iota(jnp.int32, (tq, tk), 0)
    kcol = lax.broadcasted_iota(jnp.int32, (tq, tk), 1)

    def sel_body(kt, carry):
        k0 = pl.multiple_of(kt * tk, tk)
        k = ks_ref[0, 0, pl.ds(k0, tk), :]
        v = vs_ref[0, 0, pl.ds(k0, tk), :]
        mask = (_dot(sel, ex_ref[kt]) > 0.5) & (k0 + kcol <= trow)
        for r in range(n_rep):
            s = jnp.where(mask, _dot_nt(q_head(r), k), NEG)
            m_old = m_ref[r][:, 0:1]
            m_new = jnp.maximum(m_old, jnp.max(s, axis=-1, keepdims=True))
            alpha = jnp.exp(m_old - m_new)
            p = jnp.exp(s - m_new)
            l_ref[r] = alpha * l_ref[r] + jnp.sum(p, axis=-1, keepdims=True)
            acc_ref[r] = alpha * acc_ref[r] + _dot(p.astype(BF16), v)
            m_ref[r] = jnp.broadcast_to(m_new, (tq, LANES))
        return carry

    lax.fori_loop(0, (t0 + tq + tk - 1) // tk, sel_body, 0)

    W = NSA_WINDOW
    kstart = pl.multiple_of(jnp.maximum(t0 - W, 0), tq)
    kw = kw_ref[0, 0, pl.ds(kstart, W + tq), :]
    vw = vw_ref[0, 0, pl.ds(kstart, W + tq), :]
    rel = (t0 + lax.broadcasted_iota(jnp.int32, (tq, W + tq), 0)
           - kstart - lax.broadcasted_iota(jnp.int32, (tq, W + tq), 1))
    wmask = (rel >= 0) & (rel < W)
    for r in range(n_rep):
        s = jnp.where(wmask, _dot_nt(q_head(r), kw), NEG)
        e = jnp.exp(s - jnp.max(s, axis=-1, keepdims=True))
        o_win = _dot(e.astype(BF16), vw) / jnp.sum(e, axis=-1, keepdims=True)
        o_sel = acc_ref[r] / l_ref[r][:, 0:1]
        c = 3 * r
        o = gates[:, c:c + 1] * oc_ref[r] + gates[:, c + 1:c + 2] * o_sel + gates[:, c + 2:c + 3] * o_win
        o_ref[0, :, r * HEAD_DIM:(r + 1) * HEAD_DIM] = o.astype(BF16)


def nsa_attention_old(qn, kc, vc, ks, vs, kw, vw, gl3, tq=256, tk=512):
    B, T, QW = qn.shape
    G = kc.shape[1]
    n_rep = QW // HEAD_DIM // G
    nrow = kc.shape[2]
    n_cmp = (T - NSA_CMP_LEN) // NSA_CMP_STRIDE + 1
    n_sel = T // NSA_SEL_LEN
    tk = min(tk, T)
    assert T % tq == 0 and T % tk == 0 and n_sel <= nrow and T >= NSA_WINDOW + tq and tq == NSA_WINDOW
    starts = np.arange(nrow) * NSA_CMP_STRIDE
    sel_start = np.arange(nrow) * NSA_SEL_LEN
    ov = ((starts[None, :] <= sel_start[:, None] + NSA_SEL_LEN - 1)
          & (starts[None, :] + NSA_CMP_LEN - 1 >= sel_start[:, None])
          & (np.arange(nrow)[:, None] < n_sel) & (np.arange(nrow)[None, :] < n_cmp))
    ex = (np.arange(T)[None, :] // NSA_SEL_LEN == np.arange(nrow)[:, None])
    ex = ex.reshape(nrow, T // tk, tk).transpose(1, 0, 2)
    kv_spec = pl.BlockSpec((1, 1, T, HEAD_DIM), lambda b, g, i: (b, g, 0, 0))
    c_spec = pl.BlockSpec((1, 1, nrow, HEAD_DIM), lambda b, g, i: (b, g, 0, 0))
    hw = n_rep * HEAD_DIM
    return pl.pallas_call(
        functools.partial(_nsa_kernel, n_rep=n_rep, n_cmp=n_cmp, n_sel=n_sel,
                          n_top=min(NSA_TOPK, n_sel), tk=tk),
        grid=(B, G, T // tq),
        in_specs=[pl.BlockSpec((1, tq, hw), lambda b, g, i: (b, i, g)),
                  c_spec, c_spec, kv_spec, kv_spec, kv_spec, kv_spec,
                  pl.BlockSpec((1, tq, LANES), lambda b, g, i: (b, i, g)),
                  pl.BlockSpec((nrow, nrow), lambda b, g, i: (0, 0)),
                  pl.BlockSpec((T // tk, nrow, tk), lambda b, g, i: (0, 0, 0))],
        out_specs=pl.BlockSpec((1, tq, hw), lambda b, g, i: (b, i, g)),
        out_shape=jax.ShapeDtypeStruct((B, T, QW), BF16),
        scratch_shapes=[pltpu.VMEM((n_rep, tq, LANES), F32), pltpu.VMEM((n_rep, tq, LANES), F32),
                        pltpu.VMEM((n_rep, tq, HEAD_DIM), F32), pltpu.VMEM((n_rep, tq, HEAD_DIM), F32)],
        compiler_params=_params(("parallel", "parallel", "arbitrary")),
        name="nsa_attention",
    )(qn, kc, vc, ks, vs, kw, vw, gl3, jnp.asarray(ov, BF16), jnp.asarray(ex, BF16))


def _swa_kernel(sink_ref, q_ref, k_ref, v_ref, o_ref, *, n_rep):
    tq = q_ref.shape[1]
    W = SWA_WINDOW
    g = pl.program_id(1)
    t0 = pl.program_id(2) * tq
    kstart = pl.multiple_of(jnp.maximum(t0 - W, 0), W)
    k = k_ref[0, 0, pl.ds(kstart, W + tq), :]
    v = v_ref[0, 0, pl.ds(kstart, W + tq), :]
    rel = (t0 + lax.broadcasted_iota(jnp.int32, (tq, W + tq), 0)
           - kstart - lax.broadcasted_iota(jnp.int32, (tq, W + tq), 1))
    wmask = (rel >= 0) & (rel < W)
    for r in range(n_rep):
        q = q_ref[0, :, r * HEAD_DIM:(r + 1) * HEAD_DIM]
        sk = sink_ref[g * n_rep + r]
        s = jnp.where(wmask, _dot_nt(q, k), NEG)
        m = jnp.maximum(jnp.max(s, axis=-1, keepdims=True), sk)
        e = jnp.exp(s - m)
        den = jnp.sum(e, axis=-1, keepdims=True) + jnp.exp(sk - m)
        o_ref[0, :, r * HEAD_DIM:(r + 1) * HEAD_DIM] = (_dot(e.astype(BF16), v) / den).astype(BF16)


def swa_attention(qn, kn, v, sinks, tq=256):
    B, T, QW = qn.shape
    G = kn.shape[1]
    n_rep = QW // HEAD_DIM // G
    tq = min(tq, T - SWA_WINDOW)
    assert T % tq == 0 and tq % SWA_WINDOW == 0
    hw = n_rep * HEAD_DIM
    kv_spec = pl.BlockSpec((1, 1, T, HEAD_DIM), lambda b, g, i: (b, g, 0, 0))
    return pl.pallas_call(
        functools.partial(_swa_kernel, n_rep=n_rep),
        grid=(B, G, T // tq),
        in_specs=[pl.BlockSpec(memory_space=pltpu.SMEM),
                  pl.BlockSpec((1, tq, hw), lambda b, g, i: (b, i, g)), kv_spec, kv_spec],
        out_specs=pl.BlockSpec((1, tq, hw), lambda b, g, i: (b, i, g)),
        out_shape=jax.ShapeDtypeStruct((B, T, QW), BF16),
        compiler_params=_params(("parallel", "parallel", "parallel")),
        name="swa_attention",
    )(sinks, qn, kn, v)


CONF_HALO = 32


def _conformer_kernel(a1_ref, a2_ref, h1_ref, h2_ref, w_ref, b_ref, g_ref, be_ref, o_ref,
                      glu_ref, conv_ref, *, blocks_per_seq):
    tm, C = a1_ref.shape
    first = (pl.program_id(0) % blocks_per_seq) == 0
    gh = h1_ref[...] * _sigmoid(h2_ref[...])
    glu_ref[0:CONF_HALO, :] = jnp.where(first, 0.0, gh)
    glu_ref[CONF_HALO:, :] = a1_ref[...] * _sigmoid(a2_ref[...])
    base = CONF_HALO - (CONV_WIDTH - 1)
    for cs in range(C // LANES):
        sl = slice(cs * LANES, (cs + 1) * LANES)
        acc = jnp.broadcast_to(b_ref[:, sl], (tm, LANES))
        for k in range(CONV_WIDTH):
            acc = acc + w_ref[k:k + 1, sl] * glu_ref[base + k:base + k + tm, sl]
        conv_ref[:, sl] = acc
    y = conv_ref[...]
    mu = jnp.mean(y, axis=-1, keepdims=True)
    d = y - mu
    var = jnp.mean(d * d, axis=-1, keepdims=True)
    yn = d * lax.rsqrt(var + EPS) * g_ref[...] + be_ref[...]
    o_ref[...] = _silu(yn).astype(BF16)


def conformer_conv(z, conv_w, conv_b, ln_g, ln_b, T, tm=128):
    N = z.shape[0]
    C = conv_w.shape[1]
    tm = min(tm, T)
    assert N % tm == 0 and T % tm == 0 and tm % CONF_HALO == 0
    hb = tm // CONF_HALO
    w = jnp.zeros((CONF_HALO, C), F32).at[:CONV_WIDTH].set(conv_w)
    hmap = lambda c: (lambda i: (jnp.maximum(i * hb - 1, 0), c))
    vec = pl.BlockSpec((1, C), lambda i: (0, 0))
    return pl.pallas_call(
        functools.partial(_conformer_kernel, blocks_per_seq=T // tm),
        grid=(N // tm,),
        in_specs=[pl.BlockSpec((tm, C), lambda i: (i, 0)), pl.BlockSpec((tm, C), lambda i: (i, 1)),
                  pl.BlockSpec((CONF_HALO, C), hmap(0)), pl.BlockSpec((CONF_HALO, C), hmap(1)),
                  pl.BlockSpec((CONF_HALO, C), lambda i: (0, 0)), vec, vec, vec],
        out_specs=pl.BlockSpec((tm, C), lambda i: (i, 0)),
        out_shape=jax.ShapeDtypeStruct((N, C), BF16),
        scratch_shapes=[pltpu.VMEM((tm + CONF_HALO, C), F32), pltpu.VMEM((tm, C), F32)],
        compiler_params=_params(("parallel",)),
        name="conformer_conv",
    )(z, z, z, z, w, conv_b.reshape(1, C), ln_g.reshape(1, C), ln_b.reshape(1, C))


def _gmlp_kernel(u0_ref, u1_ref, v0_ref, v1_ref, g_ref, be_ref, ws_ref, bs_ref, o_ref, *, n_groups):
    tm, hw = v0_ref.shape
    C = 2 * hw
    ch = GMLP_CHUNK
    gv = [_gelu_tanh(v0_ref[...]), _gelu_tanh(v1_ref[...])]
    mu = (jnp.sum(gv[0], axis=-1, keepdims=True) + jnp.sum(gv[1], axis=-1, keepdims=True)) / C
    d = [gv[0] - mu, gv[1] - mu]
    var = (jnp.sum(d[0] * d[0], axis=-1, keepdims=True) + jnp.sum(d[1] * d[1], axis=-1, keepdims=True)) / C
    rs = lax.rsqrt(var + EPS)
    vn = [(d[hf] * rs * g_ref[:, hf * hw:(hf + 1) * hw] + be_ref[:, hf * hw:(hf + 1) * hw]).astype(BF16)
          for hf in range(2)]
    u_refs = [u0_ref, u1_ref]
    tril = lax.broadcasted_iota(jnp.int32, (ch, ch), 0) >= lax.broadcasted_iota(jnp.int32, (ch, ch), 1)
    gph = n_groups // 2
    for gi in range(n_groups):
        hf, col = gi // gph, (gi % gph) * HEAD_DIM
        w = jnp.where(tril, ws_ref[gi], 0.0).astype(BF16)
        for c in range(tm // ch):
            rows = slice(c * ch, (c + 1) * ch)
            sp = _dot(w, vn[hf][rows, col:col + HEAD_DIM]) + bs_ref[gi]
            u = _gelu_tanh(u_refs[hf][rows, col:col + HEAD_DIM])
            o_ref[rows, gi * HEAD_DIM:(gi + 1) * HEAD_DIM] = (u * sp).astype(BF16)


def chunked_gmlp(z, u_col, ln_g, ln_b, w_s, b_s, T, tm=256):
    N = z.shape[0]
    n_groups, ch, _ = w_s.shape
    C = n_groups * HEAD_DIM
    hw = C // 2
    tm = min(tm, T)
    assert N % tm == 0 and T % tm == 0 and tm % ch == 0 and u_col % hw == 0
    cb = u_col // hw
    bsb = jnp.broadcast_to(b_s[:, :, None], (n_groups, ch, HEAD_DIM))
    zspec = lambda k: pl.BlockSpec((tm, hw), lambda i: (i, cb + k))
    vec = pl.BlockSpec((1, C), lambda i: (0, 0))
    return pl.pallas_call(
        functools.partial(_gmlp_kernel, n_groups=n_groups),
        grid=(N // tm,),
        in_specs=[zspec(0), zspec(1), zspec(2), zspec(3), vec, vec,
                  pl.BlockSpec((n_groups, ch, ch), lambda i: (0, 0, 0)),
                  pl.BlockSpec((n_groups, ch, HEAD_DIM), lambda i: (0, 0, 0))],
        out_specs=pl.BlockSpec((tm, C), lambda i: (i, 0)),
        out_shape=jax.ShapeDtypeStruct((N, C), BF16),
        compiler_params=_params(("parallel",)),
        name="chunked_gmlp",
    )(z, z, z, z, ln_g.reshape(1, C), ln_b.reshape(1, C), w_s, bsb)


def _even_mixer(xf, B, T, g, sc, sh, cosf, sinf, w_in, w_out, conv_w, conv_b, conv_ln_g, conv_ln_b,
                q_norm, k_norm, cmp_k_pos, cmp_k_w1, cmp_k_w2, cmp_v_pos, cmp_v_w1, cmp_v_w2):
    D = xf.shape[1]
    G = NSA_KV_GROUPS
    C = conv_w.shape[1]
    QW = D // 2
    KW = G * HEAD_DIM
    n_rep = QW // HEAD_DIM // G
    main = 2 * C + QW + 6 * KW
    wg = w_in[:, main:].reshape(D, G, n_rep * 3)
    wg = jnp.pad(wg, ((0, 0), (0, 0), (0, LANES - n_rep * 3))).reshape(D, G * LANES)
    z, gl = in_proj(xf, g, sc, sh, w_in[:, :main].astype(BF16), wg.astype(BF16), T)
    ya = conformer_conv(z, conv_w, conv_b, conv_ln_g, conv_ln_b, T)
    q_col = 2 * C
    kv_cols = [q_col + QW + k * KW for k in range(6)]
    qn, kcr, vcr, ks, vs, kw, vw = attn_prep(z.reshape(B, T, main), cosf, sinf, q_norm, k_norm,
                                             q_col, kv_cols,
                                             (KV_PLAIN, KV_PLAIN, 1, KV_TRANSPOSED, 2, KV_TRANSPOSED), G)
    half = NSA_CMP_STRIDE * HEAD_DIM
    hid = NSA_CMP_HIDDEN
    nch = T // NSA_CMP_STRIDE

    def w1cat(w1):
        return jnp.concatenate([w1[:half], w1[half:]], axis=1).astype(BF16)

    def posrows(pos):
        return jnp.zeros((8, half), F32).at[0].set(pos[:NSA_CMP_STRIDE].reshape(half)) \
                  .at[1].set(pos[NSA_CMP_STRIDE:].reshape(half)).astype(BF16)

    ends = np.arange(nch) * NSA_CMP_STRIDE + NSA_CMP_LEN - 1
    ends = np.minimum(ends, T - 1)
    kc, vc = nsa_compress(kcr, vcr, w1cat(cmp_k_w1), w1cat(cmp_v_w1), posrows(cmp_k_pos), posrows(cmp_v_pos),
                          cmp_k_w2.astype(BF16), cmp_v_w2.astype(BF16), k_norm[0:1],
                          cosf[:, ends], sinf[:, ends])
    yb = nsa_attention(qn, kc, vc, ks, vs, kw, vw, gl.reshape(B, T, G * LANES))
    return ya, yb.reshape(B * T, QW), w_out.astype(BF16)


def _odd_mixer(xf, B, T, g, sc, sh, cosf, sinf, w_in, w_out, q_norm, k_norm, sinks, ln_g, ln_b, w_s, b_s):
    D = xf.shape[1]
    G = SWA_KV_HEADS
    QW = D // 2
    KW = G * HEAD_DIM
    z, = in_proj(xf, g, sc, sh, w_in.astype(BF16), None, T)
    qn, kn, v = attn_prep(z.reshape(B, T, z.shape[1]), cosf, sinf, q_norm, k_norm.reshape(1, HEAD_DIM),
                          0, [QW, QW + KW], (0, -1), G)
    yc = swa_attention(qn, kn, v, sinks)
    yd = chunked_gmlp(z, QW + 2 * KW, ln_g, ln_b, w_s, b_s, T)
    return yc.reshape(B * T, QW), yd, w_out.astype(BF16)


def kernel(x, c, positions, ada_w, ada_b, norm_g, ffn_w_up, ffn_conv_w, ffn_conv_b, ffn_w_down, ev_w_in, ev_w_out, ev_conv_w, ev_conv_b, ev_conv_ln_g, ev_conv_ln_b, ev_q_norm, ev_k_norm, ev_cmp_k_pos, ev_cmp_k_w1, ev_cmp_k_w2, ev_cmp_v_pos, ev_cmp_v_w1, ev_cmp_v_w2, od_w_in, od_w_out, od_q_norm, od_k_norm, od_sinks, od_gmlp_ln_g, od_gmlp_ln_b, od_gmlp_w_s, od_gmlp_b_s):
    B, T, D = x.shape
    depth = ada_w.shape[0]
    cosf, sinf = rope_tables(positions)
    mod = adaln(c, ada_w, ada_b)
    xf = x.reshape(B * T, D)
    for i in range(depth):
        sh1, sc1, g1, sh2, sc2, g2 = [m.reshape(B, 1, D) for m in jnp.split(mod[i], 6, axis=-1)]
        j = i // 2
        if i % 2 == 0:
            ya, yb, w_out = _even_mixer(xf, B, T, norm_g[i, 0], sc1, sh1, cosf, sinf, ev_w_in[j], ev_w_out[j],
                                        ev_conv_w[j], ev_conv_b[j], ev_conv_ln_g[j], ev_conv_ln_b[j],
                                        ev_q_norm[j], ev_k_norm[j], ev_cmp_k_pos[j], ev_cmp_k_w1[j],
                                        ev_cmp_k_w2[j], ev_cmp_v_pos[j], ev_cmp_v_w1[j], ev_cmp_v_w2[j])
        else:
            ya, yb, w_out = _odd_mixer(xf, B, T, norm_g[i, 0], sc1, sh1, cosf, sinf, od_w_in[j], od_w_out[j],
                                       od_q_norm[j], od_k_norm[j], od_sinks[j], od_gmlp_ln_g[j],
                                       od_gmlp_ln_b[j], od_gmlp_w_s[j], od_gmlp_b_s[j])
        xf = out_proj(ya, yb, w_out, xf, g1, T)
        cw = jnp.zeros((1, 8, ffn_conv_w.shape[2]), F32).at[0, :FFN_CONV_WIDTH].set(ffn_conv_w[i]) \
                .at[0, FFN_CONV_WIDTH].set(ffn_conv_b[i])
        xf = conv_ffn(xf, norm_g[i, 1], sc2, sh2, g2, ffn_w_up[i].astype(BF16), cw,
                      ffn_w_down[i].astype(BF16), T)
    return xf.reshape(B, T, D)
```

```python
import functools

import numpy as np
import jax
import jax.numpy as jnp
from jax import lax
from jax.experimental import pallas as pl
from jax.experimental.pallas import tpu as pltpu

F32 = jnp.float32
BF16 = jnp.bfloat16

HEAD_DIM = 128
ROPE_THETA = 10000.0
EPS = 1e-6
NEG = -1e30
LOG2E = 1.4426950408889634

CONV_WIDTH = 31
NSA_KV_GROUPS = 2
NSA_CMP_LEN = 32
NSA_CMP_STRIDE = 16
NSA_CMP_HIDDEN = 256
NSA_SEL_LEN = 64
NSA_TOPK = 8
NSA_WINDOW = 256
NSA_FORCE = 1e6
SWA_KV_HEADS = 2
SWA_WINDOW = 128
GMLP_CHUNK = 128
FFN_CONV_WIDTH = 3

V7X_VMEM_BYTES = 64 * 1024 * 1024
VMEM_LIMIT = V7X_VMEM_BYTES - 8 * 1024 * 1024
LANES = 128
BF16_SUBLANES = 16


def _params(sem):
    return pltpu.CompilerParams(dimension_semantics=sem, vmem_limit_bytes=VMEM_LIMIT)


def _dot(a, b):
    return jnp.dot(a, b, preferred_element_type=F32)


def _dot_nt(a, b):
    return lax.dot_general(a, b, (((1,), (1,)), ((), ())), preferred_element_type=F32)


def _sigmoid(x):
    return 1.0 / (1.0 + jnp.exp(-x))


def _silu(x):
    return x * _sigmoid(x)


def _gelu_tanh(x):
    return 0.5 * x * (1.0 + jnp.tanh(np.sqrt(2.0 / np.pi).astype(np.float32) * (x + 0.044715 * (x * x * x))))


def _rope_kernel(pos_ref, inv_ref, cos_ref, sin_ref):
    ang = pos_ref[0].astype(F32) * inv_ref[0:1, :]
    cos_ref[0] = jnp.cos(ang)
    sin_ref[0] = jnp.sin(ang) * inv_ref[1:2, :]


def rope_tables(positions):
    B, T = positions.shape
    inv = ROPE_THETA ** (-jnp.arange(0, HEAD_DIM, 2, dtype=F32) / HEAD_DIM)
    half = HEAD_DIM // 2
    sign = jnp.concatenate([-jnp.ones((half,), F32), jnp.ones((half,), F32)])
    tab = jnp.stack([jnp.concatenate([inv, inv]), sign])
    return pl.pallas_call(
        _rope_kernel,
        grid=(B,),
        in_specs=[pl.BlockSpec((1, T, 1), lambda b: (b, 0, 0)),
                  pl.BlockSpec((2, HEAD_DIM), lambda b: (0, 0))],
        out_specs=[pl.BlockSpec((1, T, HEAD_DIM), lambda b: (b, 0, 0))] * 2,
        out_shape=[jax.ShapeDtypeStruct((B, T, HEAD_DIM), F32)] * 2,
        compiler_params=_params(("parallel",)),
        name="rope_tables",
    )(positions.reshape(B, T, 1), tab)


def _adaln_kernel(c_ref, w_ref, b_ref, o_ref):
    ca = _silu(c_ref[...]).astype(BF16)
    o_ref[0] = _dot(ca, w_ref[0].astype(BF16)) + b_ref[0]


def adaln(c, ada_w, ada_b, tn=1024):
    L, D, N6 = ada_w.shape
    B = c.shape[0]
    return pl.pallas_call(
        _adaln_kernel,
        grid=(L, N6 // tn),
        in_specs=[pl.BlockSpec((B, D), lambda l, j: (0, 0)),
                  pl.BlockSpec((1, D, tn), lambda l, j: (l, 0, j)),
                  pl.BlockSpec((1, 1, tn), lambda l, j: (l, 0, j))],
        out_specs=pl.BlockSpec((1, B, tn), lambda l, j: (l, 0, j)),
        out_shape=jax.ShapeDtypeStruct((L, B, N6), F32),
        compiler_params=_params(("parallel", "parallel")),
        name="adaln",
    )(c, ada_w, ada_b.reshape(L, 1, N6))


def _norm_mod_rows(x_ref, g, scale1, shift, out_ref, out_row0, nrows, chunk):
    def body(ci, carry):
        r = pl.multiple_of(ci * chunk, chunk)
        x = x_ref[pl.ds(r, chunk), :]
        ms = jnp.mean(x * x, axis=-1, keepdims=True)
        y = x * lax.rsqrt(ms + EPS) * g
        out_ref[pl.ds(out_row0 + r, chunk), :] = (y * scale1 + shift).astype(BF16)
        return carry
    lax.fori_loop(0, nrows // chunk, body, 0)


def _inproj_kernel(x_ref, g_ref, sc_ref, sh_ref, w_ref, *rest, has_gate, chunk):
    if has_gate:
        wg_ref, o_ref, og_ref, h_ref = rest
    else:
        o_ref, h_ref = rest

    @pl.when(pl.program_id(1) == 0)
    def _():
        _norm_mod_rows(x_ref, g_ref[...], 1.0 + sc_ref[0], sh_ref[0], h_ref, 0, x_ref.shape[0], chunk)
        if has_gate:
            og_ref[...] = _dot(h_ref[...], wg_ref[...])

    o_ref[...] = _dot(h_ref[...], w_ref[...]).astype(o_ref.dtype)


def in_proj(x, g, sc, sh, w, n_out, wg, T, tm=1024, tn=512):
    N, D = x.shape
    Nout = n_out
    tm = min(tm, T)
    assert N % tm == 0 and T % tm == 0 and Nout % tn == 0
    bmap = lambda i, j: ((i * tm) // T, 0, 0)
    in_specs = [pl.BlockSpec((tm, D), lambda i, j: (i, 0)),
                pl.BlockSpec((1, D), lambda i, j: (0, 0)),
                pl.BlockSpec((1, 1, D), bmap),
                pl.BlockSpec((1, 1, D), bmap),
                pl.BlockSpec((D, tn), lambda i, j: (0, j))]
    out_specs = [pl.BlockSpec((tm, tn), lambda i, j: (i, j))]
    out_shape = [jax.ShapeDtypeStruct((N, Nout), BF16)]
    args = [x, g.reshape(1, D), sc, sh, w]
    if wg is not None:
        ng = wg.shape[1]
        in_specs.append(pl.BlockSpec((D, ng), lambda i, j: (0, 0)))
        out_specs.append(pl.BlockSpec((tm, ng), lambda i, j: (i, 0)))
        out_shape.append(jax.ShapeDtypeStruct((N, ng), F32))
        args.append(wg)
    return pl.pallas_call(
        functools.partial(_inproj_kernel, has_gate=wg is not None, chunk=min(128, tm)),
        grid=(N // tm, Nout // tn),
        in_specs=in_specs, out_specs=out_specs, out_shape=out_shape,
        scratch_shapes=[pltpu.VMEM((tm, D), BF16)],
        compiler_params=_params(("parallel", "arbitrary")),
        name="in_proj",
    )(*args)


def _outproj_kernel(ya_ref, yb_ref, wa_ref, wb_ref, x_ref, gate_ref, o_ref):
    y = _dot(ya_ref[...], wa_ref[...]) + _dot(yb_ref[...], wb_ref[...])
    o_ref[...] = x_ref[...] + gate_ref[0] * y


def out_proj(ya, yb, w_out, x, gate, T, tm=512, tn=2048):
    N, D = x.shape
    Ka, Kb = ya.shape[1], yb.shape[1]
    tm = min(tm, T)
    assert N % tm == 0 and T % tm == 0 and D % tn == 0 and Ka % Kb == 0
    return pl.pallas_call(
        _outproj_kernel,
        grid=(N // tm, D // tn),
        in_specs=[pl.BlockSpec((tm, Ka), lambda i, j: (i, 0)),
                  pl.BlockSpec((tm, Kb), lambda i, j: (i, 0)),
                  pl.BlockSpec((Ka, tn), lambda i, j: (0, j)),
                  pl.BlockSpec((Kb, tn), lambda i, j: (Ka // Kb, j)),
                  pl.BlockSpec((tm, tn), lambda i, j: (i, j)),
                  pl.BlockSpec((1, 1, tn), lambda i, j: ((i * tm) // T, 0, j))],
        out_specs=pl.BlockSpec((tm, tn), lambda i, j: (i, j)),
        out_shape=jax.ShapeDtypeStruct((N, D), F32),
        compiler_params=_params(("parallel", "parallel")),
        name="out_proj",
    )(ya, yb, w_out, w_out, x, gate)


FFN_HALO = BF16_SUBLANES


def _to_tiles(a):
    rows = a.shape[0]
    return jnp.swapaxes(a.reshape(8, rows // 8, LANES), 0, 1).reshape(rows, LANES)


def _from_tiles(a):
    rows = a.shape[0]
    return jnp.swapaxes(a.reshape(rows // 8, 8, LANES), 0, 1).reshape(rows, LANES)


def _ffn_kernel(x_ref, xh_ref, g_ref, sc_ref, sh_ref, gate_ref, wa_ref, wb_ref, cwa_ref, cwb_ref,
                wd_ref, o_ref, h_ref, xp_ref, acc_ref, ya_ref, yb_ref, *, blocks_per_seq, chunk, sub):
    i = pl.program_id(0)
    j = pl.program_id(1)
    tm = x_ref.shape[0]

    K = tm // 8

    def norm_mod(xr, g, scale1, shift):
        ms = jnp.mean(xr * xr, axis=-1, keepdims=True)
        return (xr * lax.rsqrt(ms + EPS) * g) * scale1 + shift

    @pl.when(j == 0)
    def _():
        g = g_ref[...]
        scale1 = 1.0 + sc_ref[0]
        shift = sh_ref[0]
        for cb in range(x_ref.shape[1] // LANES):
            cols = slice(cb * LANES, (cb + 1) * LANES)
            xp_ref[:, cols] = _to_tiles(x_ref[:, cols])
        _norm_mod_rows(xp_ref, g, scale1, shift, h_ref, FFN_HALO, tm, chunk)
        first = (i % blocks_per_seq) == 0
        h_ref[0:FFN_HALO, :] = jnp.where(first, 0.0, norm_mod(xh_ref[...], g, scale1, shift)).astype(BF16)
        acc_ref[...] = jnp.zeros_like(acc_ref)

    h = h_ref[...]
    tf = wa_ref.shape[1]
    nsub = tf // sub
    sub0 = lax.broadcasted_iota(jnp.int32, (8, sub), 0) == 0

    def conv(y_ref, cw_ref, cs):
        H = FFN_HALO
        last = pltpu.roll(y_ref[H + tm - 8:H + tm, :], 1, axis=0)
        last2 = pltpu.roll(y_ref[H + tm - 16:H + tm - 8, :], 1, axis=0)
        m1 = jnp.where(sub0, y_ref[H - 1:H, :], last)
        m2 = jnp.where(sub0, y_ref[H - 2:H - 1, :], last2)
        y_ref[H - 8:H, :] = m1
        y_ref[H - 16:H - 8, :] = m2
        cw = cw_ref[0, :, cs]
        out = cw[FFN_CONV_WIDTH:FFN_CONV_WIDTH + 1, :]
        for k in range(FFN_CONV_WIDTH):
            off = H - 8 * (FFN_CONV_WIDTH - 1 - k)
            out = out + cw[k:k + 1, :] * y_ref[off:off + tm, :]
        return out

    for c in range(nsub):
        cs = slice(c * sub, (c + 1) * sub)
        ya_ref[c] = _dot(h, wa_ref[:, cs])
        yb_ref[c] = _dot(h, wb_ref[:, cs])
    for c in range(nsub):
        cs = slice(c * sub, (c + 1) * sub)
        act = (_silu(conv(ya_ref.at[c], cwa_ref, cs)) * conv(yb_ref.at[c], cwb_ref, cs)).astype(BF16)
        d = _dot(act, wd_ref[cs, :])
        for cb in range(acc_ref.shape[0]):
            acc_ref[cb] += d[:, cb * LANES:(cb + 1) * LANES]

    @pl.when(j == pl.num_programs(1) - 1)
    def _():
        for cb in range(acc_ref.shape[0]):
            cols = slice(cb * LANES, (cb + 1) * LANES)
            o_ref[:, cols] = x_ref[:, cols] + gate_ref[0, :, cols] * _from_tiles(acc_ref[cb])


def conv_ffn(x, g, sc, sh, gate, w_up, cw, w_down, T, tm=512, tf=512, sub=256):
    N, D = x.shape
    DFF = w_down.shape[0]
    tm = min(tm, T)
    assert N % tm == 0 and T % tm == 0 and DFF % tf == 0 and tm % FFN_HALO == 0
    sub = min(sub, tf)
    nff = DFF // tf
    hb = tm // FFN_HALO
    bmap = lambda i, j: ((i * tm) // T, 0, 0)
    return pl.pallas_call(
        functools.partial(_ffn_kernel, blocks_per_seq=T // tm, chunk=min(128, tm), sub=min(sub, tf)),
        grid=(N // tm, nff),
        in_specs=[pl.BlockSpec((tm, D), lambda i, j: (i, 0)),
                  pl.BlockSpec((FFN_HALO, D), lambda i, j: (jnp.maximum(i * hb - 1, 0), 0)),
                  pl.BlockSpec((1, D), lambda i, j: (0, 0)),
                  pl.BlockSpec((1, 1, D), bmap),
                  pl.BlockSpec((1, 1, D), bmap),
                  pl.BlockSpec((1, 1, D), bmap),
                  pl.BlockSpec((D, tf), lambda i, j: (0, j)),
                  pl.BlockSpec((D, tf), lambda i, j: (0, j + nff)),
                  pl.BlockSpec((1, 8, tf), lambda i, j: (0, 0, j)),
                  pl.BlockSpec((1, 8, tf), lambda i, j: (0, 0, j + nff)),
                  pl.BlockSpec((tf, D), lambda i, j: (j, 0))],
        out_specs=pl.BlockSpec((tm, D), lambda i, j: (i, 0)),
        out_shape=jax.ShapeDtypeStruct((N, D), F32),
        scratch_shapes=[pltpu.VMEM((tm + FFN_HALO, D), BF16), pltpu.VMEM((tm, D), F32),
                        pltpu.VMEM((D // LANES, tm, LANES), F32),
                        pltpu.VMEM((tf // sub, tm + FFN_HALO, sub), F32),
                        pltpu.VMEM((tf // sub, tm + FFN_HALO, sub), F32)],
        compiler_params=_params(("parallel", "arbitrary")),
        name="conv_ffn",
    )(x, x, g.reshape(1, D), sc, sh, gate, w_up, w_up, cw, cw, w_down)


KV_PLAIN = -1
KV_TRANSPOSED = -2
KV_TILE = 256
SEL_UNROLL = 2


def _rope(y, cos, sin):
    return y * cos + pltpu.roll(y, HEAD_DIM // 2, axis=1) * sin


def _head_norm(x, g):
    ms = jnp.mean(x * x, axis=-1, keepdims=True)
    return x * lax.rsqrt(ms + EPS) * g


def _prep_kernel(*refs, n_q_heads, kv_kinds, n_groups):
    nkv = len(kv_kinds)
    zq_ref = refs[0]
    kv_refs = refs[1:1 + nkv]
    cos_ref, sin_ref, qn_ref, kn_ref = refs[1 + nkv:5 + nkv]
    q_out = refs[5 + nkv]
    kv_out = refs[6 + nkv:]
    cos = cos_ref[0]
    sin = sin_ref[0]
    scale = HEAD_DIM ** -0.5 * LOG2E
    for hd in range(n_q_heads):
        sl = slice(hd * HEAD_DIM, (hd + 1) * HEAD_DIM)
        y = _rope(_head_norm(zq_ref[0, :, sl].astype(F32), qn_ref[...]), cos, sin)
        q_out[0, :, sl] = (y * scale).astype(BF16)
    for idx, kind in enumerate(kv_kinds):
        for gi in range(n_groups):
            sl = slice(gi * HEAD_DIM, (gi + 1) * HEAD_DIM)
            a = kv_refs[idx][0, :, sl]
            if kind >= 0:
                a = _rope(_head_norm(a.astype(F32), kn_ref[kind:kind + 1, :]), cos, sin)
            if kind == KV_TRANSPOSED:
                for c in range(a.shape[0] // KV_TILE):
                    kv_out[idx][0, gi, c] = a[c * KV_TILE:(c + 1) * KV_TILE, :].astype(F32).T.astype(BF16)
            else:
                kv_out[idx][0, gi] = a.astype(BF16)


def attn_prep(z3, cosf, sinf, q_norm, k_norm, q_col, kv_cols, kv_kinds, n_groups, tm=512):
    B, T, _ = z3.shape
    tm = min(tm, T)
    QW = 8 * HEAD_DIM
    KW = n_groups * HEAD_DIM
    assert q_col % QW == 0 and all(c % KW == 0 for c in kv_cols) and T % tm == 0
    nkv = len(kv_cols)
    in_specs = [pl.BlockSpec((1, tm, QW), lambda b, t: (b, t, q_col // QW))]
    for c in kv_cols:
        in_specs.append(pl.BlockSpec((1, tm, KW), functools.partial(lambda b, t, cb: (b, t, cb), cb=c // KW)))
    in_specs += [pl.BlockSpec((1, tm, HEAD_DIM), lambda b, t: (b, t, 0))] * 2
    in_specs += [pl.BlockSpec((1, HEAD_DIM), lambda b, t: (0, 0)),
                 pl.BlockSpec(k_norm.shape, lambda b, t: (0, 0))]
    out_specs = [pl.BlockSpec((1, tm, QW), lambda b, t: (b, t, 0))]
    out_shape = [jax.ShapeDtypeStruct((B, T, QW), BF16)]
    for kind in kv_kinds:
        if kind == KV_TRANSPOSED:
            assert tm % KV_TILE == 0
            out_specs.append(pl.BlockSpec((1, n_groups, tm // KV_TILE, HEAD_DIM, KV_TILE),
                                          lambda b, t: (b, 0, t, 0, 0)))
            out_shape.append(jax.ShapeDtypeStruct((B, n_groups, T // KV_TILE, HEAD_DIM, KV_TILE), BF16))
        else:
            out_specs.append(pl.BlockSpec((1, n_groups, tm, HEAD_DIM), lambda b, t: (b, 0, t, 0)))
            out_shape.append(jax.ShapeDtypeStruct((B, n_groups, T, HEAD_DIM), BF16))
    return pl.pallas_call(
        functools.partial(_prep_kernel, n_q_heads=QW // HEAD_DIM, kv_kinds=tuple(kv_kinds), n_groups=n_groups),
        grid=(B, T // tm),
        in_specs=in_specs, out_specs=out_specs, out_shape=out_shape,
        compiler_params=_params(("parallel", "parallel")),
        name="attn_prep",
    )(*([z3] * (1 + nkv)), cosf, sinf, q_norm.reshape(1, HEAD_DIM), k_norm)


def _compress_kernel(ak_ref, av_ref, w1k_ref, w1v_ref, pk_ref, pv_ref, w2k_ref, w2v_ref,
                     kn_ref, cos_ref, sin_ref, kc_ref, vc_ref):
    hid = NSA_CMP_HIDDEN

    def mlp(a_ref, w1_ref, p_ref, w2_ref):
        P = _dot(a_ref[0, 0], w1_ref[...])
        Q = _dot(p_ref[...], w1_ref[...])
        pb = Q[0:1, :hid] + Q[1:2, hid:]
        nxt = pltpu.roll(P[:, hid:], P.shape[0] - 1, axis=0)
        hdn = _silu(P[:, :hid] + nxt + pb)
        return _dot(hdn.astype(BF16), w2_ref[...])

    kc = mlp(ak_ref, w1k_ref, pk_ref, w2k_ref)
    kc = _rope(_head_norm(kc, kn_ref[...]), cos_ref[0], sin_ref[0])
    kc_ref[0, 0] = kc.astype(BF16)
    vc_ref[0, 0] = mlp(av_ref, w1v_ref, pv_ref, w2v_ref).T.astype(BF16)


def nsa_compress(kcr, vcr, w1k, w1v, pk, pv, w2k, w2v, kn0, cos_end, sin_end):
    B, G, T, dh = kcr.shape
    nch = T // NSA_CMP_STRIDE
    cw = NSA_CMP_STRIDE * dh
    a_spec = pl.BlockSpec((1, 1, nch, cw), lambda b, g: (b, g, 0, 0))
    full = lambda arr: pl.BlockSpec(arr.shape, lambda b, g: (0,) * arr.ndim)
    tab_spec = pl.BlockSpec((1, nch, dh), lambda b, g: (b, 0, 0))
    o_spec = pl.BlockSpec((1, 1, nch, dh), lambda b, g: (b, g, 0, 0))
    return pl.pallas_call(
        _compress_kernel,
        grid=(B, G),
        in_specs=[a_spec, a_spec, full(w1k), full(w1v), full(pk), full(pv), full(w2k), full(w2v),
                  full(kn0), tab_spec, tab_spec],
        out_specs=[o_spec, pl.BlockSpec((1, 1, dh, nch), lambda b, g: (b, g, 0, 0))],
        out_shape=[jax.ShapeDtypeStruct((B, G, nch, dh), BF16), jax.ShapeDtypeStruct((B, G, dh, nch), BF16)],
        compiler_params=_params(("parallel", "parallel")),
        name="nsa_compress",
    )(kcr.reshape(B, G, nch, cw), vcr.reshape(B, G, nch, cw), w1k, w1v, pk, pv, w2k, w2v,
      kn0, cos_end, sin_end)


def _nsa_kernel(q_ref, kc_ref, vct_ref, ks_ref, vst_ref, kw_ref, vwt_ref, gl_ref, ov_ref, ext_ref,
                o_ref, m_ref, l_ref, acc_ref, *, n_rep, n_cmp, n_sel, n_top):
    tq = q_ref.shape[1]
    tk = KV_TILE
    nrow = kc_ref.shape[2]
    t0 = pl.program_id(2) * tq
    q4 = jnp.concatenate([q_ref[0, :, r * HEAD_DIM:(r + 1) * HEAD_DIM] for r in range(n_rep)], axis=0)

    def heads(a):
        return jnp.concatenate([a] * n_rep, axis=1)

    jb = lax.broadcasted_iota(jnp.int32, (nrow, tq), 0)
    tcol = t0 + lax.broadcasted_iota(jnp.int32, (nrow, tq), 1)
    valid_c = (jb * NSA_CMP_STRIDE + (NSA_CMP_LEN - 1) <= tcol) & (jb < n_cmp)
    validf = heads(jnp.where(valid_c, 1.0, 0.0))
    s = _dot_nt(kc_ref[0, 0], q4) + heads(jnp.where(valid_c, 0.0, NEG))
    e = jnp.exp2(s - jnp.max(s, axis=0, keepdims=True))
    p = e / jnp.sum(e, axis=0, keepdims=True) * validf
    o_cmp = _dot(vct_ref[0, 0], p.astype(BF16))
    psum = p[:, 0:tq]
    for r in range(1, n_rep):
        psum = psum + p[:, r * tq:(r + 1) * tq]

    hi = psum.astype(BF16)
    lo = (psum - hi.astype(F32)).astype(BF16)
    imp = _dot(ov_ref[...], hi) + _dot(ov_ref[...], lo)
    nsr = ov_ref.shape[0]
    jb = lax.broadcasted_iota(jnp.int32, (nsr, tq), 0)
    tcol = t0 + lax.broadcasted_iota(jnp.int32, (nsr, tq), 1)
    cur = lax.shift_right_logical(tcol, int(np.log2(NSA_SEL_LEN)))
    forced = (jb == 0) | (jb == cur) | (jb == cur - 1)
    score = jnp.where(forced, NSA_FORCE, jnp.where(jb * NSA_SEL_LEN <= tcol, imp, -1.0))
    score = jnp.where(jb < n_sel, score, -2.0)
    selT = jnp.zeros((nsr, tq), F32)
    for _ in range(n_top):
        best = jnp.max(score, axis=0, keepdims=True)
        idx = jnp.min(jnp.where(score == best, jb, nsr), axis=0, keepdims=True)
        hit = jb == idx
        selT = jnp.where(hit, 1.0, selT)
        score = jnp.where(hit, -3.0, score)
    sel = selT.astype(BF16)

    m_ref[...] = jnp.full(m_ref.shape, NEG, F32)
    l_ref[...] = jnp.zeros(l_ref.shape, F32)
    acc_ref[...] = jnp.zeros(acc_ref.shape, F32)
    krow = lax.broadcasted_iota(jnp.int32, (tk, tq), 0)
    tq_col = t0 + lax.broadcasted_iota(jnp.int32, (tk, tq), 1)

    def sel_body(it, carry):
        s = []
        for u in range(SEL_UNROLL):
            kt = it * SEL_UNROLL + u
            k0 = pl.multiple_of(kt * tk, tk)
            mask = (_dot(ext_ref[kt], sel) > 0.5) & (k0 + krow <= tq_col)
            s.append(_dot_nt(ks_ref[0, 0, pl.ds(k0, tk), :], q4) + heads(jnp.where(mask, 0.0, NEG)))
        m_old = m_ref[...]
        m_new = m_old
        for u in range(SEL_UNROLL):
            m_new = jnp.maximum(m_new, jnp.max(s[u], axis=0, keepdims=True))
        alpha = jnp.exp2(m_old - m_new)
        lsum = alpha * l_ref[...]
        pv = alpha * acc_ref[...]
        for u in range(SEL_UNROLL):
            p = jnp.exp2(s[u] - m_new)
            lsum = lsum + jnp.sum(p, axis=0, keepdims=True)
            pv = pv + _dot(vst_ref[0, 0, it * SEL_UNROLL + u], p.astype(BF16))
        l_ref[...] = lsum
        acc_ref[...] = pv
        m_ref[...] = m_new
        return carry

    n_tiles = (t0 + tq) // tk
    lax.fori_loop(0, (n_tiles + SEL_UNROLL - 1) // SEL_UNROLL, sel_body, 0)
    o_sel = acc_ref[...] / l_ref[...]

    W = NSA_WINDOW
    kstart = pl.multiple_of(jnp.maximum(t0 - W, 0), tk)
    jt = kstart // tk
    sw = []
    for c in range(2):
        rel = tq_col - (kstart + c * tk) - krow
        bias = heads(jnp.where((rel >= 0) & (rel < W), 0.0, NEG))
        sw.append(_dot_nt(kw_ref[0, 0, pl.ds(kstart + c * tk, tk), :], q4) + bias)
    m = jnp.maximum(jnp.max(sw[0], axis=0, keepdims=True), jnp.max(sw[1], axis=0, keepdims=True))
    ew = [jnp.exp2(sw[c] - m) for c in range(2)]
    den = jnp.sum(ew[0], axis=0, keepdims=True) + jnp.sum(ew[1], axis=0, keepdims=True)
    o_win = (_dot(vwt_ref[0, 0, jt], ew[0].astype(BF16)) + _dot(vwt_ref[0, 0, jt + 1], ew[1].astype(BF16))) / den

    gT = _sigmoid(gl_ref[0]).T
    for r in range(n_rep):
        hs = slice(r * tq, (r + 1) * tq)
        c3 = 3 * r
        oT = (gT[c3:c3 + 1, :] * o_cmp[:, hs] + gT[c3 + 1:c3 + 2, :] * o_sel[:, hs]
              + gT[c3 + 2:c3 + 3, :] * o_win[:, hs])
        o_ref[0, :, r * HEAD_DIM:(r + 1) * HEAD_DIM] = oT.T.astype(BF16)


def _nsa_kernel_v2(q_ref, kc_ref, vct_ref, ks_ref, vst_ref, kw_ref, vwt_ref, gl_ref, ov_ref, ext_ref,
                o_ref, m_ref, l_ref, acc_ref, oc_ref, *, n_rep, n_cmp, n_sel, n_top):
    tq = q_ref.shape[1]
    tk = KV_TILE
    nrow = kc_ref.shape[2]
    t0 = pl.program_id(2) * tq

    def q_head(r):
        return q_ref[0, :, r * HEAD_DIM:(r + 1) * HEAD_DIM]

    kc = kc_ref[0, 0]
    vct = vct_ref[0, 0]
    jb = lax.broadcasted_iota(jnp.int32, (nrow, tq), 0)
    tcol = t0 + lax.broadcasted_iota(jnp.int32, (nrow, tq), 1)
    valid_c = (jb * NSA_CMP_STRIDE + (NSA_CMP_LEN - 1) <= tcol) & (jb < n_cmp)
    psum = jnp.zeros((nrow, tq), F32)
    for r in range(n_rep):
        sT = jnp.where(valid_c, _dot_nt(kc, q_head(r)), NEG)
        e = jnp.exp(sT - jnp.max(sT, axis=0, keepdims=True))
        p = jnp.where(valid_c, e / jnp.sum(e, axis=0, keepdims=True), 0.0)
        psum = psum + p
        oc_ref[r] = _dot(vct, p.astype(BF16))

    hi = psum.astype(BF16)
    lo = (psum - hi.astype(F32)).astype(BF16)
    imp = _dot(ov_ref[...], hi) + _dot(ov_ref[...], lo)
    cur = lax.shift_right_logical(tcol, int(np.log2(NSA_SEL_LEN)))
    forced = (jb == 0) | (jb == cur) | (jb == cur - 1)
    score = jnp.where(forced, NSA_FORCE, jnp.where(jb * NSA_SEL_LEN <= tcol, imp, -1.0))
    score = jnp.where(jb < n_sel, score, -2.0)
    selT = jnp.zeros((nrow, tq), F32)
    for _ in range(n_top):
        best = jnp.max(score, axis=0, keepdims=True)
        idx = jnp.min(jnp.where(score == best, jb, nrow), axis=0, keepdims=True)
        hit = jb == idx
        selT = jnp.where(hit, 1.0, selT)
        score = jnp.where(hit, -3.0, score)
    sel = selT.astype(BF16)

    m_ref[...] = jnp.full(m_ref.shape, NEG, F32)
    l_ref[...] = jnp.zeros(l_ref.shape, F32)
    acc_ref[...] = jnp.zeros(acc_ref.shape, F32)
    krow = lax.broadcasted_iota(jnp.int32, (tk, tq), 0)
    tq_col = t0 + lax.broadcasted_iota(jnp.int32, (tk, tq), 1)

    def sel_body(kt, carry):
        k0 = pl.multiple_of(kt * tk, tk)
        k = ks_ref[0, 0, pl.ds(k0, tk), :]
        vt = vst_ref[0, 0, kt]
        mask = (_dot(ext_ref[kt], sel) > 0.5) & (k0 + krow <= tq_col)
        for r in range(n_rep):
            s = jnp.where(mask, _dot_nt(k, q_head(r)), NEG)
            m_old = m_ref[r]
            m_new = jnp.maximum(m_old, jnp.max(s, axis=0, keepdims=True))
            alpha = jnp.exp(m_old - m_new)
            p = jnp.exp(s - m_new)
            l_ref[r] = alpha * l_ref[r] + jnp.sum(p, axis=0, keepdims=True)
            acc_ref[r] = alpha * acc_ref[r] + _dot(vt, p.astype(BF16))
            m_ref[r] = m_new
        return carry

    lax.fori_loop(0, (t0 + tq) // tk, sel_body, 0)

    W = NSA_WINDOW
    kstart = pl.multiple_of(jnp.maximum(t0 - W, 0), tk)
    jt = kstart // tk
    wk = [kw_ref[0, 0, pl.ds(kstart + c * tk, tk), :] for c in range(2)]
    wvt = [vwt_ref[0, 0, jt + c] for c in range(2)]
    wmask = []
    for c in range(2):
        rel = tq_col - (kstart + c * tk) - krow
        wmask.append((rel >= 0) & (rel < W))
    gT = _sigmoid(gl_ref[0]).T
    for r in range(n_rep):
        s = [jnp.where(wmask[c], _dot_nt(wk[c], q_head(r)), NEG) for c in range(2)]
        m = jnp.maximum(jnp.max(s[0], axis=0, keepdims=True), jnp.max(s[1], axis=0, keepdims=True))
        e = [jnp.exp(s[c] - m) for c in range(2)]
        den = jnp.sum(e[0], axis=0, keepdims=True) + jnp.sum(e[1], axis=0, keepdims=True)
        o_win = (_dot(wvt[0], e[0].astype(BF16)) + _dot(wvt[1], e[1].astype(BF16))) / den
        o_sel = acc_ref[r] / l_ref[r]
        c3 = 3 * r
        oT = gT[c3:c3 + 1, :] * oc_ref[r] + gT[c3 + 1:c3 + 2, :] * o_sel + gT[c3 + 2:c3 + 3, :] * o_win
        o_ref[0, :, r * HEAD_DIM:(r + 1) * HEAD_DIM] = oT.T.astype(BF16)


def nsa_attention(qn, kc, vct, ks, vst, kw, vwt, gl3, tq=256):
    B, T, QW = qn.shape
    G = kc.shape[1]
    n_rep = QW // HEAD_DIM // G
    nrow = kc.shape[2]
    tk = KV_TILE
    n_cmp = (T - NSA_CMP_LEN) // NSA_CMP_STRIDE + 1
    n_sel = T // NSA_SEL_LEN
    assert T % tq == 0 and n_sel <= nrow and T >= 2 * tk and tq == NSA_WINDOW == tk
    assert (T // tk) % SEL_UNROLL == 0
    starts = np.arange(nrow) * NSA_CMP_STRIDE
    sel_start = np.arange(nrow) * NSA_SEL_LEN
    nsr = -(-n_sel // BF16_SUBLANES) * BF16_SUBLANES
    ov = ((starts[None, :] <= sel_start[:, None] + NSA_SEL_LEN - 1)
          & (starts[None, :] + NSA_CMP_LEN - 1 >= sel_start[:, None])
          & (np.arange(nrow)[:, None] < n_sel) & (np.arange(nrow)[None, :] < n_cmp))[:nsr]
    ext = (np.arange(T)[:, None] // NSA_SEL_LEN == np.arange(nsr)[None, :]).reshape(T // tk, tk, nsr)
    k_spec = pl.BlockSpec((1, 1, T, HEAD_DIM), lambda b, g, i: (b, g, 0, 0))
    vt_spec = pl.BlockSpec((1, 1, T // tk, HEAD_DIM, tk), lambda b, g, i: (b, g, 0, 0, 0))
    c_spec = pl.BlockSpec((1, 1, nrow, HEAD_DIM), lambda b, g, i: (b, g, 0, 0))
    hw = n_rep * HEAD_DIM
    return pl.pallas_call(
        functools.partial(_nsa_kernel, n_rep=n_rep, n_cmp=n_cmp, n_sel=n_sel, n_top=min(NSA_TOPK, n_sel)),
        grid=(B, G, T // tq),
        in_specs=[pl.BlockSpec((1, tq, hw), lambda b, g, i: (b, i, g)),
                  c_spec, pl.BlockSpec((1, 1, HEAD_DIM, nrow), lambda b, g, i: (b, g, 0, 0)),
                  k_spec, vt_spec, k_spec, vt_spec,
                  pl.BlockSpec((1, tq, LANES), lambda b, g, i: (b, i, g)),
                  pl.BlockSpec((nsr, nrow), lambda b, g, i: (0, 0)),
                  pl.BlockSpec((T // tk, tk, nsr), lambda b, g, i: (0, 0, 0))],
        out_specs=pl.BlockSpec((1, tq, hw), lambda b, g, i: (b, i, g)),
        out_shape=jax.ShapeDtypeStruct((B, T, QW), BF16),
        scratch_shapes=[pltpu.VMEM((1, n_rep * tq), F32), pltpu.VMEM((1, n_rep * tq), F32),
                        pltpu.VMEM((HEAD_DIM, n_rep * tq), F32)],
        compiler_params=_params(("parallel", "parallel", "arbitrary")),
        name="nsa_attention",
    )(qn, kc, vct, ks, vst, kw, vwt, gl3, jnp.asarray(ov, BF16), jnp.asarray(ext, BF16))


def _nsa_kernel_old(q_ref, kc_ref, vc_ref, ks_ref, vs_ref, kw_ref, vw_ref, gl_ref, ov_ref, ex_ref,
                o_ref, m_ref, l_ref, acc_ref, oc_ref, *, n_rep, n_cmp, n_sel, n_top, tk):
    tq = q_ref.shape[1]
    nrow = kc_ref.shape[2]
    t0 = pl.program_id(2) * tq
    gates = _sigmoid(gl_ref[0])

    def q_head(r):
        return q_ref[0, :, r * HEAD_DIM:(r + 1) * HEAD_DIM]

    kc = kc_ref[0, 0]
    vc = vc_ref[0, 0]
    jb = lax.broadcasted_iota(jnp.int32, (nrow, tq), 0)
    tcol = t0 + lax.broadcasted_iota(jnp.int32, (nrow, tq), 1)
    valid_c = (jb * NSA_CMP_STRIDE + (NSA_CMP_LEN - 1) <= tcol) & (jb < n_cmp)
    psum = jnp.zeros((nrow, tq), F32)
    for r in range(n_rep):
        sT = jnp.where(valid_c, _dot_nt(kc, q_head(r)), NEG)
        e = jnp.exp(sT - jnp.max(sT, axis=0, keepdims=True))
        p = jnp.where(valid_c, e / jnp.sum(e, axis=0, keepdims=True), 0.0)
        psum = psum + p
        oc_ref[r] = _dot(p.T.astype(BF16), vc)

    hi = psum.astype(BF16)
    lo = (psum - hi.astype(F32)).astype(BF16)
    imp = _dot(ov_ref[...], hi) + _dot(ov_ref[...], lo)
    cur = lax.shift_right_logical(tcol, int(np.log2(NSA_SEL_LEN)))
    forced = (jb == 0) | (jb == cur) | (jb == cur - 1)
    score = jnp.where(forced, NSA_FORCE, jnp.where(jb * NSA_SEL_LEN <= tcol, imp, -1.0))
    score = jnp.where(jb < n_sel, score, -2.0)
    selT = jnp.zeros((nrow, tq), F32)
    for _ in range(n_top):
        best = jnp.max(score, axis=0, keepdims=True)
        idx = jnp.min(jnp.where(score == best, jb, nrow), axis=0, keepdims=True)
        hit = jb == idx
        selT = jnp.where(hit, 1.0, selT)
        score = jnp.where(hit, -3.0, score)
    sel = selT.T.astype(BF16)

    m_ref[...] = jnp.full(m_ref.shape, NEG, F32)
    l_ref[...] = jnp.zeros(l_ref.shape, F32)
    acc_ref[...] = jnp.zeros(acc_ref.shape, F32)
    trow = t0 + lax.broadcasted_iota(jnp.int32, (tq, tk), 0)
    kcol = lax.broadcasted_iota(jnp.int32, (tq, tk), 1)

    def sel_body(kt, carry):
        k0 = pl.multiple_of(kt * tk, tk)
        k = ks_ref[0, 0, pl.ds(k0, tk), :]
        v = vs_ref[0, 0, pl.ds(k0, tk), :]
        mask = (_dot(sel, ex_ref[kt]) > 0.5) & (k0 + kcol <= trow)
        for r in range(n_rep):
            s = jnp.where(mask, _dot_nt(q_head(r), k), NEG)
            m_old = m_ref[r][:, 0:1]
            m_new = jnp.maximum(m_old, jnp.max(s, axis=-1, keepdims=True))
            alpha = jnp.exp(m_old - m_new)
            p = jnp.exp(s - m_new)
            l_ref[r] = alpha * l_ref[r] + jnp.sum(p, axis=-1, keepdims=True)
            acc_ref[r] = alpha * acc_ref[r] + _dot(p.astype(BF16), v)
            m_ref[r] = jnp.broadcast_to(m_new, (tq, LANES))
        return carry

    lax.fori_loop(0, (t0 + tq + tk - 1) // tk, sel_body, 0)

    W = NSA_WINDOW
    kstart = pl.multiple_of(jnp.maximum(t0 - W, 0), tq)
    kw = kw_ref[0, 0, pl.ds(kstart, W + tq), :]
    vw = vw_ref[0, 0, pl.ds(kstart, W + tq), :]
    rel = (t0 + lax.broadcasted_iota(jnp.int32, (tq, W + tq), 0)
           - kstart - lax.broadcasted_iota(jnp.int32, (tq, W + tq), 1))
    wmask = (rel >= 0) & (rel < W)
    for r in range(n_rep):
        s = jnp.where(wmask, _dot_nt(q_head(r), kw), NEG)
        e = jnp.exp(s - jnp.max(s, axis=-1, keepdims=True))
        o_win = _dot(e.astype(BF16), vw) / jnp.sum(e, axis=-1, keepdims=True)
        o_sel = acc_ref[r] / l_ref[r][:, 0:1]
        c = 3 * r
        o = gates[:, c:c + 1] * oc_ref[r] + gates[:, c + 1:c + 2] * o_sel + gates[:, c + 2:c + 3] * o_win
        o_ref[0, :, r * HEAD_DIM:(r + 1) * HEAD_DIM] = o.astype(BF16)


def nsa_attention_old(qn, kc, vc, ks, vs, kw, vw, gl3, tq=256, tk=512):
    B, T, QW = qn.shape
    G = kc.shape[1]
    n_rep = QW // HEAD_DIM // G
    nrow = kc.shape[2]
    n_cmp = (T - NSA_CMP_LEN) // NSA_CMP_STRIDE + 1
    n_sel = T // NSA_SEL_LEN
    tk = min(tk, T)
    assert T % tq == 0 and T % tk == 0 and n_sel <= nrow and T >= NSA_WINDOW + tq and tq == NSA_WINDOW
    starts = np.arange(nrow) * NSA_CMP_STRIDE
    sel_start = np.arange(nrow) * NSA_SEL_LEN
    ov = ((starts[None, :] <= sel_start[:, None] + NSA_SEL_LEN - 1)
          & (starts[None, :] + NSA_CMP_LEN - 1 >= sel_start[:, None])
          & (np.arange(nrow)[:, None] < n_sel) & (np.arange(nrow)[None, :] < n_cmp))
    ex = (np.arange(T)[None, :] // NSA_SEL_LEN == np.arange(nrow)[:, None])
    ex = ex.reshape(nrow, T // tk, tk).transpose(1, 0, 2)
    kv_spec = pl.BlockSpec((1, 1, T, HEAD_DIM), lambda b, g, i: (b, g, 0, 0))
    c_spec = pl.BlockSpec((1, 1, nrow, HEAD_DIM), lambda b, g, i: (b, g, 0, 0))
    hw = n_rep * HEAD_DIM
    return pl.pallas_call(
        functools.partial(_nsa_kernel, n_rep=n_rep, n_cmp=n_cmp, n_sel=n_sel,
                          n_top=min(NSA_TOPK, n_sel), tk=tk),
        grid=(B, G, T // tq),
        in_specs=[pl.BlockSpec((1, tq, hw), lambda b, g, i: (b, i, g)),
                  c_spec, c_spec, kv_spec, kv_spec, kv_spec, kv_spec,
                  pl.BlockSpec((1, tq, LANES), lambda b, g, i: (b, i, g)),
                  pl.BlockSpec((nrow, nrow), lambda b, g, i: (0, 0)),
                  pl.BlockSpec((T // tk, nrow, tk), lambda b, g, i: (0, 0, 0))],
        out_specs=pl.BlockSpec((1, tq, hw), lambda b, g, i: (b, i, g)),
        out_shape=jax.ShapeDtypeStruct((B, T, QW), BF16),
        scratch_shapes=[pltpu.VMEM((n_rep, tq, LANES), F32), pltpu.VMEM((n_rep, tq, LANES), F32),
                        pltpu.VMEM((n_rep, tq, HEAD_DIM), F32), pltpu.VMEM((n_rep, tq, HEAD_DIM), F32)],
        compiler_params=_params(("parallel", "parallel", "arbitrary")),
        name="nsa_attention",
    )(qn, kc, vc, ks, vs, kw, vw, gl3, jnp.asarray(ov, BF16), jnp.asarray(ex, BF16))


def _swa_kernel(sink_ref, q_ref, k_ref, vt_ref, o_ref, *, n_rep):
    tq = q_ref.shape[1]
    tk = KV_TILE
    W = SWA_WINDOW
    g = pl.program_id(1)
    t0 = pl.program_id(2) * tq
    q4 = jnp.concatenate([q_ref[0, :, r * HEAD_DIM:(r + 1) * HEAD_DIM] for r in range(n_rep)], axis=0)
    sink = jnp.concatenate([jnp.full((1, tq), sink_ref[g * n_rep + r] * LOG2E, F32) for r in range(n_rep)], axis=1)
    kstart = pl.multiple_of(jnp.maximum(t0 - tk, 0), tk)
    jt = kstart // tk
    krow = lax.broadcasted_iota(jnp.int32, (tk, tq), 0)
    tq_col = t0 + lax.broadcasted_iota(jnp.int32, (tk, tq), 1)
    s = []
    for c in range(2):
        rel = tq_col - (kstart + c * tk) - krow
        bias = jnp.concatenate([jnp.where((rel >= 0) & (rel < W), 0.0, NEG)] * n_rep, axis=1)
        s.append(_dot_nt(k_ref[0, 0, pl.ds(kstart + c * tk, tk), :], q4) + bias)
    m = jnp.maximum(jnp.maximum(jnp.max(s[0], axis=0, keepdims=True), jnp.max(s[1], axis=0, keepdims=True)), sink)
    e = [jnp.exp2(s[c] - m) for c in range(2)]
    den = jnp.sum(e[0], axis=0, keepdims=True) + jnp.sum(e[1], axis=0, keepdims=True) + jnp.exp2(sink - m)
    oT = (_dot(vt_ref[0, 0, jt], e[0].astype(BF16)) + _dot(vt_ref[0, 0, jt + 1], e[1].astype(BF16))) / den
    for r in range(n_rep):
        o_ref[0, :, r * HEAD_DIM:(r + 1) * HEAD_DIM] = oT[:, r * tq:(r + 1) * tq].T.astype(BF16)


def swa_attention(qn, kn, vt, sinks, tq=256):
    B, T, QW = qn.shape
    G = kn.shape[1]
    n_rep = QW // HEAD_DIM // G
    tk = KV_TILE
    assert T % tq == 0 and tq == tk and SWA_WINDOW <= tk and T >= 2 * tk
    hw = n_rep * HEAD_DIM
    return pl.pallas_call(
        functools.partial(_swa_kernel, n_rep=n_rep),
        grid=(B, G, T // tq),
        in_specs=[pl.BlockSpec(memory_space=pltpu.SMEM),
                  pl.BlockSpec((1, tq, hw), lambda b, g, i: (b, i, g)),
                  pl.BlockSpec((1, 1, T, HEAD_DIM), lambda b, g, i: (b, g, 0, 0)),
                  pl.BlockSpec((1, 1, T // tk, HEAD_DIM, tk), lambda b, g, i: (b, g, 0, 0, 0))],
        out_specs=pl.BlockSpec((1, tq, hw), lambda b, g, i: (b, i, g)),
        out_shape=jax.ShapeDtypeStruct((B, T, QW), BF16),
        compiler_params=_params(("parallel", "parallel", "parallel")),
        name="swa_attention",
    )(sinks, qn, kn, vt)


CONF_HALO = 32


def _conformer_kernel(a1_ref, a2_ref, h1_ref, h2_ref, w_ref, b_ref, g_ref, be_ref, o_ref,
                      glu_ref, conv_ref, *, blocks_per_seq):
    tm, C = a1_ref.shape
    K = tm // 8
    PRE = CONF_HALO
    nctx = CONV_WIDTH - 1
    first = (pl.program_id(0) % blocks_per_seq) == 0

    gh = jnp.where(first, 0.0, h1_ref[...].astype(F32) * _sigmoid(h2_ref[...].astype(F32)))
    sub0 = lax.broadcasted_iota(jnp.int32, (8 * nctx, LANES), 0) % 8 == 0
    for cs in range(C // LANES):
        sl = slice(cs * LANES, (cs + 1) * LANES)
        g = _to_tiles(a1_ref[:, sl].astype(F32) * _sigmoid(a2_ref[:, sl].astype(F32)))
        glu_ref[cs, 8 * PRE:, :] = g
        moved = pltpu.roll(g[tm - 8 * nctx:, :], 1, axis=0)
        halo = jnp.concatenate([jnp.broadcast_to(gh[r:r + 1, sl], (8, LANES)) for r in range(PRE - nctx, PRE)],
                               axis=0)
        glu_ref[cs, 8 * (PRE - nctx):8 * PRE, :] = jnp.where(sub0, halo, moved)

    rb = 128

    def conv_group(gi, carry):
        r0 = pl.multiple_of(gi * rb, rb)
        for cs in range(C // LANES):
            sl = slice(cs * LANES, (cs + 1) * LANES)
            acc = jnp.broadcast_to(b_ref[:, sl], (rb, LANES))
            for k in range(CONV_WIDTH):
                off = 8 * (PRE - nctx + k)
                acc = acc + w_ref[k:k + 1, sl] * glu_ref[cs, pl.ds(r0 + off, rb), :]
            conv_ref[cs, pl.ds(r0, rb), :] = acc
        return carry

    lax.fori_loop(0, tm // rb, conv_group, 0)

    ns = C // LANES
    for r0 in range(0, tm, rb):
        y = [conv_ref[cs, r0:r0 + rb, :] for cs in range(ns)]
        mu = sum(jnp.sum(v, axis=-1, keepdims=True) for v in y) / C
        d = [v - mu for v in y]
        var = sum(jnp.sum(v * v, axis=-1, keepdims=True) for v in d) / C
        rs = lax.rsqrt(var + EPS)
        for cs in range(ns):
            sl = slice(cs * LANES, (cs + 1) * LANES)
            conv_ref[cs, r0:r0 + rb, :] = _silu(d[cs] * rs * g_ref[:, sl] + be_ref[:, sl])
    for cs in range(ns):
        o_ref[:, cs * LANES:(cs + 1) * LANES] = _from_tiles(conv_ref[cs]).astype(BF16)


def conformer_conv(z, conv_w, conv_b, ln_g, ln_b, T, tm=512):
    N = z.shape[0]
    C = conv_w.shape[1]
    tm = min(tm, T)
    assert N % tm == 0 and T % tm == 0 and tm % CONF_HALO == 0 and CONV_WIDTH - 1 <= min(CONF_HALO, tm // 8)
    hb = tm // CONF_HALO
    w = jnp.zeros((CONF_HALO, C), F32).at[:CONV_WIDTH].set(conv_w)
    hmap = lambda c: (lambda i: (jnp.maximum(i * hb - 1, 0), c))
    vec = pl.BlockSpec((1, C), lambda i: (0, 0))
    return pl.pallas_call(
        functools.partial(_conformer_kernel, blocks_per_seq=T // tm),
        grid=(N // tm,),
        in_specs=[pl.BlockSpec((tm, C), lambda i: (i, 0)), pl.BlockSpec((tm, C), lambda i: (i, 1)),
                  pl.BlockSpec((CONF_HALO, C), hmap(0)), pl.BlockSpec((CONF_HALO, C), hmap(1)),
                  pl.BlockSpec((CONF_HALO, C), lambda i: (0, 0)), vec, vec, vec],
        out_specs=pl.BlockSpec((tm, C), lambda i: (i, 0)),
        out_shape=jax.ShapeDtypeStruct((N, C), BF16),
        scratch_shapes=[pltpu.VMEM((C // LANES, tm + 8 * CONF_HALO, LANES), F32),
                        pltpu.VMEM((C // LANES, tm, LANES), F32)],
        compiler_params=_params(("parallel",)),
        name="conformer_conv",
    )(z, z, z, z, w, conv_b.reshape(1, C), ln_g.reshape(1, C), ln_b.reshape(1, C))


def _gmlp_kernel(u0_ref, u1_ref, v0_ref, v1_ref, g_ref, be_ref, ws_ref, bs_ref, o_ref, *, n_groups):
    tm, hw = v0_ref.shape
    C = 2 * hw
    ch = GMLP_CHUNK
    gv = [_gelu_tanh(v0_ref[...].astype(F32)), _gelu_tanh(v1_ref[...].astype(F32))]
    mu = (jnp.sum(gv[0], axis=-1, keepdims=True) + jnp.sum(gv[1], axis=-1, keepdims=True)) / C
    d = [gv[0] - mu, gv[1] - mu]
    var = (jnp.sum(d[0] * d[0], axis=-1, keepdims=True) + jnp.sum(d[1] * d[1], axis=-1, keepdims=True)) / C
    rs = lax.rsqrt(var + EPS)
    vn = [(d[hf] * rs * g_ref[:, hf * hw:(hf + 1) * hw] + be_ref[:, hf * hw:(hf + 1) * hw]).astype(BF16)
          for hf in range(2)]
    u_refs = [u0_ref, u1_ref]
    tril = lax.broadcasted_iota(jnp.int32, (ch, ch), 0) >= lax.broadcasted_iota(jnp.int32, (ch, ch), 1)
    gph = n_groups // 2
    for gi in range(n_groups):
        hf, col = gi // gph, (gi % gph) * HEAD_DIM
        w = jnp.where(tril, ws_ref[gi], 0.0).astype(BF16)
        for c in range(tm // ch):
            rows = slice(c * ch, (c + 1) * ch)
            sp = _dot(w, vn[hf][rows, col:col + HEAD_DIM]) + bs_ref[gi]
            u = _gelu_tanh(u_refs[hf][rows, col:col + HEAD_DIM].astype(F32))
            o_ref[rows, gi * HEAD_DIM:(gi + 1) * HEAD_DIM] = (u * sp).astype(BF16)


def chunked_gmlp(z, u_col, ln_g, ln_b, w_s, b_s, T, tm=256):
    N = z.shape[0]
    n_groups, ch, _ = w_s.shape
    C = n_groups * HEAD_DIM
    hw = C // 2
    tm = min(tm, T)
    assert N % tm == 0 and T % tm == 0 and tm % ch == 0 and u_col % hw == 0
    cb = u_col // hw
    bsb = jnp.broadcast_to(b_s[:, :, None], (n_groups, ch, HEAD_DIM))
    zspec = lambda k: pl.BlockSpec((tm, hw), lambda i: (i, cb + k))
    vec = pl.BlockSpec((1, C), lambda i: (0, 0))
    return pl.pallas_call(
        functools.partial(_gmlp_kernel, n_groups=n_groups),
        grid=(N // tm,),
        in_specs=[zspec(0), zspec(1), zspec(2), zspec(3), vec, vec,
                  pl.BlockSpec((n_groups, ch, ch), lambda i: (0, 0, 0)),
                  pl.BlockSpec((n_groups, ch, HEAD_DIM), lambda i: (0, 0, 0))],
        out_specs=pl.BlockSpec((tm, C), lambda i: (i, 0)),
        out_shape=jax.ShapeDtypeStruct((N, C), BF16),
        compiler_params=_params(("parallel",)),
        name="chunked_gmlp",
    )(z, z, z, z, ln_g.reshape(1, C), ln_b.reshape(1, C), w_s, bsb)


def _even_mixer(xf, B, T, g, sc, sh, cosf, sinf, w_in, w_out, conv_w, conv_b, conv_ln_g, conv_ln_b,
                q_norm, k_norm, cmp_k_pos, cmp_k_w1, cmp_k_w2, cmp_v_pos, cmp_v_w1, cmp_v_w2):
    D = xf.shape[1]
    G = NSA_KV_GROUPS
    C = conv_w.shape[1]
    QW = D // 2
    KW = G * HEAD_DIM
    n_rep = QW // HEAD_DIM // G
    main = 2 * C + QW + 6 * KW
    wg = w_in[:, main:].reshape(D, G, n_rep * 3)
    wg = jnp.pad(wg, ((0, 0), (0, 0), (0, LANES - n_rep * 3))).reshape(D, G * LANES)
    z, gl = in_proj(xf, g, sc, sh, w_in.astype(BF16), main, wg.astype(BF16), T)
    ya = conformer_conv(z, conv_w, conv_b, conv_ln_g, conv_ln_b, T)
    q_col = 2 * C
    kv_cols = [q_col + QW + k * KW for k in range(6)]
    qn, kcr, vcr, ks, vs, kw, vw = attn_prep(z.reshape(B, T, main), cosf, sinf, q_norm, k_norm,
                                             q_col, kv_cols,
                                             (KV_PLAIN, KV_PLAIN, 1, KV_TRANSPOSED, 2, KV_TRANSPOSED), G)
    half = NSA_CMP_STRIDE * HEAD_DIM
    hid = NSA_CMP_HIDDEN
    nch = T // NSA_CMP_STRIDE

    def w1cat(w1):
        return jnp.concatenate([w1[:half], w1[half:]], axis=1).astype(BF16)

    def posrows(pos):
        return jnp.zeros((8, half), F32).at[0].set(pos[:NSA_CMP_STRIDE].reshape(half)) \
                  .at[1].set(pos[NSA_CMP_STRIDE:].reshape(half)).astype(BF16)

    ends = np.arange(nch) * NSA_CMP_STRIDE + NSA_CMP_LEN - 1
    ends = np.minimum(ends, T - 1)
    kc, vc = nsa_compress(kcr, vcr, w1cat(cmp_k_w1), w1cat(cmp_v_w1), posrows(cmp_k_pos), posrows(cmp_v_pos),
                          cmp_k_w2.astype(BF16), cmp_v_w2.astype(BF16), k_norm[0:1],
                          cosf[:, ends], sinf[:, ends])
    yb = nsa_attention(qn, kc, vc, ks, vs, kw, vw, gl.reshape(B, T, G * LANES))
    return ya, yb.reshape(B * T, QW), w_out.astype(BF16)


def _odd_mixer(xf, B, T, g, sc, sh, cosf, sinf, w_in, w_out, q_norm, k_norm, sinks, ln_g, ln_b, w_s, b_s):
    D = xf.shape[1]
    G = SWA_KV_HEADS
    QW = D // 2
    KW = G * HEAD_DIM
    z, = in_proj(xf, g, sc, sh, w_in.astype(BF16), w_in.shape[1], None, T)
    qn, kn, v = attn_prep(z.reshape(B, T, z.shape[1]), cosf, sinf, q_norm, k_norm.reshape(1, HEAD_DIM),
                          0, [QW, QW + KW], (0, KV_TRANSPOSED), G)
    yc = swa_attention(qn, kn, v, sinks)
    yd = chunked_gmlp(z, QW + 2 * KW, ln_g, ln_b, w_s, b_s, T)
    return yc.reshape(B * T, QW), yd, w_out.astype(BF16)


def kernel(x, c, positions, ada_w, ada_b, norm_g, ffn_w_up, ffn_conv_w, ffn_conv_b, ffn_w_down, ev_w_in, ev_w_out, ev_conv_w, ev_conv_b, ev_conv_ln_g, ev_conv_ln_b, ev_q_norm, ev_k_norm, ev_cmp_k_pos, ev_cmp_k_w1, ev_cmp_k_w2, ev_cmp_v_pos, ev_cmp_v_w1, ev_cmp_v_w2, od_w_in, od_w_out, od_q_norm, od_k_norm, od_sinks, od_gmlp_ln_g, od_gmlp_ln_b, od_gmlp_w_s, od_gmlp_b_s):
    B, T, D = x.shape
    depth = ada_w.shape[0]
    cosf, sinf = rope_tables(positions)
    mod = adaln(c, ada_w, ada_b)
    xf = x.reshape(B * T, D)
    for i in range(depth):
        sh1, sc1, g1, sh2, sc2, g2 = [m.reshape(B, 1, D) for m in jnp.split(mod[i], 6, axis=-1)]
        j = i // 2
        if i % 2 == 0:
            ya, yb, w_out = _even_mixer(xf, B, T, norm_g[i, 0], sc1, sh1, cosf, sinf, ev_w_in[j], ev_w_out[j],
                                        ev_conv_w[j], ev_conv_b[j], ev_conv_ln_g[j], ev_conv_ln_b[j],
                                        ev_q_norm[j], ev_k_norm[j], ev_cmp_k_pos[j], ev_cmp_k_w1[j],
                                        ev_cmp_k_w2[j], ev_cmp_v_pos[j], ev_cmp_v_w1[j], ev_cmp_v_w2[j])
        else:
            ya, yb, w_out = _odd_mixer(xf, B, T, norm_g[i, 0], sc1, sh1, cosf, sinf, od_w_in[j], od_w_out[j],
                                       od_q_norm[j], od_k_norm[j], od_sinks[j], od_gmlp_ln_g[j],
                                       od_gmlp_ln_b[j], od_gmlp_w_s[j], od_gmlp_b_s[j])
        xf = out_proj(ya, yb, w_out, xf, g1, T)
        cw = jnp.zeros((1, 8, ffn_conv_w.shape[2]), F32).at[0, :FFN_CONV_WIDTH].set(ffn_conv_w[i]) \
                .at[0, FFN_CONV_WIDTH].set(ffn_conv_b[i])
        xf = conv_ffn(xf, norm_g[i, 1], sc2, sh2, g2, ffn_w_up[i].astype(BF16), cw,
                      ffn_w_down[i].astype(BF16), T)
    return xf.reshape(B, T, D)
```

```python
import functools

import numpy as np
import jax
import jax.numpy as jnp
from jax import lax
from jax.experimental import pallas as pl
from jax.experimental.pallas import tpu as pltpu

F32 = jnp.float32
BF16 = jnp.bfloat16

HEAD_DIM = 128
ROPE_THETA = 10000.0
EPS = 1e-6
NEG = -1e30
LOG2E = 1.4426950408889634

CONV_WIDTH = 31
NSA_KV_GROUPS = 2
NSA_CMP_LEN = 32
NSA_CMP_STRIDE = 16
NSA_CMP_HIDDEN = 256
NSA_SEL_LEN = 64
NSA_TOPK = 8
NSA_WINDOW = 256
NSA_FORCE = 1e6
SWA_KV_HEADS = 2
SWA_WINDOW = 128
GMLP_CHUNK = 128
FFN_CONV_WIDTH = 3

V7X_VMEM_BYTES = 64 * 1024 * 1024
VMEM_LIMIT = V7X_VMEM_BYTES - 8 * 1024 * 1024
LANES = 128
BF16_SUBLANES = 16


def _params(sem):
    return pltpu.CompilerParams(dimension_semantics=sem, vmem_limit_bytes=VMEM_LIMIT)


def _dot(a, b):
    return jnp.dot(a, b, preferred_element_type=F32)


def _dot_nt(a, b):
    return lax.dot_general(a, b, (((1,), (1,)), ((), ())), preferred_element_type=F32)


def _sigmoid(x):
    return 1.0 / (1.0 + jnp.exp(-x))


def _silu(x):
    return x * _sigmoid(x)


def _gelu_tanh(x):
    return 0.5 * x * (1.0 + jnp.tanh(np.sqrt(2.0 / np.pi).astype(np.float32) * (x + 0.044715 * (x * x * x))))


def _rope_kernel(pos_ref, inv_ref, cos_ref, sin_ref):
    ang = pos_ref[0].astype(F32) * inv_ref[0:1, :]
    cos_ref[0] = jnp.cos(ang)
    sin_ref[0] = jnp.sin(ang) * inv_ref[1:2, :]


def rope_tables(positions):
    B, T = positions.shape
    inv = ROPE_THETA ** (-jnp.arange(0, HEAD_DIM, 2, dtype=F32) / HEAD_DIM)
    half = HEAD_DIM // 2
    sign = jnp.concatenate([-jnp.ones((half,), F32), jnp.ones((half,), F32)])
    tab = jnp.stack([jnp.concatenate([inv, inv]), sign])
    return pl.pallas_call(
        _rope_kernel,
        grid=(B,),
        in_specs=[pl.BlockSpec((1, T, 1), lambda b: (b, 0, 0)),
                  pl.BlockSpec((2, HEAD_DIM), lambda b: (0, 0))],
        out_specs=[pl.BlockSpec((1, T, HEAD_DIM), lambda b: (b, 0, 0))] * 2,
        out_shape=[jax.ShapeDtypeStruct((B, T, HEAD_DIM), F32)] * 2,
        compiler_params=_params(("parallel",)),
        name="rope_tables",
    )(positions.reshape(B, T, 1), tab)


def _adaln_kernel(c_ref, w_ref, b_ref, o_ref):
    ca = _silu(c_ref[...]).astype(BF16)
    o_ref[0] = _dot(ca, w_ref[0].astype(BF16)) + b_ref[0]


def adaln(c, ada_w, ada_b, tn=1024):
    L, D, N6 = ada_w.shape
    B = c.shape[0]
    return pl.pallas_call(
        _adaln_kernel,
        grid=(L, N6 // tn),
        in_specs=[pl.BlockSpec((B, D), lambda l, j: (0, 0)),
                  pl.BlockSpec((1, D, tn), lambda l, j: (l, 0, j)),
                  pl.BlockSpec((1, 1, tn), lambda l, j: (l, 0, j))],
        out_specs=pl.BlockSpec((1, B, tn), lambda l, j: (l, 0, j)),
        out_shape=jax.ShapeDtypeStruct((L, B, N6), F32),
        compiler_params=_params(("parallel", "parallel")),
        name="adaln",
    )(c, ada_w, ada_b.reshape(L, 1, N6))


def _norm_mod_rows(x_ref, g, scale1, shift, out_ref, out_row0, nrows, chunk):
    def body(ci, carry):
        r = pl.multiple_of(ci * chunk, chunk)
        x = x_ref[pl.ds(r, chunk), :]
        ms = jnp.mean(x * x, axis=-1, keepdims=True)
        y = x * lax.rsqrt(ms + EPS) * g
        out_ref[pl.ds(out_row0 + r, chunk), :] = (y * scale1 + shift).astype(BF16)
        return carry
    lax.fori_loop(0, nrows // chunk, body, 0)


def _inproj_kernel(x_ref, g_ref, sc_ref, sh_ref, scn_ref, shn_ref, w_ref, *rest, has_gate, rows_per, n_chunks):
    if has_gate:
        wg_ref, o_ref, og_ref, ha_ref, hb_ref = rest
    else:
        o_ref, ha_ref, hb_ref = rest
    i = pl.program_id(0)
    j = pl.program_id(1)
    tm = x_ref.shape[0]
    g = g_ref[...]

    def norm_rows(r0, nrows, dst_ref, s_ref, t_ref):
        x = x_ref[pl.ds(r0, nrows), :]
        ms = jnp.mean(x * x, axis=-1, keepdims=True)
        y = x * lax.rsqrt(ms + EPS) * g
        dst_ref[pl.ds(r0, nrows), :] = (y * (1.0 + s_ref[0]) + t_ref[0]).astype(BF16)

    @pl.when((i == 0) & (j == 0))
    def _():
        def body(ci, carry):
            norm_rows(pl.multiple_of(ci * rows_per, BF16_SUBLANES), rows_per, ha_ref, sc_ref, sh_ref)
            return carry
        lax.fori_loop(0, tm // rows_per, body, 0)
        if tm % rows_per:
            norm_rows(tm - rows_per, rows_per, ha_ref, sc_ref, sh_ref)

    c = jnp.clip(j - 1, 0, n_chunks - 1)
    r0 = pl.multiple_of(jnp.minimum(c * rows_per, tm - rows_per), BF16_SUBLANES)

    def step(cur_ref, nxt_ref):
        if has_gate:
            @pl.when(j == 0)
            def _():
                og_ref[...] = _dot(cur_ref[...], wg_ref[...])
        o_ref[...] = _dot(cur_ref[...], w_ref[...]).astype(o_ref.dtype)
        norm_rows(r0, rows_per, nxt_ref, scn_ref, shn_ref)

    @pl.when(i % 2 == 0)
    def _():
        step(ha_ref, hb_ref)

    @pl.when(i % 2 == 1)
    def _():
        step(hb_ref, ha_ref)


def in_proj(x, g, sc, sh, w, n_out, wg, T, tm=1024, tn=512):
    N, D = x.shape
    Nout = n_out
    tm = min(tm, T)
    nb, nj = N // tm, Nout // tn
    assert N % tm == 0 and T % tm == 0 and Nout % tn == 0 and nj >= 2
    n_chunks = nj - 1
    rows_per = -(-tm // n_chunks)
    rows_per = -(-rows_per // BF16_SUBLANES) * BF16_SUBLANES
    nxt = lambda i: jnp.minimum(i + 1, nb - 1)
    xmap = lambda i, j: (jnp.where((i == 0) & (j == 0), 0, nxt(i)), 0)
    bmap = lambda i, j: ((i * tm) // T, 0, 0)
    nmap = lambda i, j: ((nxt(i) * tm) // T, 0, 0)
    in_specs = [pl.BlockSpec((tm, D), xmap),
                pl.BlockSpec((1, D), lambda i, j: (0, 0)),
                pl.BlockSpec((1, 1, D), bmap), pl.BlockSpec((1, 1, D), bmap),
                pl.BlockSpec((1, 1, D), nmap), pl.BlockSpec((1, 1, D), nmap),
                pl.BlockSpec((D, tn), lambda i, j: (0, j))]
    out_specs = [pl.BlockSpec((tm, tn), lambda i, j: (i, j))]
    out_shape = [jax.ShapeDtypeStruct((N, Nout), BF16)]
    args = [x, g.reshape(1, D), sc, sh, sc, sh, w]
    if wg is not None:
        ng = wg.shape[1]
        in_specs.append(pl.BlockSpec((D, ng), lambda i, j: (0, 0)))
        out_specs.append(pl.BlockSpec((tm, ng), lambda i, j: (i, 0)))
        out_shape.append(jax.ShapeDtypeStruct((N, ng), F32))
        args.append(wg)
    return pl.pallas_call(
        functools.partial(_inproj_kernel, has_gate=wg is not None, rows_per=rows_per, n_chunks=n_chunks),
        grid=(nb, nj),
        in_specs=in_specs, out_specs=out_specs, out_shape=out_shape,
        scratch_shapes=[pltpu.VMEM((tm, D), BF16), pltpu.VMEM((tm, D), BF16)],
        compiler_params=_params(("arbitrary", "arbitrary")),
        name="in_proj",
    )(*args)


def _outproj_kernel(ya_ref, yb_ref, wa_ref, wb_ref, x_ref, gate_ref, o_ref):
    y = _dot(ya_ref[...], wa_ref[...]) + _dot(yb_ref[...], wb_ref[...])
    o_ref[...] = x_ref[...] + gate_ref[0] * y


def out_proj(ya, yb, w_out, x, gate, T, tm=512, tn=2048):
    N, D = x.shape
    Ka, Kb = ya.shape[1], yb.shape[1]
    tm = min(tm, T)
    assert N % tm == 0 and T % tm == 0 and D % tn == 0 and Ka % Kb == 0
    return pl.pallas_call(
        _outproj_kernel,
        grid=(N // tm, D // tn),
        in_specs=[pl.BlockSpec((tm, Ka), lambda i, j: (i, 0)),
                  pl.BlockSpec((tm, Kb), lambda i, j: (i, 0)),
                  pl.BlockSpec((Ka, tn), lambda i, j: (0, j)),
                  pl.BlockSpec((Kb, tn), lambda i, j: (Ka // Kb, j)),
                  pl.BlockSpec((tm, tn), lambda i, j: (i, j)),
                  pl.BlockSpec((1, 1, tn), lambda i, j: ((i * tm) // T, 0, j))],
        out_specs=pl.BlockSpec((tm, tn), lambda i, j: (i, j)),
        out_shape=jax.ShapeDtypeStruct((N, D), F32),
        compiler_params=_params(("parallel", "parallel")),
        name="out_proj",
    )(ya, yb, w_out, w_out, x, gate)


FFN_HALO = BF16_SUBLANES


def _to_tiles(a):
    rows = a.shape[0]
    return jnp.swapaxes(a.reshape(8, rows // 8, LANES), 0, 1).reshape(rows, LANES)


def _from_tiles(a):
    rows = a.shape[0]
    return jnp.swapaxes(a.reshape(rows // 8, 8, LANES), 0, 1).reshape(rows, LANES)


def _ffn_kernel(x_ref, xh_ref, g_ref, sc_ref, sh_ref, gate_ref, wa_ref, wb_ref, cwa_ref, cwb_ref,
                wd_ref, o_ref, h_ref, xp_ref, acc_ref, ya_ref, yb_ref, *, blocks_per_seq, chunk, sub):
    i = pl.program_id(0)
    j = pl.program_id(1)
    tm = x_ref.shape[0]

    K = tm // 8

    def norm_mod(xr, g, scale1, shift):
        ms = jnp.mean(xr * xr, axis=-1, keepdims=True)
        return (xr * lax.rsqrt(ms + EPS) * g) * scale1 + shift

    @pl.when(j == 0)
    def _():
        g = g_ref[...]
        scale1 = 1.0 + sc_ref[0]
        shift = sh_ref[0]
        for cb in range(x_ref.shape[1] // LANES):
            cols = slice(cb * LANES, (cb + 1) * LANES)
            xp_ref[:, cols] = _to_tiles(x_ref[:, cols])
        _norm_mod_rows(xp_ref, g, scale1, shift, h_ref, FFN_HALO, tm, chunk)
        first = (i % blocks_per_seq) == 0
        h_ref[0:FFN_HALO, :] = jnp.where(first, 0.0, norm_mod(xh_ref[...], g, scale1, shift)).astype(BF16)
        acc_ref[...] = jnp.zeros_like(acc_ref)

    h = h_ref[...]
    tf = wa_ref.shape[2]
    nsub = tf // sub
    sub0 = lax.broadcasted_iota(jnp.int32, (8, sub), 0) == 0

    def conv(y_ref, cw_ref, cs):
        H = FFN_HALO
        last = pltpu.roll(y_ref[H + tm - 8:H + tm, :], 1, axis=0)
        last2 = pltpu.roll(y_ref[H + tm - 16:H + tm - 8, :], 1, axis=0)
        m1 = jnp.where(sub0, y_ref[H - 1:H, :], last)
        m2 = jnp.where(sub0, y_ref[H - 2:H - 1, :], last2)
        y_ref[H - 8:H, :] = m1
        y_ref[H - 16:H - 8, :] = m2
        cw = cw_ref[0, :, cs]
        out = cw[FFN_CONV_WIDTH:FFN_CONV_WIDTH + 1, :]
        for k in range(FFN_CONV_WIDTH):
            off = H - 8 * (FFN_CONV_WIDTH - 1 - k)
            out = out + cw[k:k + 1, :] * y_ref[off:off + tm, :]
        return out

    for c in range(nsub):
        cs = slice(c * sub, (c + 1) * sub)
        ya_ref[c] = _dot(h, wa_ref[0, :, cs])
        yb_ref[c] = _dot(h, wb_ref[0, :, cs])
    for c in range(nsub):
        cs = slice(c * sub, (c + 1) * sub)
        act = (_silu(conv(ya_ref.at[c], cwa_ref, cs)) * conv(yb_ref.at[c], cwb_ref, cs)).astype(BF16)
        d = _dot(act, wd_ref[0, cs, :])
        for cb in range(acc_ref.shape[0]):
            acc_ref[cb] += d[:, cb * LANES:(cb + 1) * LANES]

    @pl.when(j == pl.num_programs(1) - 1)
    def _():
        for cb in range(acc_ref.shape[0]):
            cols = slice(cb * LANES, (cb + 1) * LANES)
            o_ref[:, cols] = x_ref[:, cols] + gate_ref[0, :, cols] * _from_tiles(acc_ref[cb])


def conv_ffn(x, g, sc, sh, gate, w_up, cw, w_down, layer, T, tm=512, tf=512, sub=256):
    N, D = x.shape
    DFF = w_down.shape[1]
    tm = min(tm, T)
    assert N % tm == 0 and T % tm == 0 and DFF % tf == 0 and tm % FFN_HALO == 0
    sub = min(sub, tf)
    nff = DFF // tf
    hb = tm // FFN_HALO
    bmap = lambda i, j: ((i * tm) // T, 0, 0)
    return pl.pallas_call(
        functools.partial(_ffn_kernel, blocks_per_seq=T // tm, chunk=min(128, tm), sub=min(sub, tf)),
        grid=(N // tm, nff),
        in_specs=[pl.BlockSpec((tm, D), lambda i, j: (i, 0)),
                  pl.BlockSpec((FFN_HALO, D), lambda i, j: (jnp.maximum(i * hb - 1, 0), 0)),
                  pl.BlockSpec((1, D), lambda i, j: (0, 0)),
                  pl.BlockSpec((1, 1, D), bmap),
                  pl.BlockSpec((1, 1, D), bmap),
                  pl.BlockSpec((1, 1, D), bmap),
                  pl.BlockSpec((1, D, tf), lambda i, j: (layer, 0, j)),
                  pl.BlockSpec((1, D, tf), lambda i, j: (layer, 0, j + nff)),
                  pl.BlockSpec((1, 8, tf), lambda i, j: (layer, 0, j)),
                  pl.BlockSpec((1, 8, tf), lambda i, j: (layer, 0, j + nff)),
                  pl.BlockSpec((1, tf, D), lambda i, j: (layer, j, 0))],
        out_specs=pl.BlockSpec((tm, D), lambda i, j: (i, 0)),
        out_shape=jax.ShapeDtypeStruct((N, D), F32),
        scratch_shapes=[pltpu.VMEM((tm + FFN_HALO, D), BF16), pltpu.VMEM((tm, D), F32),
                        pltpu.VMEM((D // LANES, tm, LANES), F32),
                        pltpu.VMEM((tf // sub, tm + FFN_HALO, sub), F32),
                        pltpu.VMEM((tf // sub, tm + FFN_HALO, sub), F32)],
        compiler_params=_params(("parallel", "arbitrary")),
        name="conv_ffn",
    )(x, x, g.reshape(1, D), sc, sh, gate, w_up, w_up, cw, cw, w_down)


KV_PLAIN = -1
KV_TRANSPOSED = -2
KV_TILE = 256
SEL_UNROLL = 2


def _rope(y, cos, sin):
    return y * cos + pltpu.roll(y, HEAD_DIM // 2, axis=1) * sin


def _head_norm(x, g):
    ms = jnp.mean(x * x, axis=-1, keepdims=True)
    return x * lax.rsqrt(ms + EPS) * g


def _prep_kernel(*refs, n_q_heads, kv_kinds, n_groups):
    nkv = len(kv_kinds)
    zq_ref = refs[0]
    kv_refs = refs[1:1 + nkv]
    cos_ref, sin_ref, qn_ref, kn_ref = refs[1 + nkv:5 + nkv]
    q_out = refs[5 + nkv]
    kv_out = refs[6 + nkv:]
    cos = cos_ref[0]
    sin = sin_ref[0]
    scale = HEAD_DIM ** -0.5 * LOG2E
    for hd in range(n_q_heads):
        sl = slice(hd * HEAD_DIM, (hd + 1) * HEAD_DIM)
        y = _rope(_head_norm(zq_ref[0, :, sl].astype(F32), qn_ref[...]), cos, sin)
        q_out[0, :, sl] = (y * scale).astype(BF16)
    for idx, kind in enumerate(kv_kinds):
        for gi in range(n_groups):
            sl = slice(gi * HEAD_DIM, (gi + 1) * HEAD_DIM)
            a = kv_refs[idx][0, :, sl]
            if kind >= 0:
                a = _rope(_head_norm(a.astype(F32), kn_ref[kind:kind + 1, :]), cos, sin)
            if kind == KV_TRANSPOSED:
                for c in range(a.shape[0] // KV_TILE):
                    kv_out[idx][0, gi, c] = a[c * KV_TILE:(c + 1) * KV_TILE, :].astype(F32).T.astype(BF16)
            else:
                kv_out[idx][0, gi] = a.astype(BF16)


def attn_prep(z3, cosf, sinf, q_norm, k_norm, q_col, kv_cols, kv_kinds, n_groups, tm=512):
    B, T, _ = z3.shape
    tm = min(tm, T)
    QW = 8 * HEAD_DIM
    KW = n_groups * HEAD_DIM
    assert q_col % QW == 0 and all(c % KW == 0 for c in kv_cols) and T % tm == 0
    nkv = len(kv_cols)
    in_specs = [pl.BlockSpec((1, tm, QW), lambda b, t: (b, t, q_col // QW))]
    for c in kv_cols:
        in_specs.append(pl.BlockSpec((1, tm, KW), functools.partial(lambda b, t, cb: (b, t, cb), cb=c // KW)))
    in_specs += [pl.BlockSpec((1, tm, HEAD_DIM), lambda b, t: (b, t, 0))] * 2
    in_specs += [pl.BlockSpec((1, HEAD_DIM), lambda b, t: (0, 0)),
                 pl.BlockSpec(k_norm.shape, lambda b, t: (0, 0))]
    out_specs = [pl.BlockSpec((1, tm, QW), lambda b, t: (b, t, 0))]
    out_shape = [jax.ShapeDtypeStruct((B, T, QW), BF16)]
    for kind in kv_kinds:
        if kind == KV_TRANSPOSED:
            assert tm % KV_TILE == 0
            out_specs.append(pl.BlockSpec((1, n_groups, tm // KV_TILE, HEAD_DIM, KV_TILE),
                                          lambda b, t: (b, 0, t, 0, 0)))
            out_shape.append(jax.ShapeDtypeStruct((B, n_groups, T // KV_TILE, HEAD_DIM, KV_TILE), BF16))
        else:
            out_specs.append(pl.BlockSpec((1, n_groups, tm, HEAD_DIM), lambda b, t: (b, 0, t, 0)))
            out_shape.append(jax.ShapeDtypeStruct((B, n_groups, T, HEAD_DIM), BF16))
    return pl.pallas_call(
        functools.partial(_prep_kernel, n_q_heads=QW // HEAD_DIM, kv_kinds=tuple(kv_kinds), n_groups=n_groups),
        grid=(B, T // tm),
        in_specs=in_specs, out_specs=out_specs, out_shape=out_shape,
        compiler_params=_params(("parallel", "parallel")),
        name="attn_prep",
    )(*([z3] * (1 + nkv)), cosf, sinf, q_norm.reshape(1, HEAD_DIM), k_norm)


def _compress_kernel(ak_ref, av_ref, w1k_ref, w1v_ref, pk_ref, pv_ref, w2k_ref, w2v_ref,
                     kn_ref, cos_ref, sin_ref, kc_ref, vc_ref):
    hid = NSA_CMP_HIDDEN

    def mlp(a_ref, w1_ref, p_ref, w2_ref):
        P = _dot(a_ref[0, 0], w1_ref[...])
        Q = _dot(p_ref[...], w1_ref[...])
        pb = Q[0:1, :hid] + Q[1:2, hid:]
        nxt = pltpu.roll(P[:, hid:], P.shape[0] - 1, axis=0)
        hdn = _silu(P[:, :hid] + nxt + pb)
        return _dot(hdn.astype(BF16), w2_ref[...])

    kc = mlp(ak_ref, w1k_ref, pk_ref, w2k_ref)
    kc = _rope(_head_norm(kc, kn_ref[...]), cos_ref[0], sin_ref[0])
    kc_ref[0, 0] = kc.astype(BF16)
    vc_ref[0, 0] = mlp(av_ref, w1v_ref, pv_ref, w2v_ref).T.astype(BF16)


def nsa_compress(kcr, vcr, w1k, w1v, pk, pv, w2k, w2v, kn0, cos_end, sin_end):
    B, G, T, dh = kcr.shape
    nch = T // NSA_CMP_STRIDE
    cw = NSA_CMP_STRIDE * dh
    a_spec = pl.BlockSpec((1, 1, nch, cw), lambda b, g: (b, g, 0, 0))
    full = lambda arr: pl.BlockSpec(arr.shape, lambda b, g: (0,) * arr.ndim)
    tab_spec = pl.BlockSpec((1, nch, dh), lambda b, g: (b, 0, 0))
    o_spec = pl.BlockSpec((1, 1, nch, dh), lambda b, g: (b, g, 0, 0))
    return pl.pallas_call(
        _compress_kernel,
        grid=(B, G),
        in_specs=[a_spec, a_spec, full(w1k), full(w1v), full(pk), full(pv), full(w2k), full(w2v),
                  full(kn0), tab_spec, tab_spec],
        out_specs=[o_spec, pl.BlockSpec((1, 1, dh, nch), lambda b, g: (b, g, 0, 0))],
        out_shape=[jax.ShapeDtypeStruct((B, G, nch, dh), BF16), jax.ShapeDtypeStruct((B, G, dh, nch), BF16)],
        compiler_params=_params(("parallel", "parallel")),
        name="nsa_compress",
    )(kcr.reshape(B, G, nch, cw), vcr.reshape(B, G, nch, cw), w1k, w1v, pk, pv, w2k, w2v,
      kn0, cos_end, sin_end)


def _nsa_kernel(q_ref, kc_ref, vct_ref, ks_ref, vst_ref, kw_ref, vwt_ref, gl_ref, ov_ref, ext_ref,
                o_ref, m_ref, l_ref, acc_ref, *, n_rep, n_cmp, n_sel, n_top):
    tq = q_ref.shape[1]
    tk = KV_TILE
    nrow = kc_ref.shape[2]
    t0 = pl.program_id(2) * tq
    q4 = jnp.concatenate([q_ref[0, :, r * HEAD_DIM:(r + 1) * HEAD_DIM] for r in range(n_rep)], axis=0)

    def heads(a):
        return jnp.concatenate([a] * n_rep, axis=1)

    jb = lax.broadcasted_iota(jnp.int32, (nrow, tq), 0)
    tcol = t0 + lax.broadcasted_iota(jnp.int32, (nrow, tq), 1)
    valid_c = (jb * NSA_CMP_STRIDE + (NSA_CMP_LEN - 1) <= tcol) & (jb < n_cmp)
    validf = heads(jnp.where(valid_c, 1.0, 0.0))
    s = _dot_nt(kc_ref[0, 0], q4) + heads(jnp.where(valid_c, 0.0, NEG))
    e = jnp.exp2(s - jnp.max(s, axis=0, keepdims=True))
    p = e / jnp.sum(e, axis=0, keepdims=True) * validf
    o_cmp = _dot(vct_ref[0, 0], p.astype(BF16))
    psum = p[:, 0:tq]
    for r in range(1, n_rep):
        psum = psum + p[:, r * tq:(r + 1) * tq]

    hi = psum.astype(BF16)
    lo = (psum - hi.astype(F32)).astype(BF16)
    imp = _dot(ov_ref[...], hi) + _dot(ov_ref[...], lo)
    nsr = ov_ref.shape[0]
    jb = lax.broadcasted_iota(jnp.int32, (nsr, tq), 0)
    tcol = t0 + lax.broadcasted_iota(jnp.int32, (nsr, tq), 1)
    cur = lax.shift_right_logical(tcol, int(np.log2(NSA_SEL_LEN)))
    forced = (jb == 0) | (jb == cur) | (jb == cur - 1)
    score = jnp.where(forced, NSA_FORCE, jnp.where(jb * NSA_SEL_LEN <= tcol, imp, -1.0))
    score = jnp.where(jb < n_sel, score, -2.0)
    selT = jnp.zeros((nsr, tq), F32)
    for _ in range(n_top):
        best = jnp.max(score, axis=0, keepdims=True)
        idx = jnp.min(jnp.where(score == best, jb, nsr), axis=0, keepdims=True)
        hit = jb == idx
        selT = jnp.where(hit, 1.0, selT)
        score = jnp.where(hit, -3.0, score)
    sel = selT.astype(BF16)

    m_ref[...] = jnp.full(m_ref.shape, NEG, F32)
    l_ref[...] = jnp.zeros(l_ref.shape, F32)
    acc_ref[...] = jnp.zeros(acc_ref.shape, F32)
    krow = lax.broadcasted_iota(jnp.int32, (tk, tq), 0)
    tq_col = t0 + lax.broadcasted_iota(jnp.int32, (tk, tq), 1)

    def sel_body(it, carry):
        s = []
        for u in range(SEL_UNROLL):
            kt = it * SEL_UNROLL + u
            k0 = pl.multiple_of(kt * tk, tk)
            mask = (_dot(ext_ref[kt], sel) > 0.5) & (k0 + krow <= tq_col)
            s.append(_dot_nt(ks_ref[0, 0, pl.ds(k0, tk), :], q4) + heads(jnp.where(mask, 0.0, NEG)))
        m_old = m_ref[...]
        m_new = m_old
        for u in range(SEL_UNROLL):
            m_new = jnp.maximum(m_new, jnp.max(s[u], axis=0, keepdims=True))
        alpha = jnp.exp2(m_old - m_new)
        lsum = alpha * l_ref[...]
        pv = alpha * acc_ref[...]
        for u in range(SEL_UNROLL):
            p = jnp.exp2(s[u] - m_new)
            lsum = lsum + jnp.sum(p, axis=0, keepdims=True)
            pv = pv + _dot(vst_ref[0, 0, it * SEL_UNROLL + u], p.astype(BF16))
        l_ref[...] = lsum
        acc_ref[...] = pv
        m_ref[...] = m_new
        return carry

    n_tiles = (t0 + tq) // tk
    lax.fori_loop(0, (n_tiles + SEL_UNROLL - 1) // SEL_UNROLL, sel_body, 0)
    o_sel = acc_ref[...] / l_ref[...]

    W = NSA_WINDOW
    n_win = (W + tq) // tk
    kstart = pl.multiple_of(jnp.maximum(t0 - W, 0), tk)
    jt = kstart // tk
    sw = []
    for c in range(n_win):
        rel = tq_col - (kstart + c * tk) - krow
        bias = heads(jnp.where((rel >= 0) & (rel < W), 0.0, NEG))
        sw.append(_dot_nt(kw_ref[0, 0, pl.ds(kstart + c * tk, tk), :], q4) + bias)
    m = jnp.max(sw[0], axis=0, keepdims=True)
    for c in range(1, n_win):
        m = jnp.maximum(m, jnp.max(sw[c], axis=0, keepdims=True))
    den = jnp.zeros_like(m)
    o_win = jnp.zeros((HEAD_DIM, n_rep * tq), F32)
    for c in range(n_win):
        ew = jnp.exp2(sw[c] - m)
        den = den + jnp.sum(ew, axis=0, keepdims=True)
        o_win = o_win + _dot(vwt_ref[0, 0, jt + c], ew.astype(BF16))
    o_win = o_win / den

    gT = _sigmoid(gl_ref[0]).T
    for r in range(n_rep):
        hs = slice(r * tq, (r + 1) * tq)
        c3 = 3 * r
        oT = (gT[c3:c3 + 1, :] * o_cmp[:, hs] + gT[c3 + 1:c3 + 2, :] * o_sel[:, hs]
              + gT[c3 + 2:c3 + 3, :] * o_win[:, hs])
        o_ref[0, :, r * HEAD_DIM:(r + 1) * HEAD_DIM] = oT.T.astype(BF16)


def _nsa_kernel_v2(q_ref, kc_ref, vct_ref, ks_ref, vst_ref, kw_ref, vwt_ref, gl_ref, ov_ref, ext_ref,
                o_ref, m_ref, l_ref, acc_ref, oc_ref, *, n_rep, n_cmp, n_sel, n_top):
    tq = q_ref.shape[1]
    tk = KV_TILE
    nrow = kc_ref.shape[2]
    t0 = pl.program_id(2) * tq

    def q_head(r):
        return q_ref[0, :, r * HEAD_DIM:(r + 1) * HEAD_DIM]

    kc = kc_ref[0, 0]
    vct = vct_ref[0, 0]
    jb = lax.broadcasted_iota(jnp.int32, (nrow, tq), 0)
    tcol = t0 + lax.broadcasted_iota(jnp.int32, (nrow, tq), 1)
    valid_c = (jb * NSA_CMP_STRIDE + (NSA_CMP_LEN - 1) <= tcol) & (jb < n_cmp)
    psum = jnp.zeros((nrow, tq), F32)
    for r in range(n_rep):
        sT = jnp.where(valid_c, _dot_nt(kc, q_head(r)), NEG)
        e = jnp.exp(sT - jnp.max(sT, axis=0, keepdims=True))
        p = jnp.where(valid_c, e / jnp.sum(e, axis=0, keepdims=True), 0.0)
        psum = psum + p
        oc_ref[r] = _dot(vct, p.astype(BF16))

    hi = psum.astype(BF16)
    lo = (psum - hi.astype(F32)).astype(BF16)
    imp = _dot(ov_ref[...], hi) + _dot(ov_ref[...], lo)
    cur = lax.shift_right_logical(tcol, int(np.log2(NSA_SEL_LEN)))
    forced = (jb == 0) | (jb == cur) | (jb == cur - 1)
    score = jnp.where(forced, NSA_FORCE, jnp.where(jb * NSA_SEL_LEN <= tcol, imp, -1.0))
    score = jnp.where(jb < n_sel, score, -2.0)
    selT = jnp.zeros((nrow, tq), F32)
    for _ in range(n_top):
        best = jnp.max(score, axis=0, keepdims=True)
        idx = jnp.min(jnp.where(score == best, jb, nrow), axis=0, keepdims=True)
        hit = jb == idx
        selT = jnp.where(hit, 1.0, selT)
        score = jnp.where(hit, -3.0, score)
    sel = selT.astype(BF16)

    m_ref[...] = jnp.full(m_ref.shape, NEG, F32)
    l_ref[...] = jnp.zeros(l_ref.shape, F32)
    acc_ref[...] = jnp.zeros(acc_ref.shape, F32)
    krow = lax.broadcasted_iota(jnp.int32, (tk, tq), 0)
    tq_col = t0 + lax.broadcasted_iota(jnp.int32, (tk, tq), 1)

    def sel_body(kt, carry):
        k0 = pl.multiple_of(kt * tk, tk)
        k = ks_ref[0, 0, pl.ds(k0, tk), :]
        vt = vst_ref[0, 0, kt]
        mask = (_dot(ext_ref[kt], sel) > 0.5) & (k0 + krow <= tq_col)
        for r in range(n_rep):
            s = jnp.where(mask, _dot_nt(k, q_head(r)), NEG)
            m_old = m_ref[r]
            m_new = jnp.maximum(m_old, jnp.max(s, axis=0, keepdims=True))
            alpha = jnp.exp(m_old - m_new)
            p = jnp.exp(s - m_new)
            l_ref[r] = alpha * l_ref[r] + jnp.sum(p, axis=0, keepdims=True)
            acc_ref[r] = alpha * acc_ref[r] + _dot(vt, p.astype(BF16))
            m_ref[r] = m_new
        return carry

    lax.fori_loop(0, (t0 + tq) // tk, sel_body, 0)

    W = NSA_WINDOW
    kstart = pl.multiple_of(jnp.maximum(t0 - W, 0), tk)
    jt = kstart // tk
    wk = [kw_ref[0, 0, pl.ds(kstart + c * tk, tk), :] for c in range(2)]
    wvt = [vwt_ref[0, 0, jt + c] for c in range(2)]
    wmask = []
    for c in range(2):
        rel = tq_col - (kstart + c * tk) - krow
        wmask.append((rel >= 0) & (rel < W))
    gT = _sigmoid(gl_ref[0]).T
    for r in range(n_rep):
        s = [jnp.where(wmask[c], _dot_nt(wk[c], q_head(r)), NEG) for c in range(2)]
        m = jnp.maximum(jnp.max(s[0], axis=0, keepdims=True), jnp.max(s[1], axis=0, keepdims=True))
        e = [jnp.exp(s[c] - m) for c in range(2)]
        den = jnp.sum(e[0], axis=0, keepdims=True) + jnp.sum(e[1], axis=0, keepdims=True)
        o_win = (_dot(wvt[0], e[0].astype(BF16)) + _dot(wvt[1], e[1].astype(BF16))) / den
        o_sel = acc_ref[r] / l_ref[r]
        c3 = 3 * r
        oT = gT[c3:c3 + 1, :] * oc_ref[r] + gT[c3 + 1:c3 + 2, :] * o_sel + gT[c3 + 2:c3 + 3, :] * o_win
        o_ref[0, :, r * HEAD_DIM:(r + 1) * HEAD_DIM] = oT.T.astype(BF16)


def nsa_attention(qn, kc, vct, ks, vst, kw, vwt, gl3, tq=512):
    B, T, QW = qn.shape
    G = kc.shape[1]
    n_rep = QW // HEAD_DIM // G
    nrow = kc.shape[2]
    tk = KV_TILE
    n_cmp = (T - NSA_CMP_LEN) // NSA_CMP_STRIDE + 1
    n_sel = T // NSA_SEL_LEN
    assert T % tq == 0 and n_sel <= nrow and T >= NSA_WINDOW + tq and tq % tk == 0 and NSA_WINDOW % tk == 0
    assert (tq // tk) % SEL_UNROLL == 0
    starts = np.arange(nrow) * NSA_CMP_STRIDE
    sel_start = np.arange(nrow) * NSA_SEL_LEN
    nsr = -(-n_sel // BF16_SUBLANES) * BF16_SUBLANES
    ov = ((starts[None, :] <= sel_start[:, None] + NSA_SEL_LEN - 1)
          & (starts[None, :] + NSA_CMP_LEN - 1 >= sel_start[:, None])
          & (np.arange(nrow)[:, None] < n_sel) & (np.arange(nrow)[None, :] < n_cmp))[:nsr]
    ext = (np.arange(T)[:, None] // NSA_SEL_LEN == np.arange(nsr)[None, :]).reshape(T // tk, tk, nsr)
    k_spec = pl.BlockSpec((1, 1, T, HEAD_DIM), lambda b, g, i: (b, g, 0, 0))
    vt_spec = pl.BlockSpec((1, 1, T // tk, HEAD_DIM, tk), lambda b, g, i: (b, g, 0, 0, 0))
    c_spec = pl.BlockSpec((1, 1, nrow, HEAD_DIM), lambda b, g, i: (b, g, 0, 0))
    hw = n_rep * HEAD_DIM
    return pl.pallas_call(
        functools.partial(_nsa_kernel, n_rep=n_rep, n_cmp=n_cmp, n_sel=n_sel, n_top=min(NSA_TOPK, n_sel)),
        grid=(B, G, T // tq),
        in_specs=[pl.BlockSpec((1, tq, hw), lambda b, g, i: (b, i, g)),
                  c_spec, pl.BlockSpec((1, 1, HEAD_DIM, nrow), lambda b, g, i: (b, g, 0, 0)),
                  k_spec, vt_spec, k_spec, vt_spec,
                  pl.BlockSpec((1, tq, LANES), lambda b, g, i: (b, i, g)),
                  pl.BlockSpec((nsr, nrow), lambda b, g, i: (0, 0)),
                  pl.BlockSpec((T // tk, tk, nsr), lambda b, g, i: (0, 0, 0))],
        out_specs=pl.BlockSpec((1, tq, hw), lambda b, g, i: (b, i, g)),
        out_shape=jax.ShapeDtypeStruct((B, T, QW), BF16),
        scratch_shapes=[pltpu.VMEM((1, n_rep * tq), F32), pltpu.VMEM((1, n_rep * tq), F32),
                        pltpu.VMEM((HEAD_DIM, n_rep * tq), F32)],
        compiler_params=_params(("parallel", "parallel", "arbitrary")),
        name="nsa_attention",
    )(qn, kc, vct, ks, vst, kw, vwt, gl3, jnp.asarray(ov, BF16), jnp.asarray(ext, BF16))


def _nsa_kernel_old(q_ref, kc_ref, vc_ref, ks_ref, vs_ref, kw_ref, vw_ref, gl_ref, ov_ref, ex_ref,
                o_ref, m_ref, l_ref, acc_ref, oc_ref, *, n_rep, n_cmp, n_sel, n_top, tk):
    tq = q_ref.shape[1]
    nrow = kc_ref.shape[2]
    t0 = pl.program_id(2) * tq
    gates = _sigmoid(gl_ref[0])

    def q_head(r):
        return q_ref[0, :, r * HEAD_DIM:(r + 1) * HEAD_DIM]

    kc = kc_ref[0, 0]
    vc = vc_ref[0, 0]
    jb = lax.broadcasted_iota(jnp.int32, (nrow, tq), 0)
    tcol = t0 + lax.broadcasted_iota(jnp.int32, (nrow, tq), 1)
    valid_c = (jb * NSA_CMP_STRIDE + (NSA_CMP_LEN - 1) <= tcol) & (jb < n_cmp)
    psum = jnp.zeros((nrow, tq), F32)
    for r in range(n_rep):
        sT = jnp.where(valid_c, _dot_nt(kc, q_head(r)), NEG)
        e = jnp.exp(sT - jnp.max(sT, axis=0, keepdims=True))
        p = jnp.where(valid_c, e / jnp.sum(e, axis=0, keepdims=True), 0.0)
        psum = psum + p
        oc_ref[r] = _dot(p.T.astype(BF16), vc)

    hi = psum.astype(BF16)
    lo = (psum - hi.astype(F32)).astype(BF16)
    imp = _dot(ov_ref[...], hi) + _dot(ov_ref[...], lo)
    cur = lax.shift_right_logical(tcol, int(np.log2(NSA_SEL_LEN)))
    forced = (jb == 0) | (jb == cur) | (jb == cur - 1)
    score = jnp.where(forced, NSA_FORCE, jnp.where(jb * NSA_SEL_LEN <= tcol, imp, -1.0))
    score = jnp.where(jb < n_sel, score, -2.0)
    selT = jnp.zeros((nrow, tq), F32)
    for _ in range(n_top):
        best = jnp.max(score, axis=0, keepdims=True)
        idx = jnp.min(jnp.where(score == best, jb, nrow), axis=0, keepdims=True)
        hit = jb == idx
        selT = jnp.where(hit, 1.0, selT)
        score = jnp.where(hit, -3.0, score)
    sel = selT.T.astype(BF16)

    m_ref[...] = jnp.full(m_ref.shape, NEG, F32)
    l_ref[...] = jnp.zeros(l_ref.shape, F32)
    acc_ref[...] = jnp.zeros(acc_ref.shape, F32)
    trow = t0 + lax.broadcasted_iota(jnp.int32, (tq, tk), 0)
    kcol = lax.broadcasted_iota(jnp.int32, (tq, tk), 1)

    def sel_body(kt, carry):
        k0 = pl.multiple_of(kt * tk, tk)
        k = ks_ref[0, 0, pl.ds(k0, tk), :]
        v = vs_ref[0, 0, pl.ds(k0, tk), :]
        mask = (_dot(sel, ex_ref[kt]) > 0.5) & (k0 + kcol <= trow)
        for r in range(n_rep):
            s = jnp.where(mask, _dot_nt(q_head(r), k), NEG)
            m_old = m_ref[r][:, 0:1]
            m_new = jnp.maximum(m_old, jnp.max(s, axis=-1, keepdims=True))
            alpha = jnp.exp(m_old - m_new)
            p = jnp.exp(s - m_new)
            l_ref[r] = alpha * l_ref[r] + jnp.sum(p, axis=-1, keepdims=True)
            acc_ref[r] = alpha * acc_ref[r] + _dot(p.astype(BF16), v)
            m_ref[r] = jnp.broadcast_to(m_new, (tq, LANES))
        return carry

    lax.fori_loop(0, (t0 + tq + tk - 1) // tk, sel_body, 0)

    W = NSA_WINDOW
    kstart = pl.multiple_of(jnp.maximum(t0 - W, 0), tq)
    kw = kw_ref[0, 0, pl.ds(kstart, W + tq), :]
    vw = vw_ref[0, 0, pl.ds(kstart, W + tq), :]
    rel = (t0 + lax.broadcasted_iota(jnp.int32, (tq, W + tq), 0)
           - kstart - lax.broadcasted_iota(jnp.int32, (tq, W + tq), 1))
    wmask = (rel >= 0) & (rel < W)
    for r in range(n_rep):
        s = jnp.where(wmask, _dot_nt(q_head(r), kw), NEG)
        e = jnp.exp(s - jnp.max(s, axis=-1, keepdims=True))
        o_win = _dot(e.astype(BF16), vw) / jnp.sum(e, axis=-1, keepdims=True)
        o_sel = acc_ref[r] / l_ref[r][:, 0:1]
        c = 3 * r
        o = gates[:, c:c + 1] * oc_ref[r] + gates[:, c + 1:c + 2] * o_sel + gates[:, c + 2:c + 3] * o_win
        o_ref[0, :, r * HEAD_DIM:(r + 1) * HEAD_DIM] = o.astype(BF16)


def nsa_attention_old(qn, kc, vc, ks, vs, kw, vw, gl3, tq=256, tk=512):
    B, T, QW = qn.shape
    G = kc.shape[1]
    n_rep = QW // HEAD_DIM // G
    nrow = kc.shape[2]
    n_cmp = (T - NSA_CMP_LEN) // NSA_CMP_STRIDE + 1
    n_sel = T // NSA_SEL_LEN
    tk = min(tk, T)
    assert T % tq == 0 and T % tk == 0 and n_sel <= nrow and T >= NSA_WINDOW + tq and tq == NSA_WINDOW
    starts = np.arange(nrow) * NSA_CMP_STRIDE
    sel_start = np.arange(nrow) * NSA_SEL_LEN
    ov = ((starts[None, :] <= sel_start[:, None] + NSA_SEL_LEN - 1)
          & (starts[None, :] + NSA_CMP_LEN - 1 >= sel_start[:, None])
          & (np.arange(nrow)[:, None] < n_sel) & (np.arange(nrow)[None, :] < n_cmp))
    ex = (np.arange(T)[None, :] // NSA_SEL_LEN == np.arange(nrow)[:, None])
    ex = ex.reshape(nrow, T // tk, tk).transpose(1, 0, 2)
    kv_spec = pl.BlockSpec((1, 1, T, HEAD_DIM), lambda b, g, i: (b, g, 0, 0))
    c_spec = pl.BlockSpec((1, 1, nrow, HEAD_DIM), lambda b, g, i: (b, g, 0, 0))
    hw = n_rep * HEAD_DIM
    return pl.pallas_call(
        functools.partial(_nsa_kernel, n_rep=n_rep, n_cmp=n_cmp, n_sel=n_sel,
                          n_top=min(NSA_TOPK, n_sel), tk=tk),
        grid=(B, G, T // tq),
        in_specs=[pl.BlockSpec((1, tq, hw), lambda b, g, i: (b, i, g)),
                  c_spec, c_spec, kv_spec, kv_spec, kv_spec, kv_spec,
                  pl.BlockSpec((1, tq, LANES), lambda b, g, i: (b, i, g)),
                  pl.BlockSpec((nrow, nrow), lambda b, g, i: (0, 0)),
                  pl.BlockSpec((T // tk, nrow, tk), lambda b, g, i: (0, 0, 0))],
        out_specs=pl.BlockSpec((1, tq, hw), lambda b, g, i: (b, i, g)),
        out_shape=jax.ShapeDtypeStruct((B, T, QW), BF16),
        scratch_shapes=[pltpu.VMEM((n_rep, tq, LANES), F32), pltpu.VMEM((n_rep, tq, LANES), F32),
                        pltpu.VMEM((n_rep, tq, HEAD_DIM), F32), pltpu.VMEM((n_rep, tq, HEAD_DIM), F32)],
        compiler_params=_params(("parallel", "parallel", "arbitrary")),
        name="nsa_attention",
    )(qn, kc, vc, ks, vs, kw, vw, gl3, jnp.asarray(ov, BF16), jnp.asarray(ex, BF16))


def _swa_kernel(sink_ref, q_ref, k_ref, vt_ref, o_ref, *, n_rep):
    tq = q_ref.shape[1]
    tk = KV_TILE
    W = SWA_WINDOW
    g = pl.program_id(1)
    t0 = pl.program_id(2) * tq
    q4 = jnp.concatenate([q_ref[0, :, r * HEAD_DIM:(r + 1) * HEAD_DIM] for r in range(n_rep)], axis=0)
    sink = jnp.concatenate([jnp.full((1, tq), sink_ref[g * n_rep + r] * LOG2E, F32) for r in range(n_rep)], axis=1)
    kstart = pl.multiple_of(jnp.maximum(t0 - tk, 0), tk)
    jt = kstart // tk
    krow = lax.broadcasted_iota(jnp.int32, (tk, tq), 0)
    tq_col = t0 + lax.broadcasted_iota(jnp.int32, (tk, tq), 1)
    s = []
    for c in range(2):
        rel = tq_col - (kstart + c * tk) - krow
        bias = jnp.concatenate([jnp.where((rel >= 0) & (rel < W), 0.0, NEG)] * n_rep, axis=1)
        s.append(_dot_nt(k_ref[0, 0, pl.ds(kstart + c * tk, tk), :], q4) + bias)
    m = jnp.maximum(jnp.maximum(jnp.max(s[0], axis=0, keepdims=True), jnp.max(s[1], axis=0, keepdims=True)), sink)
    e = [jnp.exp2(s[c] - m) for c in range(2)]
    den = jnp.sum(e[0], axis=0, keepdims=True) + jnp.sum(e[1], axis=0, keepdims=True) + jnp.exp2(sink - m)
    oT = (_dot(vt_ref[0, 0, jt], e[0].astype(BF16)) + _dot(vt_ref[0, 0, jt + 1], e[1].astype(BF16))) / den
    for r in range(n_rep):
        o_ref[0, :, r * HEAD_DIM:(r + 1) * HEAD_DIM] = oT[:, r * tq:(r + 1) * tq].T.astype(BF16)


def swa_attention(qn, kn, vt, sinks, tq=256):
    B, T, QW = qn.shape
    G = kn.shape[1]
    n_rep = QW // HEAD_DIM // G
    tk = KV_TILE
    assert T % tq == 0 and tq == tk and SWA_WINDOW <= tk and T >= 2 * tk
    hw = n_rep * HEAD_DIM
    return pl.pallas_call(
        functools.partial(_swa_kernel, n_rep=n_rep),
        grid=(B, G, T // tq),
        in_specs=[pl.BlockSpec(memory_space=pltpu.SMEM),
                  pl.BlockSpec((1, tq, hw), lambda b, g, i: (b, i, g)),
                  pl.BlockSpec((1, 1, T, HEAD_DIM), lambda b, g, i: (b, g, 0, 0)),
                  pl.BlockSpec((1, 1, T // tk, HEAD_DIM, tk), lambda b, g, i: (b, g, 0, 0, 0))],
        out_specs=pl.BlockSpec((1, tq, hw), lambda b, g, i: (b, i, g)),
        out_shape=jax.ShapeDtypeStruct((B, T, QW), BF16),
        compiler_params=_params(("parallel", "parallel", "parallel")),
        name="swa_attention",
    )(sinks, qn, kn, vt)


CONF_HALO = 32


def _conformer_kernel(a1_ref, a2_ref, h1_ref, h2_ref, w_ref, b_ref, g_ref, be_ref, o_ref,
                      glu_ref, conv_ref, *, blocks_per_seq):
    tm, C = a1_ref.shape
    K = tm // 8
    PRE = CONF_HALO
    nctx = CONV_WIDTH - 1
    first = (pl.program_id(0) % blocks_per_seq) == 0

    gh = jnp.where(first, 0.0, h1_ref[...].astype(F32) * _sigmoid(h2_ref[...].astype(F32)))
    sub0 = lax.broadcasted_iota(jnp.int32, (8 * nctx, LANES), 0) % 8 == 0
    for cs in range(C // LANES):
        sl = slice(cs * LANES, (cs + 1) * LANES)
        g = _to_tiles(a1_ref[:, sl].astype(F32) * _sigmoid(a2_ref[:, sl].astype(F32)))
        glu_ref[cs, 8 * PRE:, :] = g
        moved = pltpu.roll(g[tm - 8 * nctx:, :], 1, axis=0)
        halo = jnp.concatenate([jnp.broadcast_to(gh[r:r + 1, sl], (8, LANES)) for r in range(PRE - nctx, PRE)],
                               axis=0)
        glu_ref[cs, 8 * (PRE - nctx):8 * PRE, :] = jnp.where(sub0, halo, moved)

    rb = 128

    def conv_group(gi, carry):
        r0 = pl.multiple_of(gi * rb, rb)
        for cs in range(C // LANES):
            sl = slice(cs * LANES, (cs + 1) * LANES)
            acc = jnp.broadcast_to(b_ref[:, sl], (rb, LANES))
            for k in range(CONV_WIDTH):
                off = 8 * (PRE - nctx + k)
                acc = acc + w_ref[k:k + 1, sl] * glu_ref[cs, pl.ds(r0 + off, rb), :]
            conv_ref[cs, pl.ds(r0, rb), :] = acc
        return carry

    lax.fori_loop(0, tm // rb, conv_group, 0)

    ns = C // LANES
    for r0 in range(0, tm, rb):
        y = [conv_ref[cs, r0:r0 + rb, :] for cs in range(ns)]
        mu = sum(jnp.sum(v, axis=-1, keepdims=True) for v in y) / C
        d = [v - mu for v in y]
        var = sum(jnp.sum(v * v, axis=-1, keepdims=True) for v in d) / C
        rs = lax.rsqrt(var + EPS)
        for cs in range(ns):
            sl = slice(cs * LANES, (cs + 1) * LANES)
            conv_ref[cs, r0:r0 + rb, :] = _silu(d[cs] * rs * g_ref[:, sl] + be_ref[:, sl])
    for cs in range(ns):
        o_ref[:, cs * LANES:(cs + 1) * LANES] = _from_tiles(conv_ref[cs]).astype(BF16)


def conformer_conv(z, conv_w, conv_b, ln_g, ln_b, T, tm=512):
    N = z.shape[0]
    C = conv_w.shape[1]
    tm = min(tm, T)
    assert N % tm == 0 and T % tm == 0 and tm % CONF_HALO == 0 and CONV_WIDTH - 1 <= min(CONF_HALO, tm // 8)
    hb = tm // CONF_HALO
    w = jnp.zeros((CONF_HALO, C), F32).at[:CONV_WIDTH].set(conv_w)
    hmap = lambda c: (lambda i: (jnp.maximum(i * hb - 1, 0), c))
    vec = pl.BlockSpec((1, C), lambda i: (0, 0))
    return pl.pallas_call(
        functools.partial(_conformer_kernel, blocks_per_seq=T // tm),
        grid=(N // tm,),
        in_specs=[pl.BlockSpec((tm, C), lambda i: (i, 0)), pl.BlockSpec((tm, C), lambda i: (i, 1)),
                  pl.BlockSpec((CONF_HALO, C), hmap(0)), pl.BlockSpec((CONF_HALO, C), hmap(1)),
                  pl.BlockSpec((CONF_HALO, C), lambda i: (0, 0)), vec, vec, vec],
        out_specs=pl.BlockSpec((tm, C), lambda i: (i, 0)),
        out_shape=jax.ShapeDtypeStruct((N, C), BF16),
        scratch_shapes=[pltpu.VMEM((C // LANES, tm + 8 * CONF_HALO, LANES), F32),
                        pltpu.VMEM((C // LANES, tm, LANES), F32)],
        compiler_params=_params(("parallel",)),
        name="conformer_conv",
    )(z, z, z, z, w, conv_b.reshape(1, C), ln_g.reshape(1, C), ln_b.reshape(1, C))


def _gmlp_kernel(u0_ref, u1_ref, v0_ref, v1_ref, g_ref, be_ref, ws_ref, bs_ref, o_ref, *, n_groups):
    tm, hw = v0_ref.shape
    C = 2 * hw
    ch = GMLP_CHUNK
    gv = [_gelu_tanh(v0_ref[...].astype(F32)), _gelu_tanh(v1_ref[...].astype(F32))]
    mu = (jnp.sum(gv[0], axis=-1, keepdims=True) + jnp.sum(gv[1], axis=-1, keepdims=True)) / C
    d = [gv[0] - mu, gv[1] - mu]
    var = (jnp.sum(d[0] * d[0], axis=-1, keepdims=True) + jnp.sum(d[1] * d[1], axis=-1, keepdims=True)) / C
    rs = lax.rsqrt(var + EPS)
    vn = [(d[hf] * rs * g_ref[:, hf * hw:(hf + 1) * hw] + be_ref[:, hf * hw:(hf + 1) * hw]).astype(BF16)
          for hf in range(2)]
    u_refs = [u0_ref, u1_ref]
    tril = lax.broadcasted_iota(jnp.int32, (ch, ch), 0) >= lax.broadcasted_iota(jnp.int32, (ch, ch), 1)
    gph = n_groups // 2
    for gi in range(n_groups):
        hf, col = gi // gph, (gi % gph) * HEAD_DIM
        w = jnp.where(tril, ws_ref[gi], 0.0).astype(BF16)
        for c in range(tm // ch):
            rows = slice(c * ch, (c + 1) * ch)
            sp = _dot(w, vn[hf][rows, col:col + HEAD_DIM]) + bs_ref[gi]
            u = _gelu_tanh(u_refs[hf][rows, col:col + HEAD_DIM].astype(F32))
            o_ref[rows, gi * HEAD_DIM:(gi + 1) * HEAD_DIM] = (u * sp).astype(BF16)


def chunked_gmlp(z, u_col, ln_g, ln_b, w_s, b_s, T, tm=256):
    N = z.shape[0]
    n_groups, ch, _ = w_s.shape
    C = n_groups * HEAD_DIM
    hw = C // 2
    tm = min(tm, T)
    assert N % tm == 0 and T % tm == 0 and tm % ch == 0 and u_col % hw == 0
    cb = u_col // hw
    bsb = jnp.broadcast_to(b_s[:, :, None], (n_groups, ch, HEAD_DIM))
    zspec = lambda k: pl.BlockSpec((tm, hw), lambda i: (i, cb + k))
    vec = pl.BlockSpec((1, C), lambda i: (0, 0))
    return pl.pallas_call(
        functools.partial(_gmlp_kernel, n_groups=n_groups),
        grid=(N // tm,),
        in_specs=[zspec(0), zspec(1), zspec(2), zspec(3), vec, vec,
                  pl.BlockSpec((n_groups, ch, ch), lambda i: (0, 0, 0)),
                  pl.BlockSpec((n_groups, ch, HEAD_DIM), lambda i: (0, 0, 0))],
        out_specs=pl.BlockSpec((tm, C), lambda i: (i, 0)),
        out_shape=jax.ShapeDtypeStruct((N, C), BF16),
        compiler_params=_params(("parallel",)),
        name="chunked_gmlp",
    )(z, z, z, z, ln_g.reshape(1, C), ln_b.reshape(1, C), w_s, bsb)


def _even_mixer(xf, B, T, g, sc, sh, cosf, sinf, w_in, w_out, conv_w, conv_b, conv_ln_g, conv_ln_b,
                q_norm, k_norm, cmp_k_pos, cmp_k_w1, cmp_k_w2, cmp_v_pos, cmp_v_w1, cmp_v_w2):
    D = xf.shape[1]
    G = NSA_KV_GROUPS
    C = conv_w.shape[1]
    QW = D // 2
    KW = G * HEAD_DIM
    n_rep = QW // HEAD_DIM // G
    main = 2 * C + QW + 6 * KW
    wg = w_in[:, main:].reshape(D, G, n_rep * 3)
    wg = jnp.pad(wg, ((0, 0), (0, 0), (0, LANES - n_rep * 3))).reshape(D, G * LANES)
    z, gl = in_proj(xf, g, sc, sh, w_in.astype(BF16), main, wg.astype(BF16), T)
    ya = conformer_conv(z, conv_w, conv_b, conv_ln_g, conv_ln_b, T)
    q_col = 2 * C
    kv_cols = [q_col + QW + k * KW for k in range(6)]
    qn, kcr, vcr, ks, vs, kw, vw = attn_prep(z.reshape(B, T, main), cosf, sinf, q_norm, k_norm,
                                             q_col, kv_cols,
                                             (KV_PLAIN, KV_PLAIN, 1, KV_TRANSPOSED, 2, KV_TRANSPOSED), G)
    half = NSA_CMP_STRIDE * HEAD_DIM
    hid = NSA_CMP_HIDDEN
    nch = T // NSA_CMP_STRIDE

    def w1cat(w1):
        return jnp.concatenate([w1[:half], w1[half:]], axis=1).astype(BF16)

    def posrows(pos):
        return jnp.zeros((8, half), F32).at[0].set(pos[:NSA_CMP_STRIDE].reshape(half)) \
                  .at[1].set(pos[NSA_CMP_STRIDE:].reshape(half)).astype(BF16)

    def at_block_ends(tab):
        e = tab[:, NSA_CMP_LEN - 1::NSA_CMP_STRIDE]
        return jnp.pad(e, ((0, 0), (0, nch - e.shape[1]), (0, 0)))

    kc, vc = nsa_compress(kcr, vcr, w1cat(cmp_k_w1), w1cat(cmp_v_w1), posrows(cmp_k_pos), posrows(cmp_v_pos),
                          cmp_k_w2.astype(BF16), cmp_v_w2.astype(BF16), k_norm[0:1],
                          at_block_ends(cosf), at_block_ends(sinf))
    yb = nsa_attention(qn, kc, vc, ks, vs, kw, vw, gl.reshape(B, T, G * LANES))
    return ya, yb.reshape(B * T, QW), w_out.astype(BF16)


def _odd_mixer(xf, B, T, g, sc, sh, cosf, sinf, w_in, w_out, q_norm, k_norm, sinks, ln_g, ln_b, w_s, b_s):
    D = xf.shape[1]
    G = SWA_KV_HEADS
    QW = D // 2
    KW = G * HEAD_DIM
    z, = in_proj(xf, g, sc, sh, w_in.astype(BF16), w_in.shape[1], None, T)
    qn, kn, v = attn_prep(z.reshape(B, T, z.shape[1]), cosf, sinf, q_norm, k_norm.reshape(1, HEAD_DIM),
                          0, [QW, QW + KW], (0, KV_TRANSPOSED), G)
    yc = swa_attention(qn, kn, v, sinks)
    yd = chunked_gmlp(z, QW + 2 * KW, ln_g, ln_b, w_s, b_s, T)
    return yc.reshape(B * T, QW), yd, w_out.astype(BF16)


def kernel(x, c, positions, ada_w, ada_b, norm_g, ffn_w_up, ffn_conv_w, ffn_conv_b, ffn_w_down, ev_w_in, ev_w_out, ev_conv_w, ev_conv_b, ev_conv_ln_g, ev_conv_ln_b, ev_q_norm, ev_k_norm, ev_cmp_k_pos, ev_cmp_k_w1, ev_cmp_k_w2, ev_cmp_v_pos, ev_cmp_v_w1, ev_cmp_v_w2, od_w_in, od_w_out, od_q_norm, od_k_norm, od_sinks, od_gmlp_ln_g, od_gmlp_ln_b, od_gmlp_w_s, od_gmlp_b_s):
    B, T, D = x.shape
    depth = ada_w.shape[0]
    cosf, sinf = rope_tables(positions)
    mod = adaln(c, ada_w, ada_b)
    xf = x.reshape(B * T, D)
    w_up_b = ffn_w_up.astype(BF16)
    w_down_b = ffn_w_down.astype(BF16)
    cw_all = jnp.zeros((depth, 8, ffn_conv_w.shape[2]), F32).at[:, :FFN_CONV_WIDTH].set(ffn_conv_w) \
                .at[:, FFN_CONV_WIDTH].set(ffn_conv_b)
    for i in range(depth):
        sh1, sc1, g1, sh2, sc2, g2 = [m.reshape(B, 1, D) for m in jnp.split(mod[i], 6, axis=-1)]
        j = i // 2
        if i % 2 == 0:
            ya, yb, w_out = _even_mixer(xf, B, T, norm_g[i, 0], sc1, sh1, cosf, sinf, ev_w_in[j], ev_w_out[j],
                                        ev_conv_w[j], ev_conv_b[j], ev_conv_ln_g[j], ev_conv_ln_b[j],
                                        ev_q_norm[j], ev_k_norm[j], ev_cmp_k_pos[j], ev_cmp_k_w1[j],
                                        ev_cmp_k_w2[j], ev_cmp_v_pos[j], ev_cmp_v_w1[j], ev_cmp_v_w2[j])
        else:
            ya, yb, w_out = _odd_mixer(xf, B, T, norm_g[i, 0], sc1, sh1, cosf, sinf, od_w_in[j], od_w_out[j],
                                       od_q_norm[j], od_k_norm[j], od_sinks[j], od_gmlp_ln_g[j],
                                       od_gmlp_ln_b[j], od_gmlp_w_s[j], od_gmlp_b_s[j])
        xf = out_proj(ya, yb, w_out, xf, g1, T)
        xf = conv_ffn(xf, norm_g[i, 1], sc2, sh2, g2, w_up_b, cw_all, w_down_b, i, T)
    return xf.reshape(B, T, D)
```

```python
import functools

import numpy as np
import jax
import jax.numpy as jnp
from jax import lax
from jax.experimental import pallas as pl
from jax.experimental.pallas import tpu as pltpu

F32 = jnp.float32
BF16 = jnp.bfloat16

HEAD_DIM = 128
ROPE_THETA = 10000.0
EPS = 1e-6
NEG = -1e30
LOG2E = 1.4426950408889634

CONV_WIDTH = 31
NSA_KV_GROUPS = 2
NSA_CMP_LEN = 32
NSA_CMP_STRIDE = 16
NSA_CMP_HIDDEN = 256
NSA_SEL_LEN = 64
NSA_TOPK = 8
NSA_WINDOW = 256
NSA_FORCE = 1e6
SWA_KV_HEADS = 2
SWA_WINDOW = 128
GMLP_CHUNK = 128
FFN_CONV_WIDTH = 3

V7X_VMEM_BYTES = 64 * 1024 * 1024
VMEM_LIMIT = V7X_VMEM_BYTES - 8 * 1024 * 1024
LANES = 128
BF16_SUBLANES = 16


def _params(sem):
    return pltpu.CompilerParams(dimension_semantics=sem, vmem_limit_bytes=VMEM_LIMIT)


def _dot(a, b):
    return jnp.dot(a, b, preferred_element_type=F32)


def _dot_nt(a, b):
    return lax.dot_general(a, b, (((1,), (1,)), ((), ())), preferred_element_type=F32)


def _sigmoid(x):
    return 1.0 / (1.0 + jnp.exp(-x))


def _silu(x):
    return x * _sigmoid(x)


def _gelu_tanh(x):
    return 0.5 * x * (1.0 + jnp.tanh(np.sqrt(2.0 / np.pi).astype(np.float32) * (x + 0.044715 * (x * x * x))))


def _rope_kernel(pos_ref, inv_ref, cos_ref, sin_ref):
    ang = pos_ref[0].astype(F32) * inv_ref[0:1, :]
    cos_ref[0] = jnp.cos(ang)
    sin_ref[0] = jnp.sin(ang) * inv_ref[1:2, :]


def rope_tables(positions):
    B, T = positions.shape
    inv = ROPE_THETA ** (-jnp.arange(0, HEAD_DIM, 2, dtype=F32) / HEAD_DIM)
    half = HEAD_DIM // 2
    sign = jnp.concatenate([-jnp.ones((half,), F32), jnp.ones((half,), F32)])
    tab = jnp.stack([jnp.concatenate([inv, inv]), sign])
    return pl.pallas_call(
        _rope_kernel,
        grid=(B,),
        in_specs=[pl.BlockSpec((1, T, 1), lambda b: (b, 0, 0)),
                  pl.BlockSpec((2, HEAD_DIM), lambda b: (0, 0))],
        out_specs=[pl.BlockSpec((1, T, HEAD_DIM), lambda b: (b, 0, 0))] * 2,
        out_shape=[jax.ShapeDtypeStruct((B, T, HEAD_DIM), F32)] * 2,
        compiler_params=_params(("parallel",)),
        name="rope_tables",
    )(positions.reshape(B, T, 1), tab)


def _adaln_kernel(c_ref, w_ref, b_ref, o_ref):
    ca = _silu(c_ref[...]).astype(BF16)
    o_ref[0] = _dot(ca, w_ref[0].astype(BF16)) + b_ref[0]


def adaln(c, ada_w, ada_b, tn=1024):
    L, D, N6 = ada_w.shape
    B = c.shape[0]
    return pl.pallas_call(
        _adaln_kernel,
        grid=(L, N6 // tn),
        in_specs=[pl.BlockSpec((B, D), lambda l, j: (0, 0)),
                  pl.BlockSpec((1, D, tn), lambda l, j: (l, 0, j)),
                  pl.BlockSpec((1, 1, tn), lambda l, j: (l, 0, j))],
        out_specs=pl.BlockSpec((1, B, tn), lambda l, j: (l, 0, j)),
        out_shape=jax.ShapeDtypeStruct((L, B, N6), F32),
        compiler_params=_params(("parallel", "parallel")),
        name="adaln",
    )(c, ada_w, ada_b.reshape(L, 1, N6))


def _norm_mod_rows(x_ref, g, scale1, shift, out_ref, out_row0, nrows, chunk):
    def body(ci, carry):
        r = pl.multiple_of(ci * chunk, chunk)
        x = x_ref[pl.ds(r, chunk), :]
        ms = jnp.mean(x * x, axis=-1, keepdims=True)
        y = x * lax.rsqrt(ms + EPS) * g
        out_ref[pl.ds(out_row0 + r, chunk), :] = (y * scale1 + shift).astype(BF16)
        return carry
    lax.fori_loop(0, nrows // chunk, body, 0)


def _inproj_kernel(x_ref, g_ref, sc_ref, sh_ref, scn_ref, shn_ref, w_ref, *rest, has_gate, rows_per, n_chunks):
    if has_gate:
        wg_ref, o_ref, og_ref, ha_ref, hb_ref = rest
    else:
        o_ref, ha_ref, hb_ref = rest
    i = pl.program_id(0)
    j = pl.program_id(1)
    tm = x_ref.shape[0]
    g = g_ref[...]

    def norm_rows(r0, nrows, dst_ref, s_ref, t_ref):
        x = x_ref[pl.ds(r0, nrows), :]
        ms = jnp.mean(x * x, axis=-1, keepdims=True)
        y = x * lax.rsqrt(ms + EPS) * g
        dst_ref[pl.ds(r0, nrows), :] = (y * (1.0 + s_ref[0]) + t_ref[0]).astype(BF16)

    @pl.when((i == 0) & (j == 0))
    def _():
        def body(ci, carry):
            norm_rows(pl.multiple_of(ci * rows_per, BF16_SUBLANES), rows_per, ha_ref, sc_ref, sh_ref)
            return carry
        lax.fori_loop(0, tm // rows_per, body, 0)
        if tm % rows_per:
            norm_rows(tm - rows_per, rows_per, ha_ref, sc_ref, sh_ref)

    c = jnp.clip(j - 1, 0, n_chunks - 1)
    r0 = pl.multiple_of(jnp.minimum(c * rows_per, tm - rows_per), BF16_SUBLANES)

    def step(cur_ref, nxt_ref):
        if has_gate:
            @pl.when(j == 0)
            def _():
                og_ref[...] = _dot(cur_ref[...], wg_ref[...])
        o_ref[...] = _dot(cur_ref[...], w_ref[...]).astype(o_ref.dtype)
        norm_rows(r0, rows_per, nxt_ref, scn_ref, shn_ref)

    @pl.when(i % 2 == 0)
    def _():
        step(ha_ref, hb_ref)

    @pl.when(i % 2 == 1)
    def _():
        step(hb_ref, ha_ref)


def in_proj(x, g, sc, sh, w, n_out, wg, T, tm=1024, tn=512):
    N, D = x.shape
    Nout = n_out
    tm = min(tm, T)
    nb, nj = N // tm, Nout // tn
    assert N % tm == 0 and T % tm == 0 and Nout % tn == 0 and nj >= 2
    n_chunks = nj - 1
    rows_per = -(-tm // n_chunks)
    rows_per = -(-rows_per // BF16_SUBLANES) * BF16_SUBLANES
    nxt = lambda i: jnp.minimum(i + 1, nb - 1)
    xmap = lambda i, j: (jnp.where((i == 0) & (j == 0), 0, nxt(i)), 0)
    bmap = lambda i, j: ((i * tm) // T, 0, 0)
    nmap = lambda i, j: ((nxt(i) * tm) // T, 0, 0)
    in_specs = [pl.BlockSpec((tm, D), xmap),
                pl.BlockSpec((1, D), lambda i, j: (0, 0)),
                pl.BlockSpec((1, 1, D), bmap), pl.BlockSpec((1, 1, D), bmap),
                pl.BlockSpec((1, 1, D), nmap), pl.BlockSpec((1, 1, D), nmap),
                pl.BlockSpec((D, tn), lambda i, j: (0, j))]
    out_specs = [pl.BlockSpec((tm, tn), lambda i, j: (i, j))]
    out_shape = [jax.ShapeDtypeStruct((N, Nout), BF16)]
    args = [x, g.reshape(1, D), sc, sh, sc, sh, w]
    if wg is not None:
        ng = wg.shape[1]
        in_specs.append(pl.BlockSpec((D, ng), lambda i, j: (0, 0)))
        out_specs.append(pl.BlockSpec((tm, ng), lambda i, j: (i, 0)))
        out_shape.append(jax.ShapeDtypeStruct((N, ng), F32))
        args.append(wg)
    return pl.pallas_call(
        functools.partial(_inproj_kernel, has_gate=wg is not None, rows_per=rows_per, n_chunks=n_chunks),
        grid=(nb, nj),
        in_specs=in_specs, out_specs=out_specs, out_shape=out_shape,
        scratch_shapes=[pltpu.VMEM((tm, D), BF16), pltpu.VMEM((tm, D), BF16)],
        compiler_params=_params(("arbitrary", "arbitrary")),
        name="in_proj",
    )(*args)


def _outproj_kernel(ya_ref, yb_ref, wa_ref, wb_ref, x_ref, gate_ref, o_ref):
    y = _dot(ya_ref[...], wa_ref[...]) + _dot(yb_ref[...], wb_ref[...])
    o_ref[...] = x_ref[...] + gate_ref[0] * y


def out_proj(ya, yb, w_out, x, gate, T, tm=512, tn=2048):
    N, D = x.shape
    Ka, Kb = ya.shape[1], yb.shape[1]
    tm = min(tm, T)
    assert N % tm == 0 and T % tm == 0 and D % tn == 0 and Ka % Kb == 0
    return pl.pallas_call(
        _outproj_kernel,
        grid=(N // tm, D // tn),
        in_specs=[pl.BlockSpec((tm, Ka), lambda i, j: (i, 0)),
                  pl.BlockSpec((tm, Kb), lambda i, j: (i, 0)),
                  pl.BlockSpec((Ka, tn), lambda i, j: (0, j)),
                  pl.BlockSpec((Kb, tn), lambda i, j: (Ka // Kb, j)),
                  pl.BlockSpec((tm, tn), lambda i, j: (i, j)),
                  pl.BlockSpec((1, 1, tn), lambda i, j: ((i * tm) // T, 0, j))],
        out_specs=pl.BlockSpec((tm, tn), lambda i, j: (i, j)),
        out_shape=jax.ShapeDtypeStruct((N, D), F32),
        compiler_params=_params(("parallel", "parallel")),
        name="out_proj",
    )(ya, yb, w_out, w_out, x, gate)


FFN_HALO = BF16_SUBLANES


def _to_tiles(a):
    rows = a.shape[0]
    return jnp.swapaxes(a.reshape(8, rows // 8, LANES), 0, 1).reshape(rows, LANES)


def _from_tiles(a):
    rows = a.shape[0]
    return jnp.swapaxes(a.reshape(rows // 8, 8, LANES), 0, 1).reshape(rows, LANES)


def _ffn_kernel_old(x_ref, xh_ref, g_ref, sc_ref, sh_ref, gate_ref, wa_ref, wb_ref, cwa_ref, cwb_ref,
                wd_ref, o_ref, h_ref, xp_ref, acc_ref, ya_ref, yb_ref, *, blocks_per_seq, chunk, sub):
    i = pl.program_id(0)
    j = pl.program_id(1)
    tm = x_ref.shape[0]

    K = tm // 8

    def norm_mod(xr, g, scale1, shift):
        ms = jnp.mean(xr * xr, axis=-1, keepdims=True)
        return (xr * lax.rsqrt(ms + EPS) * g) * scale1 + shift

    @pl.when(j == 0)
    def _():
        g = g_ref[...]
        scale1 = 1.0 + sc_ref[0]
        shift = sh_ref[0]
        for cb in range(x_ref.shape[1] // LANES):
            cols = slice(cb * LANES, (cb + 1) * LANES)
            xp_ref[:, cols] = _to_tiles(x_ref[:, cols])
        _norm_mod_rows(xp_ref, g, scale1, shift, h_ref, FFN_HALO, tm, chunk)
        first = (i % blocks_per_seq) == 0
        h_ref[0:FFN_HALO, :] = jnp.where(first, 0.0, norm_mod(xh_ref[...], g, scale1, shift)).astype(BF16)
        acc_ref[...] = jnp.zeros_like(acc_ref)

    h = h_ref[...]
    tf = wa_ref.shape[2]
    nsub = tf // sub
    sub0 = lax.broadcasted_iota(jnp.int32, (8, sub), 0) == 0

    def conv(y_ref, cw_ref, cs):
        H = FFN_HALO
        last = pltpu.roll(y_ref[H + tm - 8:H + tm, :], 1, axis=0)
        last2 = pltpu.roll(y_ref[H + tm - 16:H + tm - 8, :], 1, axis=0)
        m1 = jnp.where(sub0, y_ref[H - 1:H, :], last)
        m2 = jnp.where(sub0, y_ref[H - 2:H - 1, :], last2)
        y_ref[H - 8:H, :] = m1
        y_ref[H - 16:H - 8, :] = m2
        cw = cw_ref[0, :, cs]
        out = cw[FFN_CONV_WIDTH:FFN_CONV_WIDTH + 1, :]
        for k in range(FFN_CONV_WIDTH):
            off = H - 8 * (FFN_CONV_WIDTH - 1 - k)
            out = out + cw[k:k + 1, :] * y_ref[off:off + tm, :]
        return out

    for c in range(nsub):
        cs = slice(c * sub, (c + 1) * sub)
        ya_ref[c] = _dot(h, wa_ref[0, :, cs])
        yb_ref[c] = _dot(h, wb_ref[0, :, cs])
    for c in range(nsub):
        cs = slice(c * sub, (c + 1) * sub)
        act = (_silu(conv(ya_ref.at[c], cwa_ref, cs)) * conv(yb_ref.at[c], cwb_ref, cs)).astype(BF16)
        d = _dot(act, wd_ref[0, cs, :])
        for cb in range(acc_ref.shape[0]):
            acc_ref[cb] += d[:, cb * LANES:(cb + 1) * LANES]

    @pl.when(j == pl.num_programs(1) - 1)
    def _():
        for cb in range(acc_ref.shape[0]):
            cols = slice(cb * LANES, (cb + 1) * LANES)
            o_ref[:, cols] = x_ref[:, cols] + gate_ref[0, :, cols] * _from_tiles(acc_ref[cb])


def conv_ffn_old(x, g, sc, sh, gate, w_up, cw, w_down, layer, T, tm=512, tf=512, sub=256):
    N, D = x.shape
    DFF = w_down.shape[1]
    tm = min(tm, T)
    assert N % tm == 0 and T % tm == 0 and DFF % tf == 0 and tm % FFN_HALO == 0
    sub = min(sub, tf)
    nff = DFF // tf
    hb = tm // FFN_HALO
    bmap = lambda i, j: ((i * tm) // T, 0, 0)
    return pl.pallas_call(
        functools.partial(_ffn_kernel, blocks_per_seq=T // tm, chunk=min(128, tm), sub=min(sub, tf)),
        grid=(N // tm, nff),
        in_specs=[pl.BlockSpec((tm, D), lambda i, j: (i, 0)),
                  pl.BlockSpec((FFN_HALO, D), lambda i, j: (jnp.maximum(i * hb - 1, 0), 0)),
                  pl.BlockSpec((1, D), lambda i, j: (0, 0)),
                  pl.BlockSpec((1, 1, D), bmap),
                  pl.BlockSpec((1, 1, D), bmap),
                  pl.BlockSpec((1, 1, D), bmap),
                  pl.BlockSpec((1, D, tf), lambda i, j: (layer, 0, j)),
                  pl.BlockSpec((1, D, tf), lambda i, j: (layer, 0, j + nff)),
                  pl.BlockSpec((1, 8, tf), lambda i, j: (layer, 0, j)),
                  pl.BlockSpec((1, 8, tf), lambda i, j: (layer, 0, j + nff)),
                  pl.BlockSpec((1, tf, D), lambda i, j: (layer, j, 0))],
        out_specs=pl.BlockSpec((tm, D), lambda i, j: (i, 0)),
        out_shape=jax.ShapeDtypeStruct((N, D), F32),
        scratch_shapes=[pltpu.VMEM((tm + FFN_HALO, D), BF16), pltpu.VMEM((tm, D), F32),
                        pltpu.VMEM((D // LANES, tm, LANES), F32),
                        pltpu.VMEM((tf // sub, tm + FFN_HALO, sub), F32),
                        pltpu.VMEM((tf // sub, tm + FFN_HALO, sub), F32)],
        compiler_params=_params(("parallel", "arbitrary")),
        name="conv_ffn",
    )(x, x, g.reshape(1, D), sc, sh, gate, w_up, w_up, cw, cw, w_down)


FFN_GROUP = 64


def _ffn_kernel(x_ref, xn_ref, xhn_ref, g_ref, sc_ref, sh_ref, scn_ref, shn_ref, gate_ref,
                wa_ref, wb_ref, cwa_ref, cwb_ref, wd_ref, o_ref,
                ha_ref, hb_ref, acc_ref, ya_ref, yb_ref, *, blocks_per_seq, sub):
    i = pl.program_id(0)
    j = pl.program_id(1)
    tm = x_ref.shape[0]
    D = x_ref.shape[1]
    K = tm // 8
    n_groups = tm // FFN_GROUP
    g = g_ref[...]

    def norm_mod(xr, s_ref, t_ref):
        ms = jnp.mean(xr * xr, axis=-1, keepdims=True)
        return (xr * lax.rsqrt(ms + EPS) * g) * (1.0 + s_ref[0]) + t_ref[0]

    def tile_group(src_ref, c, dst_ref, s_ref, t_ref):
        rows = [src_ref[pl.ds(pl.multiple_of(K * s + 8 * c, 8), 8), :] for s in range(8)]
        hm = norm_mod(jnp.concatenate(rows, axis=0), s_ref, t_ref)
        tiled = jnp.concatenate([_to_tiles(hm[:, cb * LANES:(cb + 1) * LANES]) for cb in range(D // LANES)], axis=1)
        dst_ref[pl.ds(pl.multiple_of(FFN_HALO + FFN_GROUP * c, BF16_SUBLANES), FFN_GROUP), :] = tiled.astype(BF16)
        tot = tiled[0:8, :]
        for r0 in range(8, FFN_GROUP, 8):
            tot = tot + tiled[r0:r0 + 8, :]
        red = tot[:, 0:LANES]
        for cb in range(1, D // LANES):
            red = red + tot[:, cb * LANES:(cb + 1) * LANES]
        bits = pltpu.bitcast(red, jnp.uint32)
        zero = lax.shift_right_logical(lax.shift_right_logical(bits, jnp.uint32(16)), jnp.uint32(16))
        return zero[0:1, :].astype(F32)

    @pl.when((i == 0) & (j == 0))
    def _():
        def body(c, carry):
            tile_group(x_ref, c, ha_ref, sc_ref, sh_ref)
            return carry
        lax.fori_loop(0, n_groups, body, 0)
        ha_ref[0:FFN_HALO, :] = jnp.zeros((FFN_HALO, D), BF16)

    @pl.when(j == 0)
    def _():
        acc_ref[...] = jnp.zeros_like(acc_ref)

    tf = wa_ref.shape[2]
    nsub = tf // sub
    sub0 = lax.broadcasted_iota(jnp.int32, (8, sub), 0) == 0
    grp = jnp.clip(j - 1, 0, n_groups - 1)
    next_first = ((i + 1) % blocks_per_seq) == 0

    def conv(y_ref, cw_ref, cs, dep=None):
        H = FFN_HALO
        last = pltpu.roll(y_ref[H + tm - 8:H + tm, :], 1, axis=0)
        last2 = pltpu.roll(y_ref[H + tm - 16:H + tm - 8, :], 1, axis=0)
        m1 = jnp.where(sub0, y_ref[H - 1:H, :], last)
        m2 = jnp.where(sub0, y_ref[H - 2:H - 1, :], last2)
        y_ref[H - 8:H, :] = m1
        y_ref[H - 16:H - 8, :] = m2
        cw = cw_ref[0, :, cs]
        out = cw[FFN_CONV_WIDTH:FFN_CONV_WIDTH + 1, :]
        if dep is not None:
            out = out + dep
        for k in range(FFN_CONV_WIDTH):
            off = H - 8 * (FFN_CONV_WIDTH - 1 - k)
            out = out + cw[k:k + 1, :] * y_ref[off:off + tm, :]
        return out

    def step(cur_ref, nxt_ref):
        zero = tile_group(xn_ref, grp, nxt_ref, scn_ref, shn_ref)
        dep = jnp.concatenate([zero] * (sub // LANES), axis=1)
        nxt_ref[0:FFN_HALO, :] = jnp.where(next_first, 0.0, norm_mod(xhn_ref[...], scn_ref, shn_ref)).astype(BF16)
        h = cur_ref[...]
        for c in range(nsub):
            cs = slice(c * sub, (c + 1) * sub)
            ya_ref[c] = _dot(h, wa_ref[0, :, cs])
            yb_ref[c] = _dot(h, wb_ref[0, :, cs])
        for c in range(nsub):
            cs = slice(c * sub, (c + 1) * sub)
            gate_in = conv(yb_ref.at[c], cwb_ref, cs, dep if c == 0 else None)
            act = (_silu(conv(ya_ref.at[c], cwa_ref, cs)) * gate_in).astype(BF16)
            d = _dot(act, wd_ref[0, cs, :])
            for cb in range(acc_ref.shape[0]):
                acc_ref[cb] += d[:, cb * LANES:(cb + 1) * LANES]

    @pl.when(i % 2 == 0)
    def _():
        step(ha_ref, hb_ref)

    @pl.when(i % 2 == 1)
    def _():
        step(hb_ref, ha_ref)

    @pl.when(j == pl.num_programs(1) - 1)
    def _():
        for cb in range(acc_ref.shape[0]):
            cols = slice(cb * LANES, (cb + 1) * LANES)
            o_ref[:, cols] = x_ref[:, cols] + gate_ref[0, :, cols] * _from_tiles(acc_ref[cb])


def conv_ffn(x, g, sc, sh, gate, w_up, cw, w_down, layer, T, tm=512, tf=512, sub=256):
    N, D = x.shape
    DFF = w_down.shape[1]
    tm = min(tm, T)
    sub = min(sub, tf)
    nb, nff = N // tm, DFF // tf
    assert N % tm == 0 and T % tm == 0 and DFF % tf == 0 and tm % FFN_GROUP == 0 and nff - 1 >= tm // FFN_GROUP
    hb = tm // FFN_HALO
    nxt = lambda i: jnp.minimum(i + 1, nb - 1)
    bmap = lambda i, j: ((i * tm) // T, 0, 0)
    nmap = lambda i, j: ((nxt(i) * tm) // T, 0, 0)
    mod = pl.BlockSpec((1, 1, D), bmap)
    modn = pl.BlockSpec((1, 1, D), nmap)
    return pl.pallas_call(
        functools.partial(_ffn_kernel, blocks_per_seq=T // tm, sub=sub),
        grid=(nb, nff),
        in_specs=[pl.BlockSpec((tm, D), lambda i, j: (i, 0)),
                  pl.BlockSpec((tm, D), lambda i, j: (nxt(i), 0)),
                  pl.BlockSpec((FFN_HALO, D), lambda i, j: (jnp.maximum(nxt(i) * hb - 1, 0), 0)),
                  pl.BlockSpec((1, D), lambda i, j: (0, 0)),
                  mod, mod, modn, modn, mod,
                  pl.BlockSpec((1, D, tf), lambda i, j: (layer, 0, j)),
                  pl.BlockSpec((1, D, tf), lambda i, j: (layer, 0, j + nff)),
                  pl.BlockSpec((1, 8, tf), lambda i, j: (layer, 0, j)),
                  pl.BlockSpec((1, 8, tf), lambda i, j: (layer, 0, j + nff)),
                  pl.BlockSpec((1, tf, D), lambda i, j: (layer, j, 0))],
        out_specs=pl.BlockSpec((tm, D), lambda i, j: (i, 0)),
        out_shape=jax.ShapeDtypeStruct((N, D), F32),
        scratch_shapes=[pltpu.VMEM((tm + FFN_HALO, D), BF16), pltpu.VMEM((tm + FFN_HALO, D), BF16),
                        pltpu.VMEM((D // LANES, tm, LANES), F32),
                        pltpu.VMEM((tf // sub, tm + FFN_HALO, sub), F32),
                        pltpu.VMEM((tf // sub, tm + FFN_HALO, sub), F32)],
        compiler_params=_params(("arbitrary", "arbitrary")),
        name="conv_ffn",
    )(x, x, x, g.reshape(1, D), sc, sh, sc, sh, gate, w_up, w_up, cw, cw, w_down)


KV_PLAIN = -1
KV_TRANSPOSED = -2
KV_TILE = 256
SEL_UNROLL = 2


def _rope(y, cos, sin):
    return y * cos + pltpu.roll(y, HEAD_DIM // 2, axis=1) * sin


def _head_norm(x, g):
    ms = jnp.mean(x * x, axis=-1, keepdims=True)
    return x * lax.rsqrt(ms + EPS) * g


def _prep_kernel(*refs, n_q_heads, kv_kinds, n_groups):
    nkv = len(kv_kinds)
    zq_ref = refs[0]
    kv_refs = refs[1:1 + nkv]
    cos_ref, sin_ref, qn_ref, kn_ref = refs[1 + nkv:5 + nkv]
    q_out = refs[5 + nkv]
    kv_out = refs[6 + nkv:]
    cos = cos_ref[0]
    sin = sin_ref[0]
    scale = HEAD_DIM ** -0.5 * LOG2E
    for hd in range(n_q_heads):
        sl = slice(hd * HEAD_DIM, (hd + 1) * HEAD_DIM)
        y = _rope(_head_norm(zq_ref[0, :, sl].astype(F32), qn_ref[...]), cos, sin)
        q_out[0, :, sl] = (y * scale).astype(BF16)
    for idx, kind in enumerate(kv_kinds):
        for gi in range(n_groups):
            sl = slice(gi * HEAD_DIM, (gi + 1) * HEAD_DIM)
            a = kv_refs[idx][0, :, sl]
            if kind >= 0:
                a = _rope(_head_norm(a.astype(F32), kn_ref[kind:kind + 1, :]), cos, sin)
            if kind == KV_TRANSPOSED:
                for c in range(a.shape[0] // KV_TILE):
                    kv_out[idx][0, gi, c] = a[c * KV_TILE:(c + 1) * KV_TILE, :].astype(F32).T.astype(BF16)
            else:
                kv_out[idx][0, gi] = a.astype(BF16)


def attn_prep(z3, cosf, sinf, q_norm, k_norm, q_col, kv_cols, kv_kinds, n_groups, tm=512):
    B, T, _ = z3.shape
    tm = min(tm, T)
    QW = 8 * HEAD_DIM
    KW = n_groups * HEAD_DIM
    assert q_col % QW == 0 and all(c % KW == 0 for c in kv_cols) and T % tm == 0
    nkv = len(kv_cols)
    in_specs = [pl.BlockSpec((1, tm, QW), lambda b, t: (b, t, q_col // QW))]
    for c in kv_cols:
        in_specs.append(pl.BlockSpec((1, tm, KW), functools.partial(lambda b, t, cb: (b, t, cb), cb=c // KW)))
    in_specs += [pl.BlockSpec((1, tm, HEAD_DIM), lambda b, t: (b, t, 0))] * 2
    in_specs += [pl.BlockSpec((1, HEAD_DIM), lambda b, t: (0, 0)),
                 pl.BlockSpec(k_norm.shape, lambda b, t: (0, 0))]
    out_specs = [pl.BlockSpec((1, tm, QW), lambda b, t: (b, t, 0))]
    out_shape = [jax.ShapeDtypeStruct((B, T, QW), BF16)]
    for kind in kv_kinds:
        if kind == KV_TRANSPOSED:
            assert tm % KV_TILE == 0
            out_specs.append(pl.BlockSpec((1, n_groups, tm // KV_TILE, HEAD_DIM, KV_TILE),
                                          lambda b, t: (b, 0, t, 0, 0)))
            out_shape.append(jax.ShapeDtypeStruct((B, n_groups, T // KV_TILE, HEAD_DIM, KV_TILE), BF16))
        else:
            out_specs.append(pl.BlockSpec((1, n_groups, tm, HEAD_DIM), lambda b, t: (b, 0, t, 0)))
            out_shape.append(jax.ShapeDtypeStruct((B, n_groups, T, HEAD_DIM), BF16))
    return pl.pallas_call(
        functools.partial(_prep_kernel, n_q_heads=QW // HEAD_DIM, kv_kinds=tuple(kv_kinds), n_groups=n_groups),
        grid=(B, T // tm),
        in_specs=in_specs, out_specs=out_specs, out_shape=out_shape,
        compiler_params=_params(("parallel", "parallel")),
        name="attn_prep",
    )(*([z3] * (1 + nkv)), cosf, sinf, q_norm.reshape(1, HEAD_DIM), k_norm)


def _compress_kernel(ak_ref, av_ref, w1k_ref, w1v_ref, pk_ref, pv_ref, w2k_ref, w2v_ref,
                     kn_ref, cos_ref, sin_ref, kc_ref, vc_ref):
    hid = NSA_CMP_HIDDEN

    def mlp(a_ref, w1_ref, p_ref, w2_ref):
        P = _dot(a_ref[0, 0], w1_ref[...])
        Q = _dot(p_ref[...], w1_ref[...])
        pb = Q[0:1, :hid] + Q[1:2, hid:]
        nxt = pltpu.roll(P[:, hid:], P.shape[0] - 1, axis=0)
        hdn = _silu(P[:, :hid] + nxt + pb)
        return _dot(hdn.astype(BF16), w2_ref[...])

    kc = mlp(ak_ref, w1k_ref, pk_ref, w2k_ref)
    kc = _rope(_head_norm(kc, kn_ref[...]), cos_ref[0], sin_ref[0])
    kc_ref[0, 0] = kc.astype(BF16)
    vc_ref[0, 0] = mlp(av_ref, w1v_ref, pv_ref, w2v_ref).T.astype(BF16)


def nsa_compress(kcr, vcr, w1k, w1v, pk, pv, w2k, w2v, kn0, cos_end, sin_end):
    B, G, T, dh = kcr.shape
    nch = T // NSA_CMP_STRIDE
    cw = NSA_CMP_STRIDE * dh
    a_spec = pl.BlockSpec((1, 1, nch, cw), lambda b, g: (b, g, 0, 0))
    full = lambda arr: pl.BlockSpec(arr.shape, lambda b, g: (0,) * arr.ndim)
    tab_spec = pl.BlockSpec((1, nch, dh), lambda b, g: (b, 0, 0))
    o_spec = pl.BlockSpec((1, 1, nch, dh), lambda b, g: (b, g, 0, 0))
    return pl.pallas_call(
        _compress_kernel,
        grid=(B, G),
        in_specs=[a_spec, a_spec, full(w1k), full(w1v), full(pk), full(pv), full(w2k), full(w2v),
                  full(kn0), tab_spec, tab_spec],
        out_specs=[o_spec, pl.BlockSpec((1, 1, dh, nch), lambda b, g: (b, g, 0, 0))],
        out_shape=[jax.ShapeDtypeStruct((B, G, nch, dh), BF16), jax.ShapeDtypeStruct((B, G, dh, nch), BF16)],
        compiler_params=_params(("parallel", "parallel")),
        name="nsa_compress",
    )(kcr.reshape(B, G, nch, cw), vcr.reshape(B, G, nch, cw), w1k, w1v, pk, pv, w2k, w2v,
      kn0, cos_end, sin_end)


def _nsa_kernel(q_ref, kc_ref, vct_ref, ks_ref, vst_ref, kw_ref, vwt_ref, gl_ref, ov_ref, ext_ref,
                o_ref, m_ref, l_ref, acc_ref, *, n_rep, n_cmp, n_sel, n_top):
    tq = q_ref.shape[1]
    tk = KV_TILE
    nrow = kc_ref.shape[2]
    t0 = pl.program_id(2) * tq
    q4 = jnp.concatenate([q_ref[0, :, r * HEAD_DIM:(r + 1) * HEAD_DIM] for r in range(n_rep)], axis=0)

    def heads(a):
        return jnp.concatenate([a] * n_rep, axis=1)

    jb = lax.broadcasted_iota(jnp.int32, (nrow, tq), 0)
    tcol = t0 + lax.broadcasted_iota(jnp.int32, (nrow, tq), 1)
    valid_c = (jb * NSA_CMP_STRIDE + (NSA_CMP_LEN - 1) <= tcol) & (jb < n_cmp)
    validf = heads(jnp.where(valid_c, 1.0, 0.0))
    s = _dot_nt(kc_ref[0, 0], q4) + heads(jnp.where(valid_c, 0.0, NEG))
    e = jnp.exp2(s - jnp.max(s, axis=0, keepdims=True))
    p = e / jnp.sum(e, axis=0, keepdims=True) * validf
    o_cmp = _dot(vct_ref[0, 0], p.astype(BF16))
    psum = p[:, 0:tq]
    for r in range(1, n_rep):
        psum = psum + p[:, r * tq:(r + 1) * tq]

    hi = psum.astype(BF16)
    lo = (psum - hi.astype(F32)).astype(BF16)
    imp = _dot(ov_ref[...], hi) + _dot(ov_ref[...], lo)
    nsr = ov_ref.shape[0]
    jb = lax.broadcasted_iota(jnp.int32, (nsr, tq), 0)
    tcol = t0 + lax.broadcasted_iota(jnp.int32, (nsr, tq), 1)
    cur = lax.shift_right_logical(tcol, int(np.log2(NSA_SEL_LEN)))
    forced = (jb == 0) | (jb == cur) | (jb == cur - 1)
    score = jnp.where(forced, NSA_FORCE, jnp.where(jb * NSA_SEL_LEN <= tcol, imp, -1.0))
    score = jnp.where(jb < n_sel, score, -2.0)
    selT = jnp.zeros((nsr, tq), F32)
    for _ in range(n_top):
        best = jnp.max(score, axis=0, keepdims=True)
        idx = jnp.min(jnp.where(score == best, jb, nsr), axis=0, keepdims=True)
        hit = jb == idx
        selT = jnp.where(hit, 1.0, selT)
        score = jnp.where(hit, -3.0, score)
    sel = selT.astype(BF16)

    m_ref[...] = jnp.full(m_ref.shape, NEG, F32)
    l_ref[...] = jnp.zeros(l_ref.shape, F32)
    acc_ref[...] = jnp.zeros(acc_ref.shape, F32)
    krow = lax.broadcasted_iota(jnp.int32, (tk, tq), 0)
    tq_col = t0 + lax.broadcasted_iota(jnp.int32, (tk, tq), 1)

    def sel_body(it, carry):
        s = []
        for u in range(SEL_UNROLL):
            kt = it * SEL_UNROLL + u
            k0 = pl.multiple_of(kt * tk, tk)
            mask = (_dot(ext_ref[kt], sel) > 0.5) & (k0 + krow <= tq_col)
            s.append(_dot_nt(ks_ref[0, 0, pl.ds(k0, tk), :], q4) + heads(jnp.where(mask, 0.0, NEG)))
        m_old = m_ref[...]
        m_new = m_old
        for u in range(SEL_UNROLL):
            m_new = jnp.maximum(m_new, jnp.max(s[u], axis=0, keepdims=True))
        alpha = jnp.exp2(m_old - m_new)
        lsum = alpha * l_ref[...]
        pv = alpha * acc_ref[...]
        for u in range(SEL_UNROLL):
            p = jnp.exp2(s[u] - m_new)
            lsum = lsum + jnp.sum(p, axis=0, keepdims=True)
            pv = pv + _dot(vst_ref[0, 0, it * SEL_UNROLL + u], p.astype(BF16))
        l_ref[...] = lsum
        acc_ref[...] = pv
        m_ref[...] = m_new
        return carry

    n_tiles = (t0 + tq) // tk
    lax.fori_loop(0, (n_tiles + SEL_UNROLL - 1) // SEL_UNROLL, sel_body, 0)
    o_sel = acc_ref[...] / l_ref[...]

    W = NSA_WINDOW
    n_win = (W + tq) // tk
    kstart = pl.multiple_of(jnp.maximum(t0 - W, 0), tk)
    jt = kstart // tk
    sw = []
    for c in range(n_win):
        rel = tq_col - (kstart + c * tk) - krow
        bias = heads(jnp.where((rel >= 0) & (rel < W), 0.0, NEG))
        sw.append(_dot_nt(kw_ref[0, 0, pl.ds(kstart + c * tk, tk), :], q4) + bias)
    m = jnp.max(sw[0], axis=0, keepdims=True)
    for c in range(1, n_win):
        m = jnp.maximum(m, jnp.max(sw[c], axis=0, keepdims=True))
    den = jnp.zeros_like(m)
    o_win = jnp.zeros((HEAD_DIM, n_rep * tq), F32)
    for c in range(n_win):
        ew = jnp.exp2(sw[c] - m)
        den = den + jnp.sum(ew, axis=0, keepdims=True)
        o_win = o_win + _dot(vwt_ref[0, 0, jt + c], ew.astype(BF16))
    o_win = o_win / den

    gT = _sigmoid(gl_ref[0]).T
    for r in range(n_rep):
        hs = slice(r * tq, (r + 1) * tq)
        c3 = 3 * r
        oT = (gT[c3:c3 + 1, :] * o_cmp[:, hs] + gT[c3 + 1:c3 + 2, :] * o_sel[:, hs]
              + gT[c3 + 2:c3 + 3, :] * o_win[:, hs])
        o_ref[0, :, r * HEAD_DIM:(r + 1) * HEAD_DIM] = oT.T.astype(BF16)


def _nsa_kernel_v2(q_ref, kc_ref, vct_ref, ks_ref, vst_ref, kw_ref, vwt_ref, gl_ref, ov_ref, ext_ref,
                o_ref, m_ref, l_ref, acc_ref, oc_ref, *, n_rep, n_cmp, n_sel, n_top):
    tq = q_ref.shape[1]
    tk = KV_TILE
    nrow = kc_ref.shape[2]
    t0 = pl.program_id(2) * tq

    def q_head(r):
        return q_ref[0, :, r * HEAD_DIM:(r + 1) * HEAD_DIM]

    kc = kc_ref[0, 0]
    vct = vct_ref[0, 0]
    jb = lax.broadcasted_iota(jnp.int32, (nrow, tq), 0)
    tcol = t0 + lax.broadcasted_iota(jnp.int32, (nrow, tq), 1)
    valid_c = (jb * NSA_CMP_STRIDE + (NSA_CMP_LEN - 1) <= tcol) & (jb < n_cmp)
    psum = jnp.zeros((nrow, tq), F32)
    for r in range(n_rep):
        sT = jnp.where(valid_c, _dot_nt(kc, q_head(r)), NEG)
        e = jnp.exp(sT - jnp.max(sT, axis=0, keepdims=True))
        p = jnp.where(valid_c, e / jnp.sum(e, axis=0, keepdims=True), 0.0)
        psum = psum + p
        oc_ref[r] = _dot(vct, p.astype(BF16))

    hi = psum.astype(BF16)
    lo = (psum - hi.astype(F32)).astype(BF16)
    imp = _dot(ov_ref[...], hi) + _dot(ov_ref[...], lo)
    cur = lax.shift_right_logical(tcol, int(np.log2(NSA_SEL_LEN)))
    forced = (jb == 0) | (jb == cur) | (jb == cur - 1)
    score = jnp.where(forced, NSA_FORCE, jnp.where(jb * NSA_SEL_LEN <= tcol, imp, -1.0))
    score = jnp.where(jb < n_sel, score, -2.0)
    selT = jnp.zeros((nrow, tq), F32)
    for _ in range(n_top):
        best = jnp.max(score, axis=0, keepdims=True)
        idx = jnp.min(jnp.where(score == best, jb, nrow), axis=0, keepdims=True)
        hit = jb == idx
        selT = jnp.where(hit, 1.0, selT)
        score = jnp.where(hit, -3.0, score)
    sel = selT.astype(BF16)

    m_ref[...] = jnp.full(m_ref.shape, NEG, F32)
    l_ref[...] = jnp.zeros(l_ref.shape, F32)
    acc_ref[...] = jnp.zeros(acc_ref.shape, F32)
    krow = lax.broadcasted_iota(jnp.int32, (tk, tq), 0)
    tq_col = t0 + lax.broadcasted_iota(jnp.int32, (tk, tq), 1)

    def sel_body(kt, carry):
        k0 = pl.multiple_of(kt * tk, tk)
        k = ks_ref[0, 0, pl.ds(k0, tk), :]
        vt = vst_ref[0, 0, kt]
        mask = (_dot(ext_ref[kt], sel) > 0.5) & (k0 + krow <= tq_col)
        for r in range(n_rep):
            s = jnp.where(mask, _dot_nt(k, q_head(r)), NEG)
            m_old = m_ref[r]
            m_new = jnp.maximum(m_old, jnp.max(s, axis=0, keepdims=True))
            alpha = jnp.exp(m_old - m_new)
            p = jnp.exp(s - m_new)
            l_ref[r] = alpha * l_ref[r] + jnp.sum(p, axis=0, keepdims=True)
            acc_ref[r] = alpha * acc_ref[r] + _dot(vt, p.astype(BF16))
            m_ref[r] = m_new
        return carry

    lax.fori_loop(0, (t0 + tq) // tk, sel_body, 0)

    W = NSA_WINDOW
    kstart = pl.multiple_of(jnp.maximum(t0 - W, 0), tk)
    jt = kstart // tk
    wk = [kw_ref[0, 0, pl.ds(kstart + c * tk, tk), :] for c in range(2)]
    wvt = [vwt_ref[0, 0, jt + c] for c in range(2)]
    wmask = []
    for c in range(2):
        rel = tq_col - (kstart + c * tk) - krow
        wmask.append((rel >= 0) & (rel < W))
    gT = _sigmoid(gl_ref[0]).T
    for r in range(n_rep):
        s = [jnp.where(wmask[c], _dot_nt(wk[c], q_head(r)), NEG) for c in range(2)]
        m = jnp.maximum(jnp.max(s[0], axis=0, keepdims=True), jnp.max(s[1], axis=0, keepdims=True))
        e = [jnp.exp(s[c] - m) for c in range(2)]
        den = jnp.sum(e[0], axis=0, keepdims=True) + jnp.sum(e[1], axis=0, keepdims=True)
        o_win = (_dot(wvt[0], e[0].astype(BF16)) + _dot(wvt[1], e[1].astype(BF16))) / den
        o_sel = acc_ref[r] / l_ref[r]
        c3 = 3 * r
        oT = gT[c3:c3 + 1, :] * oc_ref[r] + gT[c3 + 1:c3 + 2, :] * o_sel + gT[c3 + 2:c3 + 3, :] * o_win
        o_ref[0, :, r * HEAD_DIM:(r + 1) * HEAD_DIM] = oT.T.astype(BF16)


def nsa_attention(qn, kc, vct, ks, vst, kw, vwt, gl3, tq=512):
    B, T, QW = qn.shape
    G = kc.shape[1]
    n_rep = QW // HEAD_DIM // G
    nrow = kc.shape[2]
    tk = KV_TILE
    n_cmp = (T - NSA_CMP_LEN) // NSA_CMP_STRIDE + 1
    n_sel = T // NSA_SEL_LEN
    assert T % tq == 0 and n_sel <= nrow and T >= NSA_WINDOW + tq and tq % tk == 0 and NSA_WINDOW % tk == 0
    assert (tq // tk) % SEL_UNROLL == 0
    starts = np.arange(nrow) * NSA_CMP_STRIDE
    sel_start = np.arange(nrow) * NSA_SEL_LEN
    nsr = -(-n_sel // BF16_SUBLANES) * BF16_SUBLANES
    ov = ((starts[None, :] <= sel_start[:, None] + NSA_SEL_LEN - 1)
          & (starts[None, :] + NSA_CMP_LEN - 1 >= sel_start[:, None])
          & (np.arange(nrow)[:, None] < n_sel) & (np.arange(nrow)[None, :] < n_cmp))[:nsr]
    ext = (np.arange(T)[:, None] // NSA_SEL_LEN == np.arange(nsr)[None, :]).reshape(T // tk, tk, nsr)
    k_spec = pl.BlockSpec((1, 1, T, HEAD_DIM), lambda b, g, i: (b, g, 0, 0))
    vt_spec = pl.BlockSpec((1, 1, T // tk, HEAD_DIM, tk), lambda b, g, i: (b, g, 0, 0, 0))
    c_spec = pl.BlockSpec((1, 1, nrow, HEAD_DIM), lambda b, g, i: (b, g, 0, 0))
    hw = n_rep * HEAD_DIM
    return pl.pallas_call(
        functools.partial(_nsa_kernel, n_rep=n_rep, n_cmp=n_cmp, n_sel=n_sel, n_top=min(NSA_TOPK, n_sel)),
        grid=(B, G, T // tq),
        in_specs=[pl.BlockSpec((1, tq, hw), lambda b, g, i: (b, i, g)),
                  c_spec, pl.BlockSpec((1, 1, HEAD_DIM, nrow), lambda b, g, i: (b, g, 0, 0)),
                  k_spec, vt_spec, k_spec, vt_spec,
                  pl.BlockSpec((1, tq, LANES), lambda b, g, i: (b, i, g)),
                  pl.BlockSpec((nsr, nrow), lambda b, g, i: (0, 0)),
                  pl.BlockSpec((T // tk, tk, nsr), lambda b, g, i: (0, 0, 0))],
        out_specs=pl.BlockSpec((1, tq, hw), lambda b, g, i: (b, i, g)),
        out_shape=jax.ShapeDtypeStruct((B, T, QW), BF16),
        scratch_shapes=[pltpu.VMEM((1, n_rep * tq), F32), pltpu.VMEM((1, n_rep * tq), F32),
                        pltpu.VMEM((HEAD_DIM, n_rep * tq), F32)],
        compiler_params=_params(("parallel", "parallel", "arbitrary")),
        name="nsa_attention",
    )(qn, kc, vct, ks, vst, kw, vwt, gl3, jnp.asarray(ov, BF16), jnp.asarray(ext, BF16))


def _nsa_kernel_old(q_ref, kc_ref, vc_ref, ks_ref, vs_ref, kw_ref, vw_ref, gl_ref, ov_ref, ex_ref,
                o_ref, m_ref, l_ref, acc_ref, oc_ref, *, n_rep, n_cmp, n_sel, n_top, tk):
    tq = q_ref.shape[1]
    nrow = kc_ref.shape[2]
    t0 = pl.program_id(2) * tq
    gates = _sigmoid(gl_ref[0])

    def q_head(r):
        return q_ref[0, :, r * HEAD_DIM:(r + 1) * HEAD_DIM]

    kc = kc_ref[0, 0]
    vc = vc_ref[0, 0]
    jb = lax.broadcasted_iota(jnp.int32, (nrow, tq), 0)
    tcol = t0 + lax.broadcasted_iota(jnp.int32, (nrow, tq), 1)
    valid_c = (jb * NSA_CMP_STRIDE + (NSA_CMP_LEN - 1) <= tcol) & (jb < n_cmp)
    psum = jnp.zeros((nrow, tq), F32)
    for r in range(n_rep):
        sT = jnp.where(valid_c, _dot_nt(kc, q_head(r)), NEG)
        e = jnp.exp(sT - jnp.max(sT, axis=0, keepdims=True))
        p = jnp.where(valid_c, e / jnp.sum(e, axis=0, keepdims=True), 0.0)
        psum = psum + p
        oc_ref[r] = _dot(p.T.astype(BF16), vc)

    hi = psum.astype(BF16)
    lo = (psum - hi.astype(F32)).astype(BF16)
    imp = _dot(ov_ref[...], hi) + _dot(ov_ref[...], lo)
    cur = lax.shift_right_logical(tcol, int(np.log2(NSA_SEL_LEN)))
    forced = (jb == 0) | (jb == cur) | (jb == cur - 1)
    score = jnp.where(forced, NSA_FORCE, jnp.where(jb * NSA_SEL_LEN <= tcol, imp, -1.0))
    score = jnp.where(jb < n_sel, score, -2.0)
    selT = jnp.zeros((nrow, tq), F32)
    for _ in range(n_top):
        best = jnp.max(score, axis=0, keepdims=True)
        idx = jnp.min(jnp.where(score == best, jb, nrow), axis=0, keepdims=True)
        hit = jb == idx
        selT = jnp.where(hit, 1.0, selT)
        score = jnp.where(hit, -3.0, score)
    sel = selT.T.astype(BF16)

    m_ref[...] = jnp.full(m_ref.shape, NEG, F32)
    l_ref[...] = jnp.zeros(l_ref.shape, F32)
    acc_ref[...] = jnp.zeros(acc_ref.shape, F32)
    trow = t0 + lax.broadcasted_iota(jnp.int32, (tq, tk), 0)
    kcol = lax.broadcasted_iota(jnp.int32, (tq, tk), 1)

    def sel_body(kt, carry):
        k0 = pl.multiple_of(kt * tk, tk)
        k = ks_ref[0, 0, pl.ds(k0, tk), :]
        v = vs_ref[0, 0, pl.ds(k0, tk), :]
        mask = (_dot(sel, ex_ref[kt]) > 0.5) & (k0 + kcol <= trow)
        for r in range(n_rep):
            s = jnp.where(mask, _dot_nt(q_head(r), k), NEG)
            m_old = m_ref[r][:, 0:1]
            m_new = jnp.maximum(m_old, jnp.max(s, axis=-1, keepdims=True))
            alpha = jnp.exp(m_old - m_new)
            p = jnp.exp(s - m_new)
            l_ref[r] = alpha * l_ref[r] + jnp.sum(p, axis=-1, keepdims=True)
            acc_ref[r] = alpha * acc_ref[r] + _dot(p.astype(BF16), v)
            m_ref[r] = jnp.broadcast_to(m_new, (tq, LANES))
        return carry

    lax.fori_loop(0, (t0 + tq + tk - 1) // tk, sel_body, 0)

    W = NSA_WINDOW
    kstart = pl.multiple_of(jnp.maximum(t0 - W, 0), tq)
    kw = kw_ref[0, 0, pl.ds(kstart, W + tq), :]
    vw = vw_ref[0, 0, pl.ds(kstart, W + tq), :]
    rel = (t0 + lax.broadcasted_iota(jnp.int32, (tq, W + tq), 0)
           - kstart - lax.broadcasted_iota(jnp.int32, (tq, W + tq), 1))
    wmask = (rel >= 0) & (rel < W)
    for r in range(n_rep):
        s = jnp.where(wmask, _dot_nt(q_head(r), kw), NEG)
        e = jnp.exp(s - jnp.max(s, axis=-1, keepdims=True))
        o_win = _dot(e.astype(BF16), vw) / jnp.sum(e, axis=-1, keepdims=True)
        o_sel = acc_ref[r] / l_ref[r][:, 0:1]
        c = 3 * r
        o = gates[:, c:c + 1] * oc_ref[r] + gates[:, c + 1:c + 2] * o_sel + gates[:, c + 2:c + 3] * o_win
        o_ref[0, :, r * HEAD_DIM:(r + 1) * HEAD_DIM] = o.astype(BF16)


def nsa_attention_old(qn, kc, vc, ks, vs, kw, vw, gl3, tq=256, tk=512):
    B, T, QW = qn.shape
    G = kc.shape[1]
    n_rep = QW // HEAD_DIM // G
    nrow = kc.shape[2]
    n_cmp = (T - NSA_CMP_LEN) // NSA_CMP_STRIDE + 1
    n_sel = T // NSA_SEL_LEN
    tk = min(tk, T)
    assert T % tq == 0 and T % tk == 0 and n_sel <= nrow and T >= NSA_WINDOW + tq and tq == NSA_WINDOW
    starts = np.arange(nrow) * NSA_CMP_STRIDE
    sel_start = np.arange(nrow) * NSA_SEL_LEN
    ov = ((starts[None, :] <= sel_start[:, None] + NSA_SEL_LEN - 1)
          & (starts[None, :] + NSA_CMP_LEN - 1 >= sel_start[:, None])
          & (np.arange(nrow)[:, None] < n_sel) & (np.arange(nrow)[None, :] < n_cmp))
    ex = (np.arange(T)[None, :] // NSA_SEL_LEN == np.arange(nrow)[:, None])
    ex = ex.reshape(nrow, T // tk, tk).transpose(1, 0, 2)
    kv_spec = pl.BlockSpec((1, 1, T, HEAD_DIM), lambda b, g, i: (b, g, 0, 0))
    c_spec = pl.BlockSpec((1, 1, nrow, HEAD_DIM), lambda b, g, i: (b, g, 0, 0))
    hw = n_rep * HEAD_DIM
    return pl.pallas_call(
        functools.partial(_nsa_kernel, n_rep=n_rep, n_cmp=n_cmp, n_sel=n_sel,
                          n_top=min(NSA_TOPK, n_sel), tk=tk),
        grid=(B, G, T // tq),
        in_specs=[pl.BlockSpec((1, tq, hw), lambda b, g, i: (b, i, g)),
                  c_spec, c_spec, kv_spec, kv_spec, kv_spec, kv_spec,
                  pl.BlockSpec((1, tq, LANES), lambda b, g, i: (b, i, g)),
                  pl.BlockSpec((nrow, nrow), lambda b, g, i: (0, 0)),
                  pl.BlockSpec((T // tk, nrow, tk), lambda b, g, i: (0, 0, 0))],
        out_specs=pl.BlockSpec((1, tq, hw), lambda b, g, i: (b, i, g)),
        out_shape=jax.ShapeDtypeStruct((B, T, QW), BF16),
        scratch_shapes=[pltpu.VMEM((n_rep, tq, LANES), F32), pltpu.VMEM((n_rep, tq, LANES), F32),
                        pltpu.VMEM((n_rep, tq, HEAD_DIM), F32), pltpu.VMEM((n_rep, tq, HEAD_DIM), F32)],
        compiler_params=_params(("parallel", "parallel", "arbitrary")),
        name="nsa_attention",
    )(qn, kc, vc, ks, vs, kw, vw, gl3, jnp.asarray(ov, BF16), jnp.asarray(ex, BF16))


def _swa_kernel(sink_ref, q_ref, k_ref, vt_ref, o_ref, *, n_rep):
    tq = q_ref.shape[1]
    tk = KV_TILE
    W = SWA_WINDOW
    g = pl.program_id(1)
    t0 = pl.program_id(2) * tq
    q4 = jnp.concatenate([q_ref[0, :, r * HEAD_DIM:(r + 1) * HEAD_DIM] for r in range(n_rep)], axis=0)
    sink = jnp.concatenate([jnp.full((1, tq), sink_ref[g * n_rep + r] * LOG2E, F32) for r in range(n_rep)], axis=1)
    kstart = pl.multiple_of(jnp.maximum(t0 - tk, 0), tk)
    jt = kstart // tk
    krow = lax.broadcasted_iota(jnp.int32, (tk, tq), 0)
    tq_col = t0 + lax.broadcasted_iota(jnp.int32, (tk, tq), 1)
    s = []
    for c in range(2):
        rel = tq_col - (kstart + c * tk) - krow
        bias = jnp.concatenate([jnp.where((rel >= 0) & (rel < W), 0.0, NEG)] * n_rep, axis=1)
        s.append(_dot_nt(k_ref[0, 0, pl.ds(kstart + c * tk, tk), :], q4) + bias)
    m = jnp.maximum(jnp.maximum(jnp.max(s[0], axis=0, keepdims=True), jnp.max(s[1], axis=0, keepdims=True)), sink)
    e = [jnp.exp2(s[c] - m) for c in range(2)]
    den = jnp.sum(e[0], axis=0, keepdims=True) + jnp.sum(e[1], axis=0, keepdims=True) + jnp.exp2(sink - m)
    oT = (_dot(vt_ref[0, 0, jt], e[0].astype(BF16)) + _dot(vt_ref[0, 0, jt + 1], e[1].astype(BF16))) / den
    for r in range(n_rep):
        o_ref[0, :, r * HEAD_DIM:(r + 1) * HEAD_DIM] = oT[:, r * tq:(r + 1) * tq].T.astype(BF16)


def swa_attention(qn, kn, vt, sinks, tq=256):
    B, T, QW = qn.shape
    G = kn.shape[1]
    n_rep = QW // HEAD_DIM // G
    tk = KV_TILE
    assert T % tq == 0 and tq == tk and SWA_WINDOW <= tk and T >= 2 * tk
    hw = n_rep * HEAD_DIM
    return pl.pallas_call(
        functools.partial(_swa_kernel, n_rep=n_rep),
        grid=(B, G, T // tq),
        in_specs=[pl.BlockSpec(memory_space=pltpu.SMEM),
                  pl.BlockSpec((1, tq, hw), lambda b, g, i: (b, i, g)),
                  pl.BlockSpec((1, 1, T, HEAD_DIM), lambda b, g, i: (b, g, 0, 0)),
                  pl.BlockSpec((1, 1, T // tk, HEAD_DIM, tk), lambda b, g, i: (b, g, 0, 0, 0))],
        out_specs=pl.BlockSpec((1, tq, hw), lambda b, g, i: (b, i, g)),
        out_shape=jax.ShapeDtypeStruct((B, T, QW), BF16),
        compiler_params=_params(("parallel", "parallel", "parallel")),
        name="swa_attention",
    )(sinks, qn, kn, vt)


CONF_HALO = 32


def _conformer_kernel(a1_ref, a2_ref, h1_ref, h2_ref, w_ref, b_ref, g_ref, be_ref, o_ref,
                      glu_ref, conv_ref, *, blocks_per_seq):
    tm, C = a1_ref.shape
    K = tm // 8
    PRE = CONF_HALO
    nctx = CONV_WIDTH - 1
    first = (pl.program_id(0) % blocks_per_seq) == 0

    gh = jnp.where(first, 0.0, h1_ref[...].astype(F32) * _sigmoid(h2_ref[...].astype(F32)))
    sub0 = lax.broadcasted_iota(jnp.int32, (8 * nctx, LANES), 0) % 8 == 0
    for cs in range(C // LANES):
        sl = slice(cs * LANES, (cs + 1) * LANES)
        g = _to_tiles(a1_ref[:, sl].astype(F32) * _sigmoid(a2_ref[:, sl].astype(F32)))
        glu_ref[cs, 8 * PRE:, :] = g
        moved = pltpu.roll(g[tm - 8 * nctx:, :], 1, axis=0)
        halo = jnp.concatenate([jnp.broadcast_to(gh[r:r + 1, sl], (8, LANES)) for r in range(PRE - nctx, PRE)],
                               axis=0)
        glu_ref[cs, 8 * (PRE - nctx):8 * PRE, :] = jnp.where(sub0, halo, moved)

    rb = 128

    def conv_group(gi, carry):
        r0 = pl.multiple_of(gi * rb, rb)
        for cs in range(C // LANES):
            sl = slice(cs * LANES, (cs + 1) * LANES)
            acc = jnp.broadcast_to(b_ref[:, sl], (rb, LANES))
            for k in range(CONV_WIDTH):
                off = 8 * (PRE - nctx + k)
                acc = acc + w_ref[k:k + 1, sl] * glu_ref[cs, pl.ds(r0 + off, rb), :]
            conv_ref[cs, pl.ds(r0, rb), :] = acc
        return carry

    lax.fori_loop(0, tm // rb, conv_group, 0)

    ns = C // LANES
    for r0 in range(0, tm, rb):
        y = [conv_ref[cs, r0:r0 + rb, :] for cs in range(ns)]
        mu = sum(jnp.sum(v, axis=-1, keepdims=True) for v in y) / C
        d = [v - mu for v in y]
        var = sum(jnp.sum(v * v, axis=-1, keepdims=True) for v in d) / C
        rs = lax.rsqrt(var + EPS)
        for cs in range(ns):
            sl = slice(cs * LANES, (cs + 1) * LANES)
            conv_ref[cs, r0:r0 + rb, :] = _silu(d[cs] * rs * g_ref[:, sl] + be_ref[:, sl])
    for cs in range(ns):
        o_ref[:, cs * LANES:(cs + 1) * LANES] = _from_tiles(conv_ref[cs]).astype(BF16)


def conformer_conv(z, conv_w, conv_b, ln_g, ln_b, T, tm=512):
    N = z.shape[0]
    C = conv_w.shape[1]
    tm = min(tm, T)
    assert N % tm == 0 and T % tm == 0 and tm % CONF_HALO == 0 and CONV_WIDTH - 1 <= min(CONF_HALO, tm // 8)
    hb = tm // CONF_HALO
    w = jnp.zeros((CONF_HALO, C), F32).at[:CONV_WIDTH].set(conv_w)
    hmap = lambda c: (lambda i: (jnp.maximum(i * hb - 1, 0), c))
    vec = pl.BlockSpec((1, C), lambda i: (0, 0))
    return pl.pallas_call(
        functools.partial(_conformer_kernel, blocks_per_seq=T // tm),
        grid=(N // tm,),
        in_specs=[pl.BlockSpec((tm, C), lambda i: (i, 0)), pl.BlockSpec((tm, C), lambda i: (i, 1)),
                  pl.BlockSpec((CONF_HALO, C), hmap(0)), pl.BlockSpec((CONF_HALO, C), hmap(1)),
                  pl.BlockSpec((CONF_HALO, C), lambda i: (0, 0)), vec, vec, vec],
        out_specs=pl.BlockSpec((tm, C), lambda i: (i, 0)),
        out_shape=jax.ShapeDtypeStruct((N, C), BF16),
        scratch_shapes=[pltpu.VMEM((C // LANES, tm + 8 * CONF_HALO, LANES), F32),
                        pltpu.VMEM((C // LANES, tm, LANES), F32)],
        compiler_params=_params(("parallel",)),
        name="conformer_conv",
    )(z, z, z, z, w, conv_b.reshape(1, C), ln_g.reshape(1, C), ln_b.reshape(1, C))


def _gmlp_kernel(u0_ref, u1_ref, v0_ref, v1_ref, g_ref, be_ref, ws_ref, bs_ref, o_ref, *, n_groups):
    tm, hw = v0_ref.shape
    C = 2 * hw
    ch = GMLP_CHUNK
    gv = [_gelu_tanh(v0_ref[...].astype(F32)), _gelu_tanh(v1_ref[...].astype(F32))]
    mu = (jnp.sum(gv[0], axis=-1, keepdims=True) + jnp.sum(gv[1], axis=-1, keepdims=True)) / C
    d = [gv[0] - mu, gv[1] - mu]
    var = (jnp.sum(d[0] * d[0], axis=-1, keepdims=True) + jnp.sum(d[1] * d[1], axis=-1, keepdims=True)) / C
    rs = lax.rsqrt(var + EPS)
    vn = [(d[hf] * rs * g_ref[:, hf * hw:(hf + 1) * hw] + be_ref[:, hf * hw:(hf + 1) * hw]).astype(BF16)
          for hf in range(2)]
    u_refs = [u0_ref, u1_ref]
    tril = lax.broadcasted_iota(jnp.int32, (ch, ch), 0) >= lax.broadcasted_iota(jnp.int32, (ch, ch), 1)
    gph = n_groups // 2
    for gi in range(n_groups):
        hf, col = gi // gph, (gi % gph) * HEAD_DIM
        w = jnp.where(tril, ws_ref[gi], 0.0).astype(BF16)
        for c in range(tm // ch):
            rows = slice(c * ch, (c + 1) * ch)
            sp = _dot(w, vn[hf][rows, col:col + HEAD_DIM]) + bs_ref[gi]
            u = _gelu_tanh(u_refs[hf][rows, col:col + HEAD_DIM].astype(F32))
            o_ref[rows, gi * HEAD_DIM:(gi + 1) * HEAD_DIM] = (u * sp).astype(BF16)


def chunked_gmlp(z, u_col, ln_g, ln_b, w_s, b_s, T, tm=256):
    N = z.shape[0]
    n_groups, ch, _ = w_s.shape
    C = n_groups * HEAD_DIM
    hw = C // 2
    tm = min(tm, T)
    assert N % tm == 0 and T % tm == 0 and tm % ch == 0 and u_col % hw == 0
    cb = u_col // hw
    bsb = jnp.broadcast_to(b_s[:, :, None], (n_groups, ch, HEAD_DIM))
    zspec = lambda k: pl.BlockSpec((tm, hw), lambda i: (i, cb + k))
    vec = pl.BlockSpec((1, C), lambda i: (0, 0))
    return pl.pallas_call(
        functools.partial(_gmlp_kernel, n_groups=n_groups),
        grid=(N // tm,),
        in_specs=[zspec(0), zspec(1), zspec(2), zspec(3), vec, vec,
                  pl.BlockSpec((n_groups, ch, ch), lambda i: (0, 0, 0)),
                  pl.BlockSpec((n_groups, ch, HEAD_DIM), lambda i: (0, 0, 0))],
        out_specs=pl.BlockSpec((tm, C), lambda i: (i, 0)),
        out_shape=jax.ShapeDtypeStruct((N, C), BF16),
        compiler_params=_params(("parallel",)),
        name="chunked_gmlp",
    )(z, z, z, z, ln_g.reshape(1, C), ln_b.reshape(1, C), w_s, bsb)


def _even_mixer(xf, B, T, g, sc, sh, cosf, sinf, w_in, w_out, conv_w, conv_b, conv_ln_g, conv_ln_b,
                q_norm, k_norm, cmp_k_pos, cmp_k_w1, cmp_k_w2, cmp_v_pos, cmp_v_w1, cmp_v_w2):
    D = xf.shape[1]
    G = NSA_KV_GROUPS
    C = conv_w.shape[1]
    QW = D // 2
    KW = G * HEAD_DIM
    n_rep = QW // HEAD_DIM // G
    main = 2 * C + QW + 6 * KW
    wg = w_in[:, main:].reshape(D, G, n_rep * 3)
    wg = jnp.pad(wg, ((0, 0), (0, 0), (0, LANES - n_rep * 3))).reshape(D, G * LANES)
    z, gl = in_proj(xf, g, sc, sh, w_in.astype(BF16), main, wg.astype(BF16), T)
    ya = conformer_conv(z, conv_w, conv_b, conv_ln_g, conv_ln_b, T)
    q_col = 2 * C
    kv_cols = [q_col + QW + k * KW for k in range(6)]
    qn, kcr, vcr, ks, vs, kw, vw = attn_prep(z.reshape(B, T, main), cosf, sinf, q_norm, k_norm,
                                             q_col, kv_cols,
                                             (KV_PLAIN, KV_PLAIN, 1, KV_TRANSPOSED, 2, KV_TRANSPOSED), G)
    half = NSA_CMP_STRIDE * HEAD_DIM
    hid = NSA_CMP_HIDDEN
    nch = T // NSA_CMP_STRIDE

    def w1cat(w1):
        return jnp.concatenate([w1[:half], w1[half:]], axis=1).astype(BF16)

    def posrows(pos):
        return jnp.zeros((8, half), F32).at[0].set(pos[:NSA_CMP_STRIDE].reshape(half)) \
                  .at[1].set(pos[NSA_CMP_STRIDE:].reshape(half)).astype(BF16)

    def at_block_ends(tab):
        e = tab[:, NSA_CMP_LEN - 1::NSA_CMP_STRIDE]
        return jnp.pad(e, ((0, 0), (0, nch - e.shape[1]), (0, 0)))

    kc, vc = nsa_compress(kcr, vcr, w1cat(cmp_k_w1), w1cat(cmp_v_w1), posrows(cmp_k_pos), posrows(cmp_v_pos),
                          cmp_k_w2.astype(BF16), cmp_v_w2.astype(BF16), k_norm[0:1],
                          at_block_ends(cosf), at_block_ends(sinf))
    yb = nsa_attention(qn, kc, vc, ks, vs, kw, vw, gl.reshape(B, T, G * LANES))
    return ya, yb.reshape(B * T, QW), w_out.astype(BF16)


def _odd_mixer(xf, B, T, g, sc, sh, cosf, sinf, w_in, w_out, q_norm, k_norm, sinks, ln_g, ln_b, w_s, b_s):
    D = xf.shape[1]
    G = SWA_KV_HEADS
    QW = D // 2
    KW = G * HEAD_DIM
    z, = in_proj(xf, g, sc, sh, w_in.astype(BF16), w_in.shape[1], None, T)
    qn, kn, v = attn_prep(z.reshape(B, T, z.shape[1]), cosf, sinf, q_norm, k_norm.reshape(1, HEAD_DIM),
                          0, [QW, QW + KW], (0, KV_TRANSPOSED), G)
    yc = swa_attention(qn, kn, v, sinks)
    yd = chunked_gmlp(z, QW + 2 * KW, ln_g, ln_b, w_s, b_s, T)
    return yc.reshape(B * T, QW), yd, w_out.astype(BF16)


def kernel(x, c, positions, ada_w, ada_b, norm_g, ffn_w_up, ffn_conv_w, ffn_conv_b, ffn_w_down, ev_w_in, ev_w_out, ev_conv_w, ev_conv_b, ev_conv_ln_g, ev_conv_ln_b, ev_q_norm, ev_k_norm, ev_cmp_k_pos, ev_cmp_k_w1, ev_cmp_k_w2, ev_cmp_v_pos, ev_cmp_v_w1, ev_cmp_v_w2, od_w_in, od_w_out, od_q_norm, od_k_norm, od_sinks, od_gmlp_ln_g, od_gmlp_ln_b, od_gmlp_w_s, od_gmlp_b_s):
    B, T, D = x.shape
    depth = ada_w.shape[0]
    cosf, sinf = rope_tables(positions)
    mod = adaln(c, ada_w, ada_b)
    xf = x.reshape(B * T, D)
    w_up_b = ffn_w_up.astype(BF16)
    w_down_b = ffn_w_down.astype(BF16)
    cw_all = jnp.zeros((depth, 8, ffn_conv_w.shape[2]), F32).at[:, :FFN_CONV_WIDTH].set(ffn_conv_w) \
                .at[:, FFN_CONV_WIDTH].set(ffn_conv_b)
    for i in range(depth):
        sh1, sc1, g1, sh2, sc2, g2 = [m.reshape(B, 1, D) for m in jnp.split(mod[i], 6, axis=-1)]
        j = i // 2
        if i % 2 == 0:
            ya, yb, w_out = _even_mixer(xf, B, T, norm_g[i, 0], sc1, sh1, cosf, sinf, ev_w_in[j], ev_w_out[j],
                                        ev_conv_w[j], ev_conv_b[j], ev_conv_ln_g[j], ev_conv_ln_b[j],
                                        ev_q_norm[j], ev_k_norm[j], ev_cmp_k_pos[j], ev_cmp_k_w1[j],
                                        ev_cmp_k_w2[j], ev_cmp_v_pos[j], ev_cmp_v_w1[j], ev_cmp_v_w2[j])
        else:
            ya, yb, w_out = _odd_mixer(xf, B, T, norm_g[i, 0], sc1, sh1, cosf, sinf, od_w_in[j], od_w_out[j],
                                       od_q_norm[j], od_k_norm[j], od_sinks[j], od_gmlp_ln_g[j],
                                       od_gmlp_ln_b[j], od_gmlp_w_s[j], od_gmlp_b_s[j])
        xf = out_proj(ya, yb, w_out, xf, g1, T)
        xf = conv_ffn(xf, norm_g[i, 1], sc2, sh2, g2, w_up_b, cw_all, w_down_b, i, T)
    return xf.reshape(B, T, D)
```

```python
import functools

import numpy as np
import jax
import jax.numpy as jnp
from jax import lax
from jax.experimental import pallas as pl
from jax.experimental.pallas import tpu as pltpu

F32 = jnp.float32
BF16 = jnp.bfloat16

HEAD_DIM = 128
ROPE_THETA = 10000.0
EPS = 1e-6
NEG = -1e30
LOG2E = 1.4426950408889634

CONV_WIDTH = 31
NSA_KV_GROUPS = 2
NSA_CMP_LEN = 32
NSA_CMP_STRIDE = 16
NSA_CMP_HIDDEN = 256
NSA_SEL_LEN = 64
NSA_TOPK = 8
NSA_WINDOW = 256
NSA_FORCE = 1e6
SWA_KV_HEADS = 2
SWA_WINDOW = 128
GMLP_CHUNK = 128
FFN_CONV_WIDTH = 3

V7X_VMEM_BYTES = 64 * 1024 * 1024
VMEM_LIMIT = V7X_VMEM_BYTES - 8 * 1024 * 1024
LANES = 128
BF16_SUBLANES = 16


def _params(sem):
    return pltpu.CompilerParams(dimension_semantics=sem, vmem_limit_bytes=VMEM_LIMIT)


def _dot(a, b):
    return jnp.dot(a, b, preferred_element_type=F32)


def _dot_nt(a, b):
    return lax.dot_general(a, b, (((1,), (1,)), ((), ())), preferred_element_type=F32)


def _sigmoid(x):
    return 1.0 / (1.0 + jnp.exp(-x))


def _silu(x):
    return x * _sigmoid(x)


def _gelu_tanh(x):
    return 0.5 * x * (1.0 + jnp.tanh(np.sqrt(2.0 / np.pi).astype(np.float32) * (x + 0.044715 * (x * x * x))))


def _rope_kernel(pos_ref, inv_ref, cos_ref, sin_ref):
    ang = pos_ref[0].astype(F32) * inv_ref[0:1, :]
    cos_ref[0] = jnp.cos(ang)
    sin_ref[0] = jnp.sin(ang) * inv_ref[1:2, :]


def rope_tables(positions):
    B, T = positions.shape
    inv = ROPE_THETA ** (-jnp.arange(0, HEAD_DIM, 2, dtype=F32) / HEAD_DIM)
    half = HEAD_DIM // 2
    sign = jnp.concatenate([-jnp.ones((half,), F32), jnp.ones((half,), F32)])
    tab = jnp.stack([jnp.concatenate([inv, inv]), sign])
    return pl.pallas_call(
        _rope_kernel,
        grid=(B,),
        in_specs=[pl.BlockSpec((1, T, 1), lambda b: (b, 0, 0)),
                  pl.BlockSpec((2, HEAD_DIM), lambda b: (0, 0))],
        out_specs=[pl.BlockSpec((1, T, HEAD_DIM), lambda b: (b, 0, 0))] * 2,
        out_shape=[jax.ShapeDtypeStruct((B, T, HEAD_DIM), F32)] * 2,
        compiler_params=_params(("parallel",)),
        name="rope_tables",
    )(positions.reshape(B, T, 1), tab)


def _adaln_kernel(c_ref, w_ref, b_ref, o_ref):
    ca = _silu(c_ref[...]).astype(BF16)
    o_ref[0] = _dot(ca, w_ref[0].astype(BF16)) + b_ref[0]


def adaln(c, ada_w, ada_b, tn=1024):
    L, D, N6 = ada_w.shape
    B = c.shape[0]
    return pl.pallas_call(
        _adaln_kernel,
        grid=(L, N6 // tn),
        in_specs=[pl.BlockSpec((B, D), lambda l, j: (0, 0)),
                  pl.BlockSpec((1, D, tn), lambda l, j: (l, 0, j)),
                  pl.BlockSpec((1, 1, tn), lambda l, j: (l, 0, j))],
        out_specs=pl.BlockSpec((1, B, tn), lambda l, j: (l, 0, j)),
        out_shape=jax.ShapeDtypeStruct((L, B, N6), F32),
        compiler_params=_params(("parallel", "parallel")),
        name="adaln",
    )(c, ada_w, ada_b.reshape(L, 1, N6))


def _norm_mod_rows(x_ref, g, scale1, shift, out_ref, out_row0, nrows, chunk):
    def body(ci, carry):
        r = pl.multiple_of(ci * chunk, chunk)
        x = x_ref[pl.ds(r, chunk), :]
        ms = jnp.mean(x * x, axis=-1, keepdims=True)
        y = x * lax.rsqrt(ms + EPS) * g
        out_ref[pl.ds(out_row0 + r, chunk), :] = (y * scale1 + shift).astype(BF16)
        return carry
    lax.fori_loop(0, nrows // chunk, body, 0)


def _inproj_kernel(x_ref, g_ref, sc_ref, sh_ref, scn_ref, shn_ref, w_ref, *rest, has_gate, rows_per, n_chunks):
    if has_gate:
        wg_ref, o_ref, og_ref, ha_ref, hb_ref = rest
    else:
        o_ref, ha_ref, hb_ref = rest
    i = pl.program_id(0)
    j = pl.program_id(1)
    tm = x_ref.shape[0]
    g = g_ref[...]

    def norm_rows(r0, nrows, dst_ref, s_ref, t_ref):
        x = x_ref[pl.ds(r0, nrows), :]
        ms = jnp.mean(x * x, axis=-1, keepdims=True)
        y = x * lax.rsqrt(ms + EPS) * g
        dst_ref[pl.ds(r0, nrows), :] = (y * (1.0 + s_ref[0]) + t_ref[0]).astype(BF16)

    @pl.when((i == 0) & (j == 0))
    def _():
        def body(ci, carry):
            norm_rows(pl.multiple_of(ci * rows_per, BF16_SUBLANES), rows_per, ha_ref, sc_ref, sh_ref)
            return carry
        lax.fori_loop(0, tm // rows_per, body, 0)
        if tm % rows_per:
            norm_rows(tm - rows_per, rows_per, ha_ref, sc_ref, sh_ref)

    c = jnp.clip(j - 1, 0, n_chunks - 1)
    r0 = pl.multiple_of(jnp.minimum(c * rows_per, tm - rows_per), BF16_SUBLANES)

    def step(cur_ref, nxt_ref):
        if has_gate:
            @pl.when(j == 0)
            def _():
                og_ref[...] = _dot(cur_ref[...], wg_ref[...])
        o_ref[...] = _dot(cur_ref[...], w_ref[...]).astype(o_ref.dtype)
        norm_rows(r0, rows_per, nxt_ref, scn_ref, shn_ref)

    @pl.when(i % 2 == 0)
    def _():
        step(ha_ref, hb_ref)

    @pl.when(i % 2 == 1)
    def _():
        step(hb_ref, ha_ref)


IN_PROJ_TN_CAP = 1536


def in_proj(x, g, sc, sh, w, n_out, wg, T, tm=1024):
    N, D = x.shape
    Nout = n_out
    tm = min(tm, T)
    tn = max(t for t in range(256, IN_PROJ_TN_CAP + 1, 256) if Nout % t == 0 and Nout // t >= 2)
    nb, nj = N // tm, Nout // tn
    assert N % tm == 0 and T % tm == 0 and Nout % tn == 0 and nj >= 2
    n_chunks = nj - 1
    rows_per = -(-tm // n_chunks)
    rows_per = -(-rows_per // BF16_SUBLANES) * BF16_SUBLANES
    nxt = lambda i: jnp.minimum(i + 1, nb - 1)
    xmap = lambda i, j: (jnp.where((i == 0) & (j == 0), 0, nxt(i)), 0)
    bmap = lambda i, j: ((i * tm) // T, 0, 0)
    nmap = lambda i, j: ((nxt(i) * tm) // T, 0, 0)
    in_specs = [pl.BlockSpec((tm, D), xmap),
                pl.BlockSpec((1, D), lambda i, j: (0, 0)),
                pl.BlockSpec((1, 1, D), bmap), pl.BlockSpec((1, 1, D), bmap),
                pl.BlockSpec((1, 1, D), nmap), pl.BlockSpec((1, 1, D), nmap),
                pl.BlockSpec((D, tn), lambda i, j: (0, j))]
    out_specs = [pl.BlockSpec((tm, tn), lambda i, j: (i, j))]
    out_shape = [jax.ShapeDtypeStruct((N, Nout), BF16)]
    args = [x, g.reshape(1, D), sc, sh, sc, sh, w]
    if wg is not None:
        ng = wg.shape[1]
        in_specs.append(pl.BlockSpec((D, ng), lambda i, j: (0, 0)))
        out_specs.append(pl.BlockSpec((tm, ng), lambda i, j: (i, 0)))
        out_shape.append(jax.ShapeDtypeStruct((N, ng), F32))
        args.append(wg)
    return pl.pallas_call(
        functools.partial(_inproj_kernel, has_gate=wg is not None, rows_per=rows_per, n_chunks=n_chunks),
        grid=(nb, nj),
        in_specs=in_specs, out_specs=out_specs, out_shape=out_shape,
        scratch_shapes=[pltpu.VMEM((tm, D), BF16), pltpu.VMEM((tm, D), BF16)],
        compiler_params=_params(("arbitrary", "arbitrary")),
        name="in_proj",
    )(*args)


def _outproj_kernel(ya_ref, yb_ref, wa_ref, wb_ref, x_ref, gate_ref, o_ref):
    y = _dot(ya_ref[...], wa_ref[...]) + _dot(yb_ref[...], wb_ref[...])
    o_ref[...] = x_ref[...] + gate_ref[0] * y


def out_proj(ya, yb, w_out, x, gate, T, tm=512, tn=2048):
    N, D = x.shape
    Ka, Kb = ya.shape[1], yb.shape[1]
    tm = min(tm, T)
    assert N % tm == 0 and T % tm == 0 and D % tn == 0 and Ka % Kb == 0
    return pl.pallas_call(
        _outproj_kernel,
        grid=(N // tm, D // tn),
        in_specs=[pl.BlockSpec((tm, Ka), lambda i, j: (i, 0)),
                  pl.BlockSpec((tm, Kb), lambda i, j: (i, 0)),
                  pl.BlockSpec((Ka, tn), lambda i, j: (0, j)),
                  pl.BlockSpec((Kb, tn), lambda i, j: (Ka // Kb, j)),
                  pl.BlockSpec((tm, tn), lambda i, j: (i, j)),
                  pl.BlockSpec((1, 1, tn), lambda i, j: ((i * tm) // T, 0, j))],
        out_specs=pl.BlockSpec((tm, tn), lambda i, j: (i, j)),
        out_shape=jax.ShapeDtypeStruct((N, D), F32),
        compiler_params=_params(("parallel", "parallel")),
        name="out_proj",
    )(ya, yb, w_out, w_out, x, gate)


FFN_HALO = BF16_SUBLANES


def _to_tiles(a):
    rows = a.shape[0]
    return jnp.swapaxes(a.reshape(8, rows // 8, LANES), 0, 1).reshape(rows, LANES)


def _from_tiles(a):
    rows = a.shape[0]
    return jnp.swapaxes(a.reshape(rows // 8, 8, LANES), 0, 1).reshape(rows, LANES)


def _ffn_kernel(x_ref, xh_ref, g_ref, sc_ref, sh_ref, gate_ref, wa_ref, wb_ref, cwa_ref, cwb_ref,
                wd_ref, o_ref, h_ref, xp_ref, acc_ref, ya_ref, yb_ref, *, blocks_per_seq, chunk, sub):
    i = pl.program_id(0)
    j = pl.program_id(1)
    tm = x_ref.shape[0]

    K = tm // 8

    def norm_mod(xr, g, scale1, shift):
        ms = jnp.mean(xr * xr, axis=-1, keepdims=True)
        return (xr * lax.rsqrt(ms + EPS) * g) * scale1 + shift

    @pl.when(j == 0)
    def _():
        g = g_ref[...]
        scale1 = 1.0 + sc_ref[0]
        shift = sh_ref[0]
        for cb in range(x_ref.shape[1] // LANES):
            cols = slice(cb * LANES, (cb + 1) * LANES)
            xp_ref[:, cols] = _to_tiles(x_ref[:, cols])
        _norm_mod_rows(xp_ref, g, scale1, shift, h_ref, FFN_HALO, tm, chunk)
        first = (i % blocks_per_seq) == 0
        h_ref[0:FFN_HALO, :] = jnp.where(first, 0.0, norm_mod(xh_ref[...], g, scale1, shift)).astype(BF16)
        acc_ref[...] = jnp.zeros_like(acc_ref)

    h = h_ref[...]
    tf = wa_ref.shape[2]
    nsub = tf // sub
    sub0 = lax.broadcasted_iota(jnp.int32, (8, sub), 0) == 0

    def conv(y_ref, cw_ref, cs):
        H = FFN_HALO
        last = pltpu.roll(y_ref[H + tm - 8:H + tm, :], 1, axis=0)
        last2 = pltpu.roll(y_ref[H + tm - 16:H + tm - 8, :], 1, axis=0)
        m1 = jnp.where(sub0, y_ref[H - 1:H, :], last)
        m2 = jnp.where(sub0, y_ref[H - 2:H - 1, :], last2)
        y_ref[H - 8:H, :] = m1
        y_ref[H - 16:H - 8, :] = m2
        cw = cw_ref[0, :, cs]
        out = cw[FFN_CONV_WIDTH:FFN_CONV_WIDTH + 1, :]
        for k in range(FFN_CONV_WIDTH):
            off = H - 8 * (FFN_CONV_WIDTH - 1 - k)
            out = out + cw[k:k + 1, :] * y_ref[off:off + tm, :]
        return out

    for c in range(nsub):
        cs = slice(c * sub, (c + 1) * sub)
        ya_ref[c] = _dot(h, wa_ref[0, :, cs])
        yb_ref[c] = _dot(h, wb_ref[0, :, cs])
    for c in range(nsub):
        cs = slice(c * sub, (c + 1) * sub)
        act = (_silu(conv(ya_ref.at[c], cwa_ref, cs)) * conv(yb_ref.at[c], cwb_ref, cs)).astype(BF16)
        d = _dot(act, wd_ref[0, cs, :])
        for cb in range(acc_ref.shape[0]):
            acc_ref[cb] += d[:, cb * LANES:(cb + 1) * LANES]

    @pl.when(j == pl.num_programs(1) - 1)
    def _():
        for cb in range(acc_ref.shape[0]):
            cols = slice(cb * LANES, (cb + 1) * LANES)
            o_ref[:, cols] = x_ref[:, cols] + gate_ref[0, :, cols] * _from_tiles(acc_ref[cb])


def conv_ffn(x, g, sc, sh, gate, w_up, cw, w_down, layer, T, tm=512, tf=512, sub=256):
    N, D = x.shape
    DFF = w_down.shape[1]
    tm = min(tm, T)
    assert N % tm == 0 and T % tm == 0 and DFF % tf == 0 and tm % FFN_HALO == 0
    sub = min(sub, tf)
    nff = DFF // tf
    hb = tm // FFN_HALO
    bmap = lambda i, j: ((i * tm) // T, 0, 0)
    return pl.pallas_call(
        functools.partial(_ffn_kernel, blocks_per_seq=T // tm, chunk=min(128, tm), sub=min(sub, tf)),
        grid=(N // tm, nff),
        in_specs=[pl.BlockSpec((tm, D), lambda i, j: (i, 0)),
                  pl.BlockSpec((FFN_HALO, D), lambda i, j: (jnp.maximum(i * hb - 1, 0), 0)),
                  pl.BlockSpec((1, D), lambda i, j: (0, 0)),
                  pl.BlockSpec((1, 1, D), bmap),
                  pl.BlockSpec((1, 1, D), bmap),
                  pl.BlockSpec((1, 1, D), bmap),
                  pl.BlockSpec((1, D, tf), lambda i, j: (layer, 0, j)),
                  pl.BlockSpec((1, D, tf), lambda i, j: (layer, 0, j + nff)),
                  pl.BlockSpec((1, 8, tf), lambda i, j: (layer, 0, j)),
                  pl.BlockSpec((1, 8, tf), lambda i, j: (layer, 0, j + nff)),
                  pl.BlockSpec((1, tf, D), lambda i, j: (layer, j, 0))],
        out_specs=pl.BlockSpec((tm, D), lambda i, j: (i, 0)),
        out_shape=jax.ShapeDtypeStruct((N, D), F32),
        scratch_shapes=[pltpu.VMEM((tm + FFN_HALO, D), BF16), pltpu.VMEM((tm, D), F32),
                        pltpu.VMEM((D // LANES, tm, LANES), F32),
                        pltpu.VMEM((tf // sub, tm + FFN_HALO, sub), F32),
                        pltpu.VMEM((tf // sub, tm + FFN_HALO, sub), F32)],
        compiler_params=_params(("parallel", "arbitrary")),
        name="conv_ffn",
    )(x, x, g.reshape(1, D), sc, sh, gate, w_up, w_up, cw, cw, w_down)


FFN_GROUP = 64


def _ffn_kernel_pipe(x_ref, xn_ref, xhn_ref, g_ref, sc_ref, sh_ref, scn_ref, shn_ref, gate_ref,
                wa_ref, wb_ref, cwa_ref, cwb_ref, wd_ref, o_ref,
                ha_ref, hb_ref, acc_ref, ya_ref, yb_ref, *, blocks_per_seq, sub):
    i = pl.program_id(0)
    j = pl.program_id(1)
    tm = x_ref.shape[0]
    D = x_ref.shape[1]
    K = tm // 8
    n_groups = tm // FFN_GROUP
    g = g_ref[...]

    def norm_mod(xr, s_ref, t_ref):
        ms = jnp.mean(xr * xr, axis=-1, keepdims=True)
        return (xr * lax.rsqrt(ms + EPS) * g) * (1.0 + s_ref[0]) + t_ref[0]

    def tile_group(src_ref, c, dst_ref, s_ref, t_ref):
        rows = [src_ref[pl.ds(pl.multiple_of(K * s + 8 * c, 8), 8), :] for s in range(8)]
        hm = norm_mod(jnp.concatenate(rows, axis=0), s_ref, t_ref)
        tiled = jnp.concatenate([_to_tiles(hm[:, cb * LANES:(cb + 1) * LANES]) for cb in range(D // LANES)], axis=1)
        dst_ref[pl.ds(pl.multiple_of(FFN_HALO + FFN_GROUP * c, BF16_SUBLANES), FFN_GROUP), :] = tiled.astype(BF16)
        tot = tiled[0:8, :]
        for r0 in range(8, FFN_GROUP, 8):
            tot = tot + tiled[r0:r0 + 8, :]
        red = tot[:, 0:LANES]
        for cb in range(1, D // LANES):
            red = red + tot[:, cb * LANES:(cb + 1) * LANES]
        bits = pltpu.bitcast(red, jnp.uint32)
        zero = lax.shift_right_logical(lax.shift_right_logical(bits, jnp.uint32(16)), jnp.uint32(16))
        return zero[0:1, :].astype(F32)

    @pl.when((i == 0) & (j == 0))
    def _():
        def body(c, carry):
            tile_group(x_ref, c, ha_ref, sc_ref, sh_ref)
            return carry
        lax.fori_loop(0, n_groups, body, 0)
        ha_ref[0:FFN_HALO, :] = jnp.zeros((FFN_HALO, D), BF16)

    @pl.when(j == 0)
    def _():
        acc_ref[...] = jnp.zeros_like(acc_ref)

    tf = wa_ref.shape[2]
    nsub = tf // sub
    sub0 = lax.broadcasted_iota(jnp.int32, (8, sub), 0) == 0
    grp = jnp.clip(j - 1, 0, n_groups - 1)
    next_first = ((i + 1) % blocks_per_seq) == 0

    def conv(y_ref, cw_ref, cs, dep=None):
        H = FFN_HALO
        last = pltpu.roll(y_ref[H + tm - 8:H + tm, :], 1, axis=0)
        last2 = pltpu.roll(y_ref[H + tm - 16:H + tm - 8, :], 1, axis=0)
        m1 = jnp.where(sub0, y_ref[H - 1:H, :], last)
        m2 = jnp.where(sub0, y_ref[H - 2:H - 1, :], last2)
        y_ref[H - 8:H, :] = m1
        y_ref[H - 16:H - 8, :] = m2
        cw = cw_ref[0, :, cs]
        out = cw[FFN_CONV_WIDTH:FFN_CONV_WIDTH + 1, :]
        if dep is not None:
            out = out + dep
        for k in range(FFN_CONV_WIDTH):
            off = H - 8 * (FFN_CONV_WIDTH - 1 - k)
            out = out + cw[k:k + 1, :] * y_ref[off:off + tm, :]
        return out

    def step(cur_ref, nxt_ref):
        zero = tile_group(xn_ref, grp, nxt_ref, scn_ref, shn_ref)
        dep = jnp.concatenate([zero] * (sub // LANES), axis=1)
        nxt_ref[0:FFN_HALO, :] = jnp.where(next_first, 0.0, norm_mod(xhn_ref[...], scn_ref, shn_ref)).astype(BF16)
        h = cur_ref[...]
        for c in range(nsub):
            cs = slice(c * sub, (c + 1) * sub)
            ya_ref[c] = _dot(h, wa_ref[0, :, cs])
            yb_ref[c] = _dot(h, wb_ref[0, :, cs])
        for c in range(nsub):
            cs = slice(c * sub, (c + 1) * sub)
            gate_in = conv(yb_ref.at[c], cwb_ref, cs, dep if c == 0 else None)
            act = (_silu(conv(ya_ref.at[c], cwa_ref, cs)) * gate_in).astype(BF16)
            d = _dot(act, wd_ref[0, cs, :])
            for cb in range(acc_ref.shape[0]):
                acc_ref[cb] += d[:, cb * LANES:(cb + 1) * LANES]

    @pl.when(i % 2 == 0)
    def _():
        step(ha_ref, hb_ref)

    @pl.when(i % 2 == 1)
    def _():
        step(hb_ref, ha_ref)

    @pl.when(j == pl.num_programs(1) - 1)
    def _():
        for cb in range(acc_ref.shape[0]):
            cols = slice(cb * LANES, (cb + 1) * LANES)
            o_ref[:, cols] = x_ref[:, cols] + gate_ref[0, :, cols] * _from_tiles(acc_ref[cb])


def conv_ffn_pipe(x, g, sc, sh, gate, w_up, cw, w_down, layer, T, tm=512, tf=512, sub=256):
    N, D = x.shape
    DFF = w_down.shape[1]
    tm = min(tm, T)
    sub = min(sub, tf)
    nb, nff = N // tm, DFF // tf
    assert N % tm == 0 and T % tm == 0 and DFF % tf == 0 and tm % FFN_GROUP == 0 and nff - 1 >= tm // FFN_GROUP
    hb = tm // FFN_HALO
    nxt = lambda i: jnp.minimum(i + 1, nb - 1)
    bmap = lambda i, j: ((i * tm) // T, 0, 0)
    nmap = lambda i, j: ((nxt(i) * tm) // T, 0, 0)
    mod = pl.BlockSpec((1, 1, D), bmap)
    modn = pl.BlockSpec((1, 1, D), nmap)
    return pl.pallas_call(
        functools.partial(_ffn_kernel_pipe, blocks_per_seq=T // tm, sub=sub),
        grid=(nb, nff),
        in_specs=[pl.BlockSpec((tm, D), lambda i, j: (i, 0)),
                  pl.BlockSpec((tm, D), lambda i, j: (nxt(i), 0)),
                  pl.BlockSpec((FFN_HALO, D), lambda i, j: (jnp.maximum(nxt(i) * hb - 1, 0), 0)),
                  pl.BlockSpec((1, D), lambda i, j: (0, 0)),
                  mod, mod, modn, modn, mod,
                  pl.BlockSpec((1, D, tf), lambda i, j: (layer, 0, j)),
                  pl.BlockSpec((1, D, tf), lambda i, j: (layer, 0, j + nff)),
                  pl.BlockSpec((1, 8, tf), lambda i, j: (layer, 0, j)),
                  pl.BlockSpec((1, 8, tf), lambda i, j: (layer, 0, j + nff)),
                  pl.BlockSpec((1, tf, D), lambda i, j: (layer, j, 0))],
        out_specs=pl.BlockSpec((tm, D), lambda i, j: (i, 0)),
        out_shape=jax.ShapeDtypeStruct((N, D), F32),
        scratch_shapes=[pltpu.VMEM((tm + FFN_HALO, D), BF16), pltpu.VMEM((tm + FFN_HALO, D), BF16),
                        pltpu.VMEM((D // LANES, tm, LANES), F32),
                        pltpu.VMEM((tf // sub, tm + FFN_HALO, sub), F32),
                        pltpu.VMEM((tf // sub, tm + FFN_HALO, sub), F32)],
        compiler_params=_params(("arbitrary", "arbitrary")),
        name="conv_ffn",
    )(x, x, x, g.reshape(1, D), sc, sh, sc, sh, gate, w_up, w_up, cw, cw, w_down)


KV_PLAIN = -1
KV_TRANSPOSED = -2
KV_TILE = 256
SEL_UNROLL = 2


def _rope(y, cos, sin):
    return y * cos + pltpu.roll(y, HEAD_DIM // 2, axis=1) * sin


def _head_norm(x, g):
    ms = jnp.mean(x * x, axis=-1, keepdims=True)
    return x * lax.rsqrt(ms + EPS) * g


def _prep_kernel(*refs, n_q_heads, kv_kinds, n_groups):
    nkv = len(kv_kinds)
    zq_ref = refs[0]
    kv_refs = refs[1:1 + nkv]
    cos_ref, sin_ref, qn_ref, kn_ref = refs[1 + nkv:5 + nkv]
    q_out = refs[5 + nkv]
    kv_out = refs[6 + nkv:]
    cos = cos_ref[0]
    sin = sin_ref[0]
    scale = HEAD_DIM ** -0.5 * LOG2E
    for hd in range(n_q_heads):
        sl = slice(hd * HEAD_DIM, (hd + 1) * HEAD_DIM)
        y = _rope(_head_norm(zq_ref[0, :, sl].astype(F32), qn_ref[...]), cos, sin)
        q_out[0, :, sl] = (y * scale).astype(BF16)
    for idx, kind in enumerate(kv_kinds):
        for gi in range(n_groups):
            sl = slice(gi * HEAD_DIM, (gi + 1) * HEAD_DIM)
            a = kv_refs[idx][0, :, sl]
            if kind >= 0:
                a = _rope(_head_norm(a.astype(F32), kn_ref[kind:kind + 1, :]), cos, sin)
            if kind == KV_TRANSPOSED:
                for c in range(a.shape[0] // KV_TILE):
                    kv_out[idx][0, gi, c] = a[c * KV_TILE:(c + 1) * KV_TILE, :].astype(F32).T.astype(BF16)
            else:
                kv_out[idx][0, gi] = a.astype(BF16)


def attn_prep(z3, cosf, sinf, q_norm, k_norm, q_col, kv_cols, kv_kinds, n_groups, tm=512):
    B, T, _ = z3.shape
    tm = min(tm, T)
    QW = 8 * HEAD_DIM
    KW = n_groups * HEAD_DIM
    assert q_col % QW == 0 and all(c % KW == 0 for c in kv_cols) and T % tm == 0
    nkv = len(kv_cols)
    in_specs = [pl.BlockSpec((1, tm, QW), lambda b, t: (b, t, q_col // QW))]
    for c in kv_cols:
        in_specs.append(pl.BlockSpec((1, tm, KW), functools.partial(lambda b, t, cb: (b, t, cb), cb=c // KW)))
    in_specs += [pl.BlockSpec((1, tm, HEAD_DIM), lambda b, t: (b, t, 0))] * 2
    in_specs += [pl.BlockSpec((1, HEAD_DIM), lambda b, t: (0, 0)),
                 pl.BlockSpec(k_norm.shape, lambda b, t: (0, 0))]
    out_specs = [pl.BlockSpec((1, tm, QW), lambda b, t: (b, t, 0))]
    out_shape = [jax.ShapeDtypeStruct((B, T, QW), BF16)]
    for kind in kv_kinds:
        if kind == KV_TRANSPOSED:
            assert tm % KV_TILE == 0
            out_specs.append(pl.BlockSpec((1, n_groups, tm // KV_TILE, HEAD_DIM, KV_TILE),
                                          lambda b, t: (b, 0, t, 0, 0)))
            out_shape.append(jax.ShapeDtypeStruct((B, n_groups, T // KV_TILE, HEAD_DIM, KV_TILE), BF16))
        else:
            out_specs.append(pl.BlockSpec((1, n_groups, tm, HEAD_DIM), lambda b, t: (b, 0, t, 0)))
            out_shape.append(jax.ShapeDtypeStruct((B, n_groups, T, HEAD_DIM), BF16))
    return pl.pallas_call(
        functools.partial(_prep_kernel, n_q_heads=QW // HEAD_DIM, kv_kinds=tuple(kv_kinds), n_groups=n_groups),
        grid=(B, T // tm),
        in_specs=in_specs, out_specs=out_specs, out_shape=out_shape,
        compiler_params=_params(("parallel", "parallel")),
        name="attn_prep",
    )(*([z3] * (1 + nkv)), cosf, sinf, q_norm.reshape(1, HEAD_DIM), k_norm)


def _compress_kernel(ak_ref, av_ref, w1k_ref, w1v_ref, pk_ref, pv_ref, w2k_ref, w2v_ref,
                     kn_ref, cos_ref, sin_ref, kc_ref, vc_ref):
    hid = NSA_CMP_HIDDEN

    def mlp(a_ref, w1_ref, p_ref, w2_ref):
        P = _dot(a_ref[0, 0], w1_ref[...])
        Q = _dot(p_ref[...], w1_ref[...])
        pb = Q[0:1, :hid] + Q[1:2, hid:]
        nxt = pltpu.roll(P[:, hid:], P.shape[0] - 1, axis=0)
        hdn = _silu(P[:, :hid] + nxt + pb)
        return _dot(hdn.astype(BF16), w2_ref[...])

    kc = mlp(ak_ref, w1k_ref, pk_ref, w2k_ref)
    kc = _rope(_head_norm(kc, kn_ref[...]), cos_ref[0], sin_ref[0])
    kc_ref[0, 0] = kc.astype(BF16)
    vc_ref[0, 0] = mlp(av_ref, w1v_ref, pv_ref, w2v_ref).T.astype(BF16)


def nsa_compress(kcr, vcr, w1k, w1v, pk, pv, w2k, w2v, kn0, cos_end, sin_end):
    B, G, T, dh = kcr.shape
    nch = T // NSA_CMP_STRIDE
    cw = NSA_CMP_STRIDE * dh
    a_spec = pl.BlockSpec((1, 1, nch, cw), lambda b, g: (b, g, 0, 0))
    full = lambda arr: pl.BlockSpec(arr.shape, lambda b, g: (0,) * arr.ndim)
    tab_spec = pl.BlockSpec((1, nch, dh), lambda b, g: (b, 0, 0))
    o_spec = pl.BlockSpec((1, 1, nch, dh), lambda b, g: (b, g, 0, 0))
    return pl.pallas_call(
        _compress_kernel,
        grid=(B, G),
        in_specs=[a_spec, a_spec, full(w1k), full(w1v), full(pk), full(pv), full(w2k), full(w2v),
                  full(kn0), tab_spec, tab_spec],
        out_specs=[o_spec, pl.BlockSpec((1, 1, dh, nch), lambda b, g: (b, g, 0, 0))],
        out_shape=[jax.ShapeDtypeStruct((B, G, nch, dh), BF16), jax.ShapeDtypeStruct((B, G, dh, nch), BF16)],
        compiler_params=_params(("parallel", "parallel")),
        name="nsa_compress",
    )(kcr.reshape(B, G, nch, cw), vcr.reshape(B, G, nch, cw), w1k, w1v, pk, pv, w2k, w2v,
      kn0, cos_end, sin_end)


def _nsa_kernel(q_ref, kc_ref, vct_ref, ks_ref, vst_ref, kw_ref, vwt_ref, gl_ref, ov_ref, ext_ref,
                o_ref, m_ref, l_ref, acc_ref, *, n_rep, n_cmp, n_sel, n_top):
    tq = q_ref.shape[1]
    tk = KV_TILE
    nrow = kc_ref.shape[2]
    t0 = pl.program_id(2) * tq
    q4 = jnp.concatenate([q_ref[0, :, r * HEAD_DIM:(r + 1) * HEAD_DIM] for r in range(n_rep)], axis=0)

    def heads(a):
        return jnp.concatenate([a] * n_rep, axis=1)

    jb = lax.broadcasted_iota(jnp.int32, (nrow, tq), 0)
    tcol = t0 + lax.broadcasted_iota(jnp.int32, (nrow, tq), 1)
    valid_c = (jb * NSA_CMP_STRIDE + (NSA_CMP_LEN - 1) <= tcol) & (jb < n_cmp)
    validf = heads(jnp.where(valid_c, 1.0, 0.0))
    s = _dot_nt(kc_ref[0, 0], q4) + heads(jnp.where(valid_c, 0.0, NEG))
    e = jnp.exp2(s - jnp.max(s, axis=0, keepdims=True))
    p = e / jnp.sum(e, axis=0, keepdims=True) * validf
    o_cmp = _dot(vct_ref[0, 0], p.astype(BF16))
    psum = p[:, 0:tq]
    for r in range(1, n_rep):
        psum = psum + p[:, r * tq:(r + 1) * tq]

    hi = psum.astype(BF16)
    lo = (psum - hi.astype(F32)).astype(BF16)
    imp = _dot(ov_ref[...], hi) + _dot(ov_ref[...], lo)
    nsr = ov_ref.shape[0]
    jb = lax.broadcasted_iota(jnp.int32, (nsr, tq), 0)
    tcol = t0 + lax.broadcasted_iota(jnp.int32, (nsr, tq), 1)
    cur = lax.shift_right_logical(tcol, int(np.log2(NSA_SEL_LEN)))
    forced = (jb == 0) | (jb == cur) | (jb == cur - 1)
    score = jnp.where(forced, NSA_FORCE, jnp.where(jb * NSA_SEL_LEN <= tcol, imp, -1.0))
    score = jnp.where(jb < n_sel, score, -2.0)
    selT = jnp.zeros((nsr, tq), F32)
    for _ in range(n_top):
        best = jnp.max(score, axis=0, keepdims=True)
        idx = jnp.min(jnp.where(score == best, jb, nsr), axis=0, keepdims=True)
        hit = jb == idx
        selT = jnp.where(hit, 1.0, selT)
        score = jnp.where(hit, -3.0, score)
    sel = selT.astype(BF16)

    m_ref[...] = jnp.full(m_ref.shape, NEG, F32)
    l_ref[...] = jnp.zeros(l_ref.shape, F32)
    acc_ref[...] = jnp.zeros(acc_ref.shape, F32)
    krow = lax.broadcasted_iota(jnp.int32, (tk, tq), 0)
    tq_col = t0 + lax.broadcasted_iota(jnp.int32, (tk, tq), 1)

    def sel_body(it, carry):
        s = []
        for u in range(SEL_UNROLL):
            kt = it * SEL_UNROLL + u
            k0 = pl.multiple_of(kt * tk, tk)
            mask = (_dot(ext_ref[kt], sel) > 0.5) & (k0 + krow <= tq_col)
            s.append(_dot_nt(ks_ref[0, 0, pl.ds(k0, tk), :], q4) + heads(jnp.where(mask, 0.0, NEG)))
        m_old = m_ref[...]
        m_new = m_old
        for u in range(SEL_UNROLL):
            m_new = jnp.maximum(m_new, jnp.max(s[u], axis=0, keepdims=True))
        alpha = jnp.exp2(m_old - m_new)
        lsum = alpha * l_ref[...]
        pv = alpha * acc_ref[...]
        for u in range(SEL_UNROLL):
            p = jnp.exp2(s[u] - m_new)
            lsum = lsum + jnp.sum(p, axis=0, keepdims=True)
            pv = pv + _dot(vst_ref[0, 0, it * SEL_UNROLL + u], p.astype(BF16))
        l_ref[...] = lsum
        acc_ref[...] = pv
        m_ref[...] = m_new
        return carry

    n_tiles = (t0 + tq) // tk
    lax.fori_loop(0, (n_tiles + SEL_UNROLL - 1) // SEL_UNROLL, sel_body, 0)
    o_sel = acc_ref[...] / l_ref[...]

    W = NSA_WINDOW
    n_win = (W + tq) // tk
    kstart = pl.multiple_of(jnp.maximum(t0 - W, 0), tk)
    jt = kstart // tk
    sw = []
    for c in range(n_win):
        rel = tq_col - (kstart + c * tk) - krow
        bias = heads(jnp.where((rel >= 0) & (rel < W), 0.0, NEG))
        sw.append(_dot_nt(kw_ref[0, 0, pl.ds(kstart + c * tk, tk), :], q4) + bias)
    m = jnp.max(sw[0], axis=0, keepdims=True)
    for c in range(1, n_win):
        m = jnp.maximum(m, jnp.max(sw[c], axis=0, keepdims=True))
    den = jnp.zeros_like(m)
    o_win = jnp.zeros((HEAD_DIM, n_rep * tq), F32)
    for c in range(n_win):
        ew = jnp.exp2(sw[c] - m)
        den = den + jnp.sum(ew, axis=0, keepdims=True)
        o_win = o_win + _dot(vwt_ref[0, 0, jt + c], ew.astype(BF16))
    o_win = o_win / den

    gT = _sigmoid(gl_ref[0]).T
    for r in range(n_rep):
        hs = slice(r * tq, (r + 1) * tq)
        c3 = 3 * r
        oT = (gT[c3:c3 + 1, :] * o_cmp[:, hs] + gT[c3 + 1:c3 + 2, :] * o_sel[:, hs]
              + gT[c3 + 2:c3 + 3, :] * o_win[:, hs])
        o_ref[0, :, r * HEAD_DIM:(r + 1) * HEAD_DIM] = oT.T.astype(BF16)


def _nsa_kernel_v2(q_ref, kc_ref, vct_ref, ks_ref, vst_ref, kw_ref, vwt_ref, gl_ref, ov_ref, ext_ref,
                o_ref, m_ref, l_ref, acc_ref, oc_ref, *, n_rep, n_cmp, n_sel, n_top):
    tq = q_ref.shape[1]
    tk = KV_TILE
    nrow = kc_ref.shape[2]
    t0 = pl.program_id(2) * tq

    def q_head(r):
        return q_ref[0, :, r * HEAD_DIM:(r + 1) * HEAD_DIM]

    kc = kc_ref[0, 0]
    vct = vct_ref[0, 0]
    jb = lax.broadcasted_iota(jnp.int32, (nrow, tq), 0)
    tcol = t0 + lax.broadcasted_iota(jnp.int32, (nrow, tq), 1)
    valid_c = (jb * NSA_CMP_STRIDE + (NSA_CMP_LEN - 1) <= tcol) & (jb < n_cmp)
    psum = jnp.zeros((nrow, tq), F32)
    for r in range(n_rep):
        sT = jnp.where(valid_c, _dot_nt(kc, q_head(r)), NEG)
        e = jnp.exp(sT - jnp.max(sT, axis=0, keepdims=True))
        p = jnp.where(valid_c, e / jnp.sum(e, axis=0, keepdims=True), 0.0)
        psum = psum + p
        oc_ref[r] = _dot(vct, p.astype(BF16))

    hi = psum.astype(BF16)
    lo = (psum - hi.astype(F32)).astype(BF16)
    imp = _dot(ov_ref[...], hi) + _dot(ov_ref[...], lo)
    cur = lax.shift_right_logical(tcol, int(np.log2(NSA_SEL_LEN)))
    forced = (jb == 0) | (jb == cur) | (jb == cur - 1)
    score = jnp.where(forced, NSA_FORCE, jnp.where(jb * NSA_SEL_LEN <= tcol, imp, -1.0))
    score = jnp.where(jb < n_sel, score, -2.0)
    selT = jnp.zeros((nrow, tq), F32)
    for _ in range(n_top):
        best = jnp.max(score, axis=0, keepdims=True)
        idx = jnp.min(jnp.where(score == best, jb, nrow), axis=0, keepdims=True)
        hit = jb == idx
        selT = jnp.where(hit, 1.0, selT)
        score = jnp.where(hit, -3.0, score)
    sel = selT.astype(BF16)

    m_ref[...] = jnp.full(m_ref.shape, NEG, F32)
    l_ref[...] = jnp.zeros(l_ref.shape, F32)
    acc_ref[...] = jnp.zeros(acc_ref.shape, F32)
    krow = lax.broadcasted_iota(jnp.int32, (tk, tq), 0)
    tq_col = t0 + lax.broadcasted_iota(jnp.int32, (tk, tq), 1)

    def sel_body(kt, carry):
        k0 = pl.multiple_of(kt * tk, tk)
        k = ks_ref[0, 0, pl.ds(k0, tk), :]
        vt = vst_ref[0, 0, kt]
        mask = (_dot(ext_ref[kt], sel) > 0.5) & (k0 + krow <= tq_col)
        for r in range(n_rep):
            s = jnp.where(mask, _dot_nt(k, q_head(r)), NEG)
            m_old = m_ref[r]
            m_new = jnp.maximum(m_old, jnp.max(s, axis=0, keepdims=True))
            alpha = jnp.exp(m_old - m_new)
            p = jnp.exp(s - m_new)
            l_ref[r] = alpha * l_ref[r] + jnp.sum(p, axis=0, keepdims=True)
            acc_ref[r] = alpha * acc_ref[r] + _dot(vt, p.astype(BF16))
            m_ref[r] = m_new
        return carry

    lax.fori_loop(0, (t0 + tq) // tk, sel_body, 0)

    W = NSA_WINDOW
    kstart = pl.multiple_of(jnp.maximum(t0 - W, 0), tk)
    jt = kstart // tk
    wk = [kw_ref[0, 0, pl.ds(kstart + c * tk, tk), :] for c in range(2)]
    wvt = [vwt_ref[0, 0, jt + c] for c in range(2)]
    wmask = []
    for c in range(2):
        rel = tq_col - (kstart + c * tk) - krow
        wmask.append((rel >= 0) & (rel < W))
    gT = _sigmoid(gl_ref[0]).T
    for r in range(n_rep):
        s = [jnp.where(wmask[c], _dot_nt(wk[c], q_head(r)), NEG) for c in range(2)]
        m = jnp.maximum(jnp.max(s[0], axis=0, keepdims=True), jnp.max(s[1], axis=0, keepdims=True))
        e = [jnp.exp(s[c] - m) for c in range(2)]
        den = jnp.sum(e[0], axis=0, keepdims=True) + jnp.sum(e[1], axis=0, keepdims=True)
        o_win = (_dot(wvt[0], e[0].astype(BF16)) + _dot(wvt[1], e[1].astype(BF16))) / den
        o_sel = acc_ref[r] / l_ref[r]
        c3 = 3 * r
        oT = gT[c3:c3 + 1, :] * oc_ref[r] + gT[c3 + 1:c3 + 2, :] * o_sel + gT[c3 + 2:c3 + 3, :] * o_win
        o_ref[0, :, r * HEAD_DIM:(r + 1) * HEAD_DIM] = oT.T.astype(BF16)


def nsa_attention(qn, kc, vct, ks, vst, kw, vwt, gl3, tq=512):
    B, T, QW = qn.shape
    G = kc.shape[1]
    n_rep = QW // HEAD_DIM // G
    nrow = kc.shape[2]
    tk = KV_TILE
    n_cmp = (T - NSA_CMP_LEN) // NSA_CMP_STRIDE + 1
    n_sel = T // NSA_SEL_LEN
    assert T % tq == 0 and n_sel <= nrow and T >= NSA_WINDOW + tq and tq % tk == 0 and NSA_WINDOW % tk == 0
    assert (tq // tk) % SEL_UNROLL == 0
    starts = np.arange(nrow) * NSA_CMP_STRIDE
    sel_start = np.arange(nrow) * NSA_SEL_LEN
    nsr = -(-n_sel // BF16_SUBLANES) * BF16_SUBLANES
    ov = ((starts[None, :] <= sel_start[:, None] + NSA_SEL_LEN - 1)
          & (starts[None, :] + NSA_CMP_LEN - 1 >= sel_start[:, None])
          & (np.arange(nrow)[:, None] < n_sel) & (np.arange(nrow)[None, :] < n_cmp))[:nsr]
    ext = (np.arange(T)[:, None] // NSA_SEL_LEN == np.arange(nsr)[None, :]).reshape(T // tk, tk, nsr)
    k_spec = pl.BlockSpec((1, 1, T, HEAD_DIM), lambda b, g, i: (b, g, 0, 0))
    vt_spec = pl.BlockSpec((1, 1, T // tk, HEAD_DIM, tk), lambda b, g, i: (b, g, 0, 0, 0))
    c_spec = pl.BlockSpec((1, 1, nrow, HEAD_DIM), lambda b, g, i: (b, g, 0, 0))
    hw = n_rep * HEAD_DIM
    return pl.pallas_call(
        functools.partial(_nsa_kernel, n_rep=n_rep, n_cmp=n_cmp, n_sel=n_sel, n_top=min(NSA_TOPK, n_sel)),
        grid=(B, G, T // tq),
        in_specs=[pl.BlockSpec((1, tq, hw), lambda b, g, i: (b, i, g)),
                  c_spec, pl.BlockSpec((1, 1, HEAD_DIM, nrow), lambda b, g, i: (b, g, 0, 0)),
                  k_spec, vt_spec, k_spec, vt_spec,
                  pl.BlockSpec((1, tq, LANES), lambda b, g, i: (b, i, g)),
                  pl.BlockSpec((nsr, nrow), lambda b, g, i: (0, 0)),
                  pl.BlockSpec((T // tk, tk, nsr), lambda b, g, i: (0, 0, 0))],
        out_specs=pl.BlockSpec((1, tq, hw), lambda b, g, i: (b, i, g)),
        out_shape=jax.ShapeDtypeStruct((B, T, QW), BF16),
        scratch_shapes=[pltpu.VMEM((1, n_rep * tq), F32), pltpu.VMEM((1, n_rep * tq), F32),
                        pltpu.VMEM((HEAD_DIM, n_rep * tq), F32)],
        compiler_params=_params(("parallel", "parallel", "arbitrary")),
        name="nsa_attention",
    )(qn, kc, vct, ks, vst, kw, vwt, gl3, jnp.asarray(ov, BF16), jnp.asarray(ext, BF16))


def _nsa_kernel_old(q_ref, kc_ref, vc_ref, ks_ref, vs_ref, kw_ref, vw_ref, gl_ref, ov_ref, ex_ref,
                o_ref, m_ref, l_ref, acc_ref, oc_ref, *, n_rep, n_cmp, n_sel, n_top, tk):
    tq = q_ref.shape[1]
    nrow = kc_ref.shape[2]
    t0 = pl.program_id(2) * tq
    gates = _sigmoid(gl_ref[0])

    def q_head(r):
        return q_ref[0, :, r * HEAD_DIM:(r + 1) * HEAD_DIM]

    kc = kc_ref[0, 0]
    vc = vc_ref[0, 0]
    jb = lax.broadcasted_iota(jnp.int32, (nrow, tq), 0)
    tcol = t0 + lax.broadcasted_iota(jnp.int32, (nrow, tq), 1)
    valid_c = (jb * NSA_CMP_STRIDE + (NSA_CMP_LEN - 1) <= tcol) & (jb < n_cmp)
    psum = jnp.zeros((nrow, tq), F32)
    for r in range(n_rep):
        sT = jnp.where(valid_c, _dot_nt(kc, q_head(r)), NEG)
        e = jnp.exp(sT - jnp.max(sT, axis=0, keepdims=True))
        p = jnp.where(valid_c, e / jnp.sum(e, axis=0, keepdims=True), 0.0)
        psum = psum + p
        oc_ref[r] = _dot(p.T.astype(BF16), vc)

    hi = psum.astype(BF16)
    lo = (psum - hi.astype(F32)).astype(BF16)
    imp = _dot(ov_ref[...], hi) + _dot(ov_ref[...], lo)
    cur = lax.shift_right_logical(tcol, int(np.log2(NSA_SEL_LEN)))
    forced = (jb == 0) | (jb == cur) | (jb == cur - 1)
    score = jnp.where(forced, NSA_FORCE, jnp.where(jb * NSA_SEL_LEN <= tcol, imp, -1.0))
    score = jnp.where(jb < n_sel, score, -2.0)
    selT = jnp.zeros((nrow, tq), F32)
    for _ in range(n_top):
        best = jnp.max(score, axis=0, keepdims=True)
        idx = jnp.min(jnp.where(score == best, jb, nrow), axis=0, keepdims=True)
        hit = jb == idx
        selT = jnp.where(hit, 1.0, selT)
        score = jnp.where(hit, -3.0, score)
    sel = selT.T.astype(BF16)

    m_ref[...] = jnp.full(m_ref.shape, NEG, F32)
    l_ref[...] = jnp.zeros(l_ref.shape, F32)
    acc_ref[...] = jnp.zeros(acc_ref.shape, F32)
    trow = t0 + lax.broadcasted_iota(jnp.int32, (tq, tk), 0)
    kcol = lax.broadcasted_iota(jnp.int32, (tq, tk), 1)

    def sel_body(kt, carry):
        k0 = pl.multiple_of(kt * tk, tk)
        k = ks_ref[0, 0, pl.ds(k0, tk), :]
        v = vs_ref[0, 0, pl.ds(k0, tk), :]
        mask = (_dot(sel, ex_ref[kt]) > 0.5) & (k0 + kcol <= trow)
        for r in range(n_rep):
            s = jnp.where(mask, _dot_nt(q_head(r), k), NEG)
            m_old = m_ref[r][:, 0:1]
            m_new = jnp.maximum(m_old, jnp.max(s, axis=-1, keepdims=True))
            alpha = jnp.exp(m_old - m_new)
            p = jnp.exp(s - m_new)
            l_ref[r] = alpha * l_ref[r] + jnp.sum(p, axis=-1, keepdims=True)
            acc_ref[r] = alpha * acc_ref[r] + _dot(p.astype(BF16), v)
            m_ref[r] = jnp.broadcast_to(m_new, (tq, LANES))
        return carry

    lax.fori_loop(0, (t0 + tq + tk - 1) // tk, sel_body, 0)

    W = NSA_WINDOW
    kstart = pl.multiple_of(jnp.maximum(t0 - W, 0), tq)
    kw = kw_ref[0, 0, pl.ds(kstart, W + tq), :]
    vw = vw_ref[0, 0, pl.ds(kstart, W + tq), :]
    rel = (t0 + lax.broadcasted_iota(jnp.int32, (tq, W + tq), 0)
           - kstart - lax.broadcasted_iota(jnp.int32, (tq, W + tq), 1))
    wmask = (rel >= 0) & (rel < W)
    for r in range(n_rep):
        s = jnp.where(wmask, _dot_nt(q_head(r), kw), NEG)
        e = jnp.exp(s - jnp.max(s, axis=-1, keepdims=True))
        o_win = _dot(e.astype(BF16), vw) / jnp.sum(e, axis=-1, keepdims=True)
        o_sel = acc_ref[r] / l_ref[r][:, 0:1]
        c = 3 * r
        o = gates[:, c:c + 1] * oc_ref[r] + gates[:, c + 1:c + 2] * o_sel + gates[:, c + 2:c + 3] * o_win
        o_ref[0, :, r * HEAD_DIM:(r + 1) * HEAD_DIM] = o.astype(BF16)


def nsa_attention_old(qn, kc, vc, ks, vs, kw, vw, gl3, tq=256, tk=512):
    B, T, QW = qn.shape
    G = kc.shape[1]
    n_rep = QW // HEAD_DIM // G
    nrow = kc.shape[2]
    n_cmp = (T - NSA_CMP_LEN) // NSA_CMP_STRIDE + 1
    n_sel = T // NSA_SEL_LEN
    tk = min(tk, T)
    assert T % tq == 0 and T % tk == 0 and n_sel <= nrow and T >= NSA_WINDOW + tq and tq == NSA_WINDOW
    starts = np.arange(nrow) * NSA_CMP_STRIDE
    sel_start = np.arange(nrow) * NSA_SEL_LEN
    ov = ((starts[None, :] <= sel_start[:, None] + NSA_SEL_LEN - 1)
          & (starts[None, :] + NSA_CMP_LEN - 1 >= sel_start[:, None])
          & (np.arange(nrow)[:, None] < n_sel) & (np.arange(nrow)[None, :] < n_cmp))
    ex = (np.arange(T)[None, :] // NSA_SEL_LEN == np.arange(nrow)[:, None])
    ex = ex.reshape(nrow, T // tk, tk).transpose(1, 0, 2)
    kv_spec = pl.BlockSpec((1, 1, T, HEAD_DIM), lambda b, g, i: (b, g, 0, 0))
    c_spec = pl.BlockSpec((1, 1, nrow, HEAD_DIM), lambda b, g, i: (b, g, 0, 0))
    hw = n_rep * HEAD_DIM
    return pl.pallas_call(
        functools.partial(_nsa_kernel, n_rep=n_rep, n_cmp=n_cmp, n_sel=n_sel,
                          n_top=min(NSA_TOPK, n_sel), tk=tk),
        grid=(B, G, T // tq),
        in_specs=[pl.BlockSpec((1, tq, hw), lambda b, g, i: (b, i, g)),
                  c_spec, c_spec, kv_spec, kv_spec, kv_spec, kv_spec,
                  pl.BlockSpec((1, tq, LANES), lambda b, g, i: (b, i, g)),
                  pl.BlockSpec((nrow, nrow), lambda b, g, i: (0, 0)),
                  pl.BlockSpec((T // tk, nrow, tk), lambda b, g, i: (0, 0, 0))],
        out_specs=pl.BlockSpec((1, tq, hw), lambda b, g, i: (b, i, g)),
        out_shape=jax.ShapeDtypeStruct((B, T, QW), BF16),
        scratch_shapes=[pltpu.VMEM((n_rep, tq, LANES), F32), pltpu.VMEM((n_rep, tq, LANES), F32),
                        pltpu.VMEM((n_rep, tq, HEAD_DIM), F32), pltpu.VMEM((n_rep, tq, HEAD_DIM), F32)],
        compiler_params=_params(("parallel", "parallel", "arbitrary")),
        name="nsa_attention",
    )(qn, kc, vc, ks, vs, kw, vw, gl3, jnp.asarray(ov, BF16), jnp.asarray(ex, BF16))


def _swa_kernel(sink_ref, q_ref, k_ref, vt_ref, o_ref, *, n_rep):
    tq = q_ref.shape[1]
    tk = KV_TILE
    W = SWA_WINDOW
    n_win = (tk + tq) // tk
    g = pl.program_id(1)
    t0 = pl.program_id(2) * tq
    q4 = jnp.concatenate([q_ref[0, :, r * HEAD_DIM:(r + 1) * HEAD_DIM] for r in range(n_rep)], axis=0)
    sink = jnp.concatenate([jnp.full((1, tq), sink_ref[g * n_rep + r] * LOG2E, F32) for r in range(n_rep)], axis=1)
    kstart = pl.multiple_of(jnp.maximum(t0 - tk, 0), tk)
    jt = kstart // tk
    krow = lax.broadcasted_iota(jnp.int32, (tk, tq), 0)
    tq_col = t0 + lax.broadcasted_iota(jnp.int32, (tk, tq), 1)
    s = []
    for c in range(n_win):
        rel = tq_col - (kstart + c * tk) - krow
        bias = jnp.concatenate([jnp.where((rel >= 0) & (rel < W), 0.0, NEG)] * n_rep, axis=1)
        s.append(_dot_nt(k_ref[0, 0, pl.ds(kstart + c * tk, tk), :], q4) + bias)
    m = sink
    for c in range(n_win):
        m = jnp.maximum(m, jnp.max(s[c], axis=0, keepdims=True))
    den = jnp.exp2(sink - m)
    oT = jnp.zeros((HEAD_DIM, n_rep * tq), F32)
    for c in range(n_win):
        e = jnp.exp2(s[c] - m)
        den = den + jnp.sum(e, axis=0, keepdims=True)
        oT = oT + _dot(vt_ref[0, 0, jt + c], e.astype(BF16))
    oT = oT / den
    for r in range(n_rep):
        o_ref[0, :, r * HEAD_DIM:(r + 1) * HEAD_DIM] = oT[:, r * tq:(r + 1) * tq].T.astype(BF16)


def swa_attention(qn, kn, vt, sinks, tq=256):
    B, T, QW = qn.shape
    G = kn.shape[1]
    n_rep = QW // HEAD_DIM // G
    tk = KV_TILE
    assert T % tq == 0 and tq % tk == 0 and SWA_WINDOW <= tk and T >= tk + tq
    hw = n_rep * HEAD_DIM
    return pl.pallas_call(
        functools.partial(_swa_kernel, n_rep=n_rep),
        grid=(B, G, T // tq),
        in_specs=[pl.BlockSpec(memory_space=pltpu.SMEM),
                  pl.BlockSpec((1, tq, hw), lambda b, g, i: (b, i, g)),
                  pl.BlockSpec((1, 1, T, HEAD_DIM), lambda b, g, i: (b, g, 0, 0)),
                  pl.BlockSpec((1, 1, T // tk, HEAD_DIM, tk), lambda b, g, i: (b, g, 0, 0, 0))],
        out_specs=pl.BlockSpec((1, tq, hw), lambda b, g, i: (b, i, g)),
        out_shape=jax.ShapeDtypeStruct((B, T, QW), BF16),
        compiler_params=_params(("parallel", "parallel", "parallel")),
        name="swa_attention",
    )(sinks, qn, kn, vt)


CONF_HALO = 32


def _conformer_kernel(a1_ref, a2_ref, h1_ref, h2_ref, w_ref, b_ref, g_ref, be_ref, o_ref,
                      glu_ref, conv_ref, *, blocks_per_seq):
    tm, C = a1_ref.shape
    K = tm // 8
    PRE = CONF_HALO
    nctx = CONV_WIDTH - 1
    first = (pl.program_id(0) % blocks_per_seq) == 0

    gh = jnp.where(first, 0.0, h1_ref[...].astype(F32) * _sigmoid(h2_ref[...].astype(F32)))
    sub0 = lax.broadcasted_iota(jnp.int32, (8 * nctx, LANES), 0) % 8 == 0
    for cs in range(C // LANES):
        sl = slice(cs * LANES, (cs + 1) * LANES)
        g = _to_tiles(a1_ref[:, sl].astype(F32) * _sigmoid(a2_ref[:, sl].astype(F32)))
        glu_ref[cs, 8 * PRE:, :] = g
        moved = pltpu.roll(g[tm - 8 * nctx:, :], 1, axis=0)
        halo = jnp.concatenate([jnp.broadcast_to(gh[r:r + 1, sl], (8, LANES)) for r in range(PRE - nctx, PRE)],
                               axis=0)
        glu_ref[cs, 8 * (PRE - nctx):8 * PRE, :] = jnp.where(sub0, halo, moved)

    rb = 128

    def conv_group(gi, carry):
        r0 = pl.multiple_of(gi * rb, rb)
        for cs in range(C // LANES):
            sl = slice(cs * LANES, (cs + 1) * LANES)
            acc = jnp.broadcast_to(b_ref[:, sl], (rb, LANES))
            for k in range(CONV_WIDTH):
                off = 8 * (PRE - nctx + k)
                acc = acc + w_ref[k:k + 1, sl] * glu_ref[cs, pl.ds(r0 + off, rb), :]
            conv_ref[cs, pl.ds(r0, rb), :] = acc
        return carry

    lax.fori_loop(0, tm // rb, conv_group, 0)

    ns = C // LANES
    for r0 in range(0, tm, rb):
        y = [conv_ref[cs, r0:r0 + rb, :] for cs in range(ns)]
        mu = sum(jnp.sum(v, axis=-1, keepdims=True) for v in y) / C
        d = [v - mu for v in y]
        var = sum(jnp.sum(v * v, axis=-1, keepdims=True) for v in d) / C
        rs = lax.rsqrt(var + EPS)
        for cs in range(ns):
            sl = slice(cs * LANES, (cs + 1) * LANES)
            conv_ref[cs, r0:r0 + rb, :] = _silu(d[cs] * rs * g_ref[:, sl] + be_ref[:, sl])
    for cs in range(ns):
        o_ref[:, cs * LANES:(cs + 1) * LANES] = _from_tiles(conv_ref[cs]).astype(BF16)


def conformer_conv(z, conv_w, conv_b, ln_g, ln_b, T, tm=512):
    N = z.shape[0]
    C = conv_w.shape[1]
    tm = min(tm, T)
    assert N % tm == 0 and T % tm == 0 and tm % CONF_HALO == 0 and CONV_WIDTH - 1 <= min(CONF_HALO, tm // 8)
    hb = tm // CONF_HALO
    w = jnp.zeros((CONF_HALO, C), F32).at[:CONV_WIDTH].set(conv_w)
    hmap = lambda c: (lambda i: (jnp.maximum(i * hb - 1, 0), c))
    vec = pl.BlockSpec((1, C), lambda i: (0, 0))
    return pl.pallas_call(
        functools.partial(_conformer_kernel, blocks_per_seq=T // tm),
        grid=(N // tm,),
        in_specs=[pl.BlockSpec((tm, C), lambda i: (i, 0)), pl.BlockSpec((tm, C), lambda i: (i, 1)),
                  pl.BlockSpec((CONF_HALO, C), hmap(0)), pl.BlockSpec((CONF_HALO, C), hmap(1)),
                  pl.BlockSpec((CONF_HALO, C), lambda i: (0, 0)), vec, vec, vec],
        out_specs=pl.BlockSpec((tm, C), lambda i: (i, 0)),
        out_shape=jax.ShapeDtypeStruct((N, C), BF16),
        scratch_shapes=[pltpu.VMEM((C // LANES, tm + 8 * CONF_HALO, LANES), F32),
                        pltpu.VMEM((C // LANES, tm, LANES), F32)],
        compiler_params=_params(("parallel",)),
        name="conformer_conv",
    )(z, z, z, z, w, conv_b.reshape(1, C), ln_g.reshape(1, C), ln_b.reshape(1, C))


def _gmlp_kernel(u0_ref, u1_ref, v0_ref, v1_ref, g_ref, be_ref, ws_ref, bs_ref, o_ref, *, n_groups):
    tm, hw = v0_ref.shape
    C = 2 * hw
    ch = GMLP_CHUNK
    gv = [_gelu_tanh(v0_ref[...].astype(F32)), _gelu_tanh(v1_ref[...].astype(F32))]
    mu = (jnp.sum(gv[0], axis=-1, keepdims=True) + jnp.sum(gv[1], axis=-1, keepdims=True)) / C
    d = [gv[0] - mu, gv[1] - mu]
    var = (jnp.sum(d[0] * d[0], axis=-1, keepdims=True) + jnp.sum(d[1] * d[1], axis=-1, keepdims=True)) / C
    rs = lax.rsqrt(var + EPS)
    vn = [(d[hf] * rs * g_ref[:, hf * hw:(hf + 1) * hw] + be_ref[:, hf * hw:(hf + 1) * hw]).astype(BF16)
          for hf in range(2)]
    u_refs = [u0_ref, u1_ref]
    tril = lax.broadcasted_iota(jnp.int32, (ch, ch), 0) >= lax.broadcasted_iota(jnp.int32, (ch, ch), 1)
    gph = n_groups // 2
    for gi in range(n_groups):
        hf, col = gi // gph, (gi % gph) * HEAD_DIM
        w = jnp.where(tril, ws_ref[gi], 0.0).astype(BF16)
        for c in range(tm // ch):
            rows = slice(c * ch, (c + 1) * ch)
            sp = _dot(w, vn[hf][rows, col:col + HEAD_DIM]) + bs_ref[gi]
            u = _gelu_tanh(u_refs[hf][rows, col:col + HEAD_DIM].astype(F32))
            o_ref[rows, gi * HEAD_DIM:(gi + 1) * HEAD_DIM] = (u * sp).astype(BF16)


def chunked_gmlp(z, u_col, ln_g, ln_b, w_s, b_s, T, tm=256):
    N = z.shape[0]
    n_groups, ch, _ = w_s.shape
    C = n_groups * HEAD_DIM
    hw = C // 2
    tm = min(tm, T)
    assert N % tm == 0 and T % tm == 0 and tm % ch == 0 and u_col % hw == 0
    cb = u_col // hw
    bsb = jnp.broadcast_to(b_s[:, :, None], (n_groups, ch, HEAD_DIM))
    zspec = lambda k: pl.BlockSpec((tm, hw), lambda i: (i, cb + k))
    vec = pl.BlockSpec((1, C), lambda i: (0, 0))
    return pl.pallas_call(
        functools.partial(_gmlp_kernel, n_groups=n_groups),
        grid=(N // tm,),
        in_specs=[zspec(0), zspec(1), zspec(2), zspec(3), vec, vec,
                  pl.BlockSpec((n_groups, ch, ch), lambda i: (0, 0, 0)),
                  pl.BlockSpec((n_groups, ch, HEAD_DIM), lambda i: (0, 0, 0))],
        out_specs=pl.BlockSpec((tm, C), lambda i: (i, 0)),
        out_shape=jax.ShapeDtypeStruct((N, C), BF16),
        compiler_params=_params(("parallel",)),
        name="chunked_gmlp",
    )(z, z, z, z, ln_g.reshape(1, C), ln_b.reshape(1, C), w_s, bsb)


def _even_mixer(xf, B, T, g, sc, sh, cosf, sinf, w_in, w_out, conv_w, conv_b, conv_ln_g, conv_ln_b,
                q_norm, k_norm, cmp_k_pos, cmp_k_w1, cmp_k_w2, cmp_v_pos, cmp_v_w1, cmp_v_w2):
    D = xf.shape[1]
    G = NSA_KV_GROUPS
    C = conv_w.shape[1]
    QW = D // 2
    KW = G * HEAD_DIM
    n_rep = QW // HEAD_DIM // G
    main = 2 * C + QW + 6 * KW
    wg = w_in[:, main:].reshape(D, G, n_rep * 3)
    wg = jnp.pad(wg, ((0, 0), (0, 0), (0, LANES - n_rep * 3))).reshape(D, G * LANES)
    z, gl = in_proj(xf, g, sc, sh, w_in.astype(BF16), main, wg.astype(BF16), T)
    ya = conformer_conv(z, conv_w, conv_b, conv_ln_g, conv_ln_b, T)
    q_col = 2 * C
    kv_cols = [q_col + QW + k * KW for k in range(6)]
    qn, kcr, vcr, ks, vs, kw, vw = attn_prep(z.reshape(B, T, main), cosf, sinf, q_norm, k_norm,
                                             q_col, kv_cols,
                                             (KV_PLAIN, KV_PLAIN, 1, KV_TRANSPOSED, 2, KV_TRANSPOSED), G)
    half = NSA_CMP_STRIDE * HEAD_DIM
    hid = NSA_CMP_HIDDEN
    nch = T // NSA_CMP_STRIDE

    def w1cat(w1):
        return jnp.concatenate([w1[:half], w1[half:]], axis=1).astype(BF16)

    def posrows(pos):
        return jnp.zeros((8, half), F32).at[0].set(pos[:NSA_CMP_STRIDE].reshape(half)) \
                  .at[1].set(pos[NSA_CMP_STRIDE:].reshape(half)).astype(BF16)

    def at_block_ends(tab):
        e = tab[:, NSA_CMP_LEN - 1::NSA_CMP_STRIDE]
        return jnp.pad(e, ((0, 0), (0, nch - e.shape[1]), (0, 0)))

    kc, vc = nsa_compress(kcr, vcr, w1cat(cmp_k_w1), w1cat(cmp_v_w1), posrows(cmp_k_pos), posrows(cmp_v_pos),
                          cmp_k_w2.astype(BF16), cmp_v_w2.astype(BF16), k_norm[0:1],
                          at_block_ends(cosf), at_block_ends(sinf))
    yb = nsa_attention(qn, kc, vc, ks, vs, kw, vw, gl.reshape(B, T, G * LANES))
    return ya, yb.reshape(B * T, QW), w_out.astype(BF16)


def _odd_mixer(xf, B, T, g, sc, sh, cosf, sinf, w_in, w_out, q_norm, k_norm, sinks, ln_g, ln_b, w_s, b_s):
    D = xf.shape[1]
    G = SWA_KV_HEADS
    QW = D // 2
    KW = G * HEAD_DIM
    z, = in_proj(xf, g, sc, sh, w_in.astype(BF16), w_in.shape[1], None, T)
    qn, kn, v = attn_prep(z.reshape(B, T, z.shape[1]), cosf, sinf, q_norm, k_norm.reshape(1, HEAD_DIM),
                          0, [QW, QW + KW], (0, KV_TRANSPOSED), G)
    yc = swa_attention(qn, kn, v, sinks)
    yd = chunked_gmlp(z, QW + 2 * KW, ln_g, ln_b, w_s, b_s, T)
    return yc.reshape(B * T, QW), yd, w_out.astype(BF16)


def kernel(x, c, positions, ada_w, ada_b, norm_g, ffn_w_up, ffn_conv_w, ffn_conv_b, ffn_w_down, ev_w_in, ev_w_out, ev_conv_w, ev_conv_b, ev_conv_ln_g, ev_conv_ln_b, ev_q_norm, ev_k_norm, ev_cmp_k_pos, ev_cmp_k_w1, ev_cmp_k_w2, ev_cmp_v_pos, ev_cmp_v_w1, ev_cmp_v_w2, od_w_in, od_w_out, od_q_norm, od_k_norm, od_sinks, od_gmlp_ln_g, od_gmlp_ln_b, od_gmlp_w_s, od_gmlp_b_s):
    B, T, D = x.shape
    depth = ada_w.shape[0]
    cosf, sinf = rope_tables(positions)
    mod = adaln(c, ada_w, ada_b)
    xf = x.reshape(B * T, D)
    w_up_b = ffn_w_up.astype(BF16)
    w_down_b = ffn_w_down.astype(BF16)
    cw_all = jnp.zeros((depth, 8, ffn_conv_w.shape[2]), F32).at[:, :FFN_CONV_WIDTH].set(ffn_conv_w) \
                .at[:, FFN_CONV_WIDTH].set(ffn_conv_b)
    for i in range(depth):
        sh1, sc1, g1, sh2, sc2, g2 = [m.reshape(B, 1, D) for m in jnp.split(mod[i], 6, axis=-1)]
        j = i // 2
        if i % 2 == 0:
            ya, yb, w_out = _even_mixer(xf, B, T, norm_g[i, 0], sc1, sh1, cosf, sinf, ev_w_in[j], ev_w_out[j],
                                        ev_conv_w[j], ev_conv_b[j], ev_conv_ln_g[j], ev_conv_ln_b[j],
                                        ev_q_norm[j], ev_k_norm[j], ev_cmp_k_pos[j], ev_cmp_k_w1[j],
                                        ev_cmp_k_w2[j], ev_cmp_v_pos[j], ev_cmp_v_w1[j], ev_cmp_v_w2[j])
        else:
            ya, yb, w_out = _odd_mixer(xf, B, T, norm_g[i, 0], sc1, sh1, cosf, sinf, od_w_in[j], od_w_out[j],
                                       od_q_norm[j], od_k_norm[j], od_sinks[j], od_gmlp_ln_g[j],
                                       od_gmlp_ln_b[j], od_gmlp_w_s[j], od_gmlp_b_s[j])
        xf = out_proj(ya, yb, w_out, xf, g1, T)
        xf = conv_ffn(xf, norm_g[i, 1], sc2, sh2, g2, w_up_b, cw_all, w_down_b, i, T)
    return xf.reshape(B, T, D)
```

```python
import functools

import numpy as np
import jax
import jax.numpy as jnp
from jax import lax
from jax.experimental import pallas as pl
from jax.experimental.pallas import tpu as pltpu

F32 = jnp.float32
BF16 = jnp.bfloat16

HEAD_DIM = 128
ROPE_THETA = 10000.0
EPS = 1e-6
NEG = -1e30
LOG2E = 1.4426950408889634

CONV_WIDTH = 31
NSA_KV_GROUPS = 2
NSA_CMP_LEN = 32
NSA_CMP_STRIDE = 16
NSA_CMP_HIDDEN = 256
NSA_SEL_LEN = 64
NSA_TOPK = 8
NSA_WINDOW = 256
NSA_FORCE = 1e6
SWA_KV_HEADS = 2
SWA_WINDOW = 128
GMLP_CHUNK = 128
FFN_CONV_WIDTH = 3

V7X_VMEM_BYTES = 64 * 1024 * 1024
VMEM_LIMIT = V7X_VMEM_BYTES - 8 * 1024 * 1024
LANES = 128
BF16_SUBLANES = 16


def _params(sem):
    return pltpu.CompilerParams(dimension_semantics=sem, vmem_limit_bytes=VMEM_LIMIT)


def _dot(a, b):
    return jnp.dot(a, b, preferred_element_type=F32)


def _dot_nt(a, b):
    return lax.dot_general(a, b, (((1,), (1,)), ((), ())), preferred_element_type=F32)


def _sigmoid(x):
    return 1.0 / (1.0 + jnp.exp(-x))


def _silu(x):
    return x * _sigmoid(x)


def _gelu_tanh(x):
    return 0.5 * x * (1.0 + jnp.tanh(np.sqrt(2.0 / np.pi).astype(np.float32) * (x + 0.044715 * (x * x * x))))


def _rope_kernel(pos_ref, inv_ref, cos_ref, sin_ref):
    ang = pos_ref[0].astype(F32) * inv_ref[0:1, :]
    cos_ref[0] = jnp.cos(ang)
    sin_ref[0] = jnp.sin(ang) * inv_ref[1:2, :]


def rope_tables(positions):
    B, T = positions.shape
    inv = ROPE_THETA ** (-jnp.arange(0, HEAD_DIM, 2, dtype=F32) / HEAD_DIM)
    half = HEAD_DIM // 2
    sign = jnp.concatenate([-jnp.ones((half,), F32), jnp.ones((half,), F32)])
    tab = jnp.stack([jnp.concatenate([inv, inv]), sign])
    return pl.pallas_call(
        _rope_kernel,
        grid=(B,),
        in_specs=[pl.BlockSpec((1, T, 1), lambda b: (b, 0, 0)),
                  pl.BlockSpec((2, HEAD_DIM), lambda b: (0, 0))],
        out_specs=[pl.BlockSpec((1, T, HEAD_DIM), lambda b: (b, 0, 0))] * 2,
        out_shape=[jax.ShapeDtypeStruct((B, T, HEAD_DIM), F32)] * 2,
        compiler_params=_params(("parallel",)),
        name="rope_tables",
    )(positions.reshape(B, T, 1), tab)


def _adaln_kernel(c_ref, w_ref, b_ref, o_ref):
    ca = _silu(c_ref[...]).astype(BF16)
    o_ref[0] = _dot(ca, w_ref[0].astype(BF16)) + b_ref[0]


def adaln(c, ada_w, ada_b, tn=1024):
    L, D, N6 = ada_w.shape
    B = c.shape[0]
    return pl.pallas_call(
        _adaln_kernel,
        grid=(L, N6 // tn),
        in_specs=[pl.BlockSpec((B, D), lambda l, j: (0, 0)),
                  pl.BlockSpec((1, D, tn), lambda l, j: (l, 0, j)),
                  pl.BlockSpec((1, 1, tn), lambda l, j: (l, 0, j))],
        out_specs=pl.BlockSpec((1, B, tn), lambda l, j: (l, 0, j)),
        out_shape=jax.ShapeDtypeStruct((L, B, N6), F32),
        compiler_params=_params(("parallel", "parallel")),
        name="adaln",
    )(c, ada_w, ada_b.reshape(L, 1, N6))


def _norm_mod_rows(x_ref, g, scale1, shift, out_ref, out_row0, nrows, chunk):
    def body(ci, carry):
        r = pl.multiple_of(ci * chunk, chunk)
        x = x_ref[pl.ds(r, chunk), :]
        ms = jnp.mean(x * x, axis=-1, keepdims=True)
        y = x * lax.rsqrt(ms + EPS) * g
        out_ref[pl.ds(out_row0 + r, chunk), :] = (y * scale1 + shift).astype(BF16)
        return carry
    lax.fori_loop(0, nrows // chunk, body, 0)


def _inproj_kernel(x_ref, g_ref, sc_ref, sh_ref, scn_ref, shn_ref, w_ref, *rest, has_gate, rows_per, n_chunks):
    if has_gate:
        wg_ref, o_ref, og_ref, ha_ref, hb_ref = rest
    else:
        o_ref, ha_ref, hb_ref = rest
    i = pl.program_id(0)
    j = pl.program_id(1)
    tm = x_ref.shape[0]
    g = g_ref[...]

    def norm_rows(r0, nrows, dst_ref, s_ref, t_ref):
        x = x_ref[pl.ds(r0, nrows), :]
        ms = jnp.mean(x * x, axis=-1, keepdims=True)
        y = x * lax.rsqrt(ms + EPS) * g
        dst_ref[pl.ds(r0, nrows), :] = (y * (1.0 + s_ref[0]) + t_ref[0]).astype(BF16)

    @pl.when((i == 0) & (j == 0))
    def _():
        def body(ci, carry):
            norm_rows(pl.multiple_of(ci * rows_per, BF16_SUBLANES), rows_per, ha_ref, sc_ref, sh_ref)
            return carry
        lax.fori_loop(0, tm // rows_per, body, 0)
        if tm % rows_per:
            norm_rows(tm - rows_per, rows_per, ha_ref, sc_ref, sh_ref)

    c = jnp.clip(j - 1, 0, n_chunks - 1)
    r0 = pl.multiple_of(jnp.minimum(c * rows_per, tm - rows_per), BF16_SUBLANES)

    def step(cur_ref, nxt_ref):
        if has_gate:
            @pl.when(j == 0)
            def _():
                og_ref[...] = _dot(cur_ref[...], wg_ref[...])
        o_ref[...] = _dot(cur_ref[...], w_ref[...]).astype(o_ref.dtype)
        norm_rows(r0, rows_per, nxt_ref, scn_ref, shn_ref)

    @pl.when(i % 2 == 0)
    def _():
        step(ha_ref, hb_ref)

    @pl.when(i % 2 == 1)
    def _():
        step(hb_ref, ha_ref)


IN_PROJ_TN_CAP = 1792


def in_proj(x, g, sc, sh, w, n_out, wg, T, tm=1024):
    N, D = x.shape
    Nout = n_out
    tm = min(tm, T)
    tn = max(t for t in range(256, IN_PROJ_TN_CAP + 1, 256) if Nout % t == 0 and Nout // t >= 2)
    nb, nj = N // tm, Nout // tn
    assert N % tm == 0 and T % tm == 0 and Nout % tn == 0 and nj >= 2
    n_chunks = nj - 1
    rows_per = -(-tm // n_chunks)
    rows_per = -(-rows_per // BF16_SUBLANES) * BF16_SUBLANES
    nxt = lambda i: jnp.minimum(i + 1, nb - 1)
    xmap = lambda i, j: (jnp.where((i == 0) & (j == 0), 0, nxt(i)), 0)
    bmap = lambda i, j: ((i * tm) // T, 0, 0)
    nmap = lambda i, j: ((nxt(i) * tm) // T, 0, 0)
    in_specs = [pl.BlockSpec((tm, D), xmap),
                pl.BlockSpec((1, D), lambda i, j: (0, 0)),
                pl.BlockSpec((1, 1, D), bmap), pl.BlockSpec((1, 1, D), bmap),
                pl.BlockSpec((1, 1, D), nmap), pl.BlockSpec((1, 1, D), nmap),
                pl.BlockSpec((D, tn), lambda i, j: (0, j))]
    out_specs = [pl.BlockSpec((tm, tn), lambda i, j: (i, j))]
    out_shape = [jax.ShapeDtypeStruct((N, Nout), BF16)]
    args = [x, g.reshape(1, D), sc, sh, sc, sh, w]
    if wg is not None:
        ng = wg.shape[1]
        in_specs.append(pl.BlockSpec((D, ng), lambda i, j: (0, 0)))
        out_specs.append(pl.BlockSpec((tm, ng), lambda i, j: (i, 0)))
        out_shape.append(jax.ShapeDtypeStruct((N, ng), F32))
        args.append(wg)
    return pl.pallas_call(
        functools.partial(_inproj_kernel, has_gate=wg is not None, rows_per=rows_per, n_chunks=n_chunks),
        grid=(nb, nj),
        in_specs=in_specs, out_specs=out_specs, out_shape=out_shape,
        scratch_shapes=[pltpu.VMEM((tm, D), BF16), pltpu.VMEM((tm, D), BF16)],
        compiler_params=_params(("arbitrary", "arbitrary")),
        name="in_proj",
    )(*args)


def _outproj_kernel(ya_ref, yb_ref, wa_ref, wb_ref, x_ref, gate_ref, o_ref):
    y = _dot(ya_ref[...], wa_ref[...]) + _dot(yb_ref[...], wb_ref[...])
    o_ref[...] = x_ref[...] + gate_ref[0] * y


def out_proj(ya, yb, w_out, x, gate, T, tm=512, tn=2048):
    N, D = x.shape
    Ka, Kb = ya.shape[1], yb.shape[1]
    tm = min(tm, T)
    assert N % tm == 0 and T % tm == 0 and D % tn == 0 and Ka % Kb == 0
    return pl.pallas_call(
        _outproj_kernel,
        grid=(N // tm, D // tn),
        in_specs=[pl.BlockSpec((tm, Ka), lambda i, j: (i, 0)),
                  pl.BlockSpec((tm, Kb), lambda i, j: (i, 0)),
                  pl.BlockSpec((Ka, tn), lambda i, j: (0, j)),
                  pl.BlockSpec((Kb, tn), lambda i, j: (Ka // Kb, j)),
                  pl.BlockSpec((tm, tn), lambda i, j: (i, j)),
                  pl.BlockSpec((1, 1, tn), lambda i, j: ((i * tm) // T, 0, j))],
        out_specs=pl.BlockSpec((tm, tn), lambda i, j: (i, j)),
        out_shape=jax.ShapeDtypeStruct((N, D), F32),
        compiler_params=_params(("parallel", "parallel")),
        name="out_proj",
    )(ya, yb, w_out, w_out, x, gate)


FFN_HALO = BF16_SUBLANES


def _to_tiles(a):
    rows = a.shape[0]
    return jnp.swapaxes(a.reshape(8, rows // 8, LANES), 0, 1).reshape(rows, LANES)


def _from_tiles(a):
    rows = a.shape[0]
    return jnp.swapaxes(a.reshape(rows // 8, 8, LANES), 0, 1).reshape(rows, LANES)


def _ffn_kernel(x_ref, xh_ref, g_ref, sc_ref, sh_ref, gate_ref, wa_ref, wb_ref, cwa_ref, cwb_ref,
                wd_ref, o_ref, h_ref, xp_ref, acc_ref, ya_ref, yb_ref, *, blocks_per_seq, chunk, sub):
    i = pl.program_id(0)
    j = pl.program_id(1)
    tm = x_ref.shape[0]


    def norm_mod(xr, g, scale1, shift):
        ms = jnp.mean(xr * xr, axis=-1, keepdims=True)
        return (xr * lax.rsqrt(ms + EPS) * g) * scale1 + shift

    @pl.when(j == 0)
    def _():
        g = g_ref[...]
        scale1 = 1.0 + sc_ref[0]
        shift = sh_ref[0]
        for cb in range(x_ref.shape[1] // LANES):
            cols = slice(cb * LANES, (cb + 1) * LANES)
            xp_ref[:, cols] = _to_tiles(x_ref[:, cols])
        _norm_mod_rows(xp_ref, g, scale1, shift, h_ref, FFN_HALO, tm, chunk)
        first = (i % blocks_per_seq) == 0
        h_ref[0:FFN_HALO, :] = jnp.where(first, 0.0, norm_mod(xh_ref[...], g, scale1, shift)).astype(BF16)
        acc_ref[...] = jnp.zeros_like(acc_ref)

    h = h_ref[...]
    tf = wa_ref.shape[2]
    nsub = tf // sub
    sub0 = lax.broadcasted_iota(jnp.int32, (8, sub), 0) == 0

    def conv(y_ref, cw_ref, cs):
        H = FFN_HALO
        last = pltpu.roll(y_ref[H + tm - 8:H + tm, :], 1, axis=0)
        last2 = pltpu.roll(y_ref[H + tm - 16:H + tm - 8, :], 1, axis=0)
        m1 = jnp.where(sub0, y_ref[H - 1:H, :], last)
        m2 = jnp.where(sub0, y_ref[H - 2:H - 1, :], last2)
        y_ref[H - 8:H, :] = m1
        y_ref[H - 16:H - 8, :] = m2
        cw = cw_ref[0, :, cs]
        out = cw[FFN_CONV_WIDTH:FFN_CONV_WIDTH + 1, :]
        for k in range(FFN_CONV_WIDTH):
            off = H - 8 * (FFN_CONV_WIDTH - 1 - k)
            out = out + cw[k:k + 1, :] * y_ref[off:off + tm, :]
        return out

    for c in range(nsub):
        cs = slice(c * sub, (c + 1) * sub)
        ya_ref[c] = _dot(h, wa_ref[0, :, cs])
        yb_ref[c] = _dot(h, wb_ref[0, :, cs])
    for c in range(nsub):
        cs = slice(c * sub, (c + 1) * sub)
        act = (_silu(conv(ya_ref.at[c], cwa_ref, cs)) * conv(yb_ref.at[c], cwb_ref, cs)).astype(BF16)
        d = _dot(act, wd_ref[0, cs, :])
        for cb in range(acc_ref.shape[0]):
            acc_ref[cb] += d[:, cb * LANES:(cb + 1) * LANES]

    @pl.when(j == pl.num_programs(1) - 1)
    def _():
        for cb in range(acc_ref.shape[0]):
            cols = slice(cb * LANES, (cb + 1) * LANES)
            o_ref[:, cols] = x_ref[:, cols] + gate_ref[0, :, cols] * _from_tiles(acc_ref[cb])


def conv_ffn(x, g, sc, sh, gate, w_up, cw, w_down, layer, T, tm=512, tf=512, sub=256):
    N, D = x.shape
    DFF = w_down.shape[1]
    tm = min(tm, T)
    assert N % tm == 0 and T % tm == 0 and DFF % tf == 0 and tm % FFN_HALO == 0
    sub = min(sub, tf)
    nff = DFF // tf
    hb = tm // FFN_HALO
    bmap = lambda i, j: ((i * tm) // T, 0, 0)
    return pl.pallas_call(
        functools.partial(_ffn_kernel, blocks_per_seq=T // tm, chunk=min(128, tm), sub=min(sub, tf)),
        grid=(N // tm, nff),
        in_specs=[pl.BlockSpec((tm, D), lambda i, j: (i, 0)),
                  pl.BlockSpec((FFN_HALO, D), lambda i, j: (jnp.maximum(i * hb - 1, 0), 0)),
                  pl.BlockSpec((1, D), lambda i, j: (0, 0)),
                  pl.BlockSpec((1, 1, D), bmap),
                  pl.BlockSpec((1, 1, D), bmap),
                  pl.BlockSpec((1, 1, D), bmap),
                  pl.BlockSpec((1, D, tf), lambda i, j: (layer, 0, j)),
                  pl.BlockSpec((1, D, tf), lambda i, j: (layer, 0, j + nff)),
                  pl.BlockSpec((1, 8, tf), lambda i, j: (layer, 0, j)),
                  pl.BlockSpec((1, 8, tf), lambda i, j: (layer, 0, j + nff)),
                  pl.BlockSpec((1, tf, D), lambda i, j: (layer, j, 0))],
        out_specs=pl.BlockSpec((tm, D), lambda i, j: (i, 0)),
        out_shape=jax.ShapeDtypeStruct((N, D), F32),
        scratch_shapes=[pltpu.VMEM((tm + FFN_HALO, D), BF16), pltpu.VMEM((tm, D), F32),
                        pltpu.VMEM((D // LANES, tm, LANES), F32),
                        pltpu.VMEM((tf // sub, tm + FFN_HALO, sub), F32),
                        pltpu.VMEM((tf // sub, tm + FFN_HALO, sub), F32)],
        compiler_params=_params(("parallel", "arbitrary")),
        name="conv_ffn",
    )(x, x, g.reshape(1, D), sc, sh, gate, w_up, w_up, cw, cw, w_down)


KV_PLAIN = -1
KV_TRANSPOSED = -2
KV_TILE = 256
SEL_UNROLL = 2


def _rope(y, cos, sin):
    return y * cos + pltpu.roll(y, HEAD_DIM // 2, axis=1) * sin


def _head_norm(x, g):
    ms = jnp.mean(x * x, axis=-1, keepdims=True)
    return x * lax.rsqrt(ms + EPS) * g


def _prep_kernel(*refs, n_q_heads, kv_kinds, n_groups):
    nkv = len(kv_kinds)
    zq_ref = refs[0]
    kv_refs = refs[1:1 + nkv]
    cos_ref, sin_ref, qn_ref, kn_ref = refs[1 + nkv:5 + nkv]
    q_out = refs[5 + nkv]
    kv_out = refs[6 + nkv:]
    cos = cos_ref[0]
    sin = sin_ref[0]
    scale = HEAD_DIM ** -0.5 * LOG2E
    for hd in range(n_q_heads):
        sl = slice(hd * HEAD_DIM, (hd + 1) * HEAD_DIM)
        y = _rope(_head_norm(zq_ref[0, :, sl].astype(F32), qn_ref[...]), cos, sin)
        q_out[0, :, sl] = (y * scale).astype(BF16)
    for idx, kind in enumerate(kv_kinds):
        for gi in range(n_groups):
            sl = slice(gi * HEAD_DIM, (gi + 1) * HEAD_DIM)
            a = kv_refs[idx][0, :, sl]
            if kind >= 0:
                a = _rope(_head_norm(a.astype(F32), kn_ref[kind:kind + 1, :]), cos, sin)
            if kind == KV_TRANSPOSED:
                for c in range(a.shape[0] // KV_TILE):
                    kv_out[idx][0, gi, c] = a[c * KV_TILE:(c + 1) * KV_TILE, :].astype(F32).T.astype(BF16)
            else:
                kv_out[idx][0, gi] = a.astype(BF16)


def attn_prep(z3, cosf, sinf, q_norm, k_norm, q_col, kv_cols, kv_kinds, n_groups, tm=512):
    B, T, _ = z3.shape
    tm = min(tm, T)
    QW = 8 * HEAD_DIM
    KW = n_groups * HEAD_DIM
    assert q_col % QW == 0 and all(c % KW == 0 for c in kv_cols) and T % tm == 0
    nkv = len(kv_cols)
    in_specs = [pl.BlockSpec((1, tm, QW), lambda b, t: (b, t, q_col // QW))]
    for c in kv_cols:
        in_specs.append(pl.BlockSpec((1, tm, KW), functools.partial(lambda b, t, cb: (b, t, cb), cb=c // KW)))
    in_specs += [pl.BlockSpec((1, tm, HEAD_DIM), lambda b, t: (b, t, 0))] * 2
    in_specs += [pl.BlockSpec((1, HEAD_DIM), lambda b, t: (0, 0)),
                 pl.BlockSpec(k_norm.shape, lambda b, t: (0, 0))]
    out_specs = [pl.BlockSpec((1, tm, QW), lambda b, t: (b, t, 0))]
    out_shape = [jax.ShapeDtypeStruct((B, T, QW), BF16)]
    for kind in kv_kinds:
        if kind == KV_TRANSPOSED:
            assert tm % KV_TILE == 0
            out_specs.append(pl.BlockSpec((1, n_groups, tm // KV_TILE, HEAD_DIM, KV_TILE),
                                          lambda b, t: (b, 0, t, 0, 0)))
            out_shape.append(jax.ShapeDtypeStruct((B, n_groups, T // KV_TILE, HEAD_DIM, KV_TILE), BF16))
        else:
            out_specs.append(pl.BlockSpec((1, n_groups, tm, HEAD_DIM), lambda b, t: (b, 0, t, 0)))
            out_shape.append(jax.ShapeDtypeStruct((B, n_groups, T, HEAD_DIM), BF16))
    return pl.pallas_call(
        functools.partial(_prep_kernel, n_q_heads=QW // HEAD_DIM, kv_kinds=tuple(kv_kinds), n_groups=n_groups),
        grid=(B, T // tm),
        in_specs=in_specs, out_specs=out_specs, out_shape=out_shape,
        compiler_params=_params(("parallel", "parallel")),
        name="attn_prep",
    )(*([z3] * (1 + nkv)), cosf, sinf, q_norm.reshape(1, HEAD_DIM), k_norm)


def _compress_kernel(ak_ref, av_ref, w1k_ref, w1v_ref, pk_ref, pv_ref, w2k_ref, w2v_ref,
                     kn_ref, cos_ref, sin_ref, kc_ref, vc_ref):
    hid = NSA_CMP_HIDDEN

    def mlp(a_ref, w1_ref, p_ref, w2_ref):
        P = _dot(a_ref[0, 0], w1_ref[...])
        Q = _dot(p_ref[...], w1_ref[...])
        pb = Q[0:1, :hid] + Q[1:2, hid:]
        nxt = pltpu.roll(P[:, hid:], P.shape[0] - 1, axis=0)
        hdn = _silu(P[:, :hid] + nxt + pb)
        return _dot(hdn.astype(BF16), w2_ref[...])

    kc = mlp(ak_ref, w1k_ref, pk_ref, w2k_ref)
    kc = _rope(_head_norm(kc, kn_ref[...]), cos_ref[0], sin_ref[0])
    kc_ref[0, 0] = kc.astype(BF16)
    vc_ref[0, 0] = mlp(av_ref, w1v_ref, pv_ref, w2v_ref).T.astype(BF16)


def nsa_compress(kcr, vcr, w1k, w1v, pk, pv, w2k, w2v, kn0, cos_end, sin_end):
    B, G, T, dh = kcr.shape
    nch = T // NSA_CMP_STRIDE
    cw = NSA_CMP_STRIDE * dh
    a_spec = pl.BlockSpec((1, 1, nch, cw), lambda b, g: (b, g, 0, 0))
    full = lambda arr: pl.BlockSpec(arr.shape, lambda b, g: (0,) * arr.ndim)
    tab_spec = pl.BlockSpec((1, nch, dh), lambda b, g: (b, 0, 0))
    o_spec = pl.BlockSpec((1, 1, nch, dh), lambda b, g: (b, g, 0, 0))
    return pl.pallas_call(
        _compress_kernel,
        grid=(B, G),
        in_specs=[a_spec, a_spec, full(w1k), full(w1v), full(pk), full(pv), full(w2k), full(w2v),
                  full(kn0), tab_spec, tab_spec],
        out_specs=[o_spec, pl.BlockSpec((1, 1, dh, nch), lambda b, g: (b, g, 0, 0))],
        out_shape=[jax.ShapeDtypeStruct((B, G, nch, dh), BF16), jax.ShapeDtypeStruct((B, G, dh, nch), BF16)],
        compiler_params=_params(("parallel", "parallel")),
        name="nsa_compress",
    )(kcr.reshape(B, G, nch, cw), vcr.reshape(B, G, nch, cw), w1k, w1v, pk, pv, w2k, w2v,
      kn0, cos_end, sin_end)


def _nsa_kernel(q_ref, kc_ref, vct_ref, ks_ref, vst_ref, kw_ref, vwt_ref, gl_ref, ov_ref, ext_ref,
                o_ref, m_ref, l_ref, acc_ref, *, n_rep, n_cmp, n_sel, n_top):
    tq = q_ref.shape[1]
    tk = KV_TILE
    nrow = kc_ref.shape[2]
    t0 = pl.program_id(2) * tq
    q4 = jnp.concatenate([q_ref[0, :, r * HEAD_DIM:(r + 1) * HEAD_DIM] for r in range(n_rep)], axis=0)

    def heads(a):
        return jnp.concatenate([a] * n_rep, axis=1)

    jb = lax.broadcasted_iota(jnp.int32, (nrow, tq), 0)
    tcol = t0 + lax.broadcasted_iota(jnp.int32, (nrow, tq), 1)
    valid_c = (jb * NSA_CMP_STRIDE + (NSA_CMP_LEN - 1) <= tcol) & (jb < n_cmp)
    validf = heads(jnp.where(valid_c, 1.0, 0.0))
    s = _dot_nt(kc_ref[0, 0], q4) + heads(jnp.where(valid_c, 0.0, NEG))
    e = jnp.exp2(s - jnp.max(s, axis=0, keepdims=True))
    p = e / jnp.sum(e, axis=0, keepdims=True) * validf
    o_cmp = _dot(vct_ref[0, 0], p.astype(BF16))
    psum = p[:, 0:tq]
    for r in range(1, n_rep):
        psum = psum + p[:, r * tq:(r + 1) * tq]

    hi = psum.astype(BF16)
    lo = (psum - hi.astype(F32)).astype(BF16)
    imp = _dot(ov_ref[...], hi) + _dot(ov_ref[...], lo)
    nsr = ov_ref.shape[0]
    jb = lax.broadcasted_iota(jnp.int32, (nsr, tq), 0)
    tcol = t0 + lax.broadcasted_iota(jnp.int32, (nsr, tq), 1)
    cur = lax.shift_right_logical(tcol, int(np.log2(NSA_SEL_LEN)))
    forced = (jb == 0) | (jb == cur) | (jb == cur - 1)
    score = jnp.where(forced, NSA_FORCE, jnp.where(jb * NSA_SEL_LEN <= tcol, imp, -1.0))
    score = jnp.where(jb < n_sel, score, -2.0)
    selT = jnp.zeros((nsr, tq), F32)
    for _ in range(n_top):
        best = jnp.max(score, axis=0, keepdims=True)
        idx = jnp.min(jnp.where(score == best, jb, nsr), axis=0, keepdims=True)
        hit = jb == idx
        selT = jnp.where(hit, 1.0, selT)
        score = jnp.where(hit, -3.0, score)
    sel = selT.astype(BF16)

    m_ref[...] = jnp.full(m_ref.shape, NEG, F32)
    l_ref[...] = jnp.zeros(l_ref.shape, F32)
    acc_ref[...] = jnp.zeros(acc_ref.shape, F32)
    krow = lax.broadcasted_iota(jnp.int32, (tk, tq), 0)
    tq_col = t0 + lax.broadcasted_iota(jnp.int32, (tk, tq), 1)

    def sel_body(it, carry):
        s = []
        for u in range(SEL_UNROLL):
            kt = it * SEL_UNROLL + u
            k0 = pl.multiple_of(kt * tk, tk)
            mask = (_dot(ext_ref[kt], sel) > 0.5) & (k0 + krow <= tq_col)
            s.append(_dot_nt(ks_ref[0, 0, pl.ds(k0, tk), :], q4) + heads(jnp.where(mask, 0.0, NEG)))
        m_old = m_ref[...]
        m_new = m_old
        for u in range(SEL_UNROLL):
            m_new = jnp.maximum(m_new, jnp.max(s[u], axis=0, keepdims=True))
        alpha = jnp.exp2(m_old - m_new)
        lsum = alpha * l_ref[...]
        pv = alpha * acc_ref[...]
        for u in range(SEL_UNROLL):
            p = jnp.exp2(s[u] - m_new)
            lsum = lsum + jnp.sum(p, axis=0, keepdims=True)
            pv = pv + _dot(vst_ref[0, 0, it * SEL_UNROLL + u], p.astype(BF16))
        l_ref[...] = lsum
        acc_ref[...] = pv
        m_ref[...] = m_new
        return carry

    n_tiles = (t0 + tq) // tk
    lax.fori_loop(0, (n_tiles + SEL_UNROLL - 1) // SEL_UNROLL, sel_body, 0)
    o_sel = acc_ref[...] / l_ref[...]

    W = NSA_WINDOW
    n_win = (W + tq) // tk
    kstart = pl.multiple_of(jnp.maximum(t0 - W, 0), tk)
    jt = kstart // tk
    sw = []
    for c in range(n_win):
        rel = tq_col - (kstart + c * tk) - krow
        bias = heads(jnp.where((rel >= 0) & (rel < W), 0.0, NEG))
        sw.append(_dot_nt(kw_ref[0, 0, pl.ds(kstart + c * tk, tk), :], q4) + bias)
    m = jnp.max(sw[0], axis=0, keepdims=True)
    for c in range(1, n_win):
        m = jnp.maximum(m, jnp.max(sw[c], axis=0, keepdims=True))
    den = jnp.zeros_like(m)
    o_win = jnp.zeros((HEAD_DIM, n_rep * tq), F32)
    for c in range(n_win):
        ew = jnp.exp2(sw[c] - m)
        den = den + jnp.sum(ew, axis=0, keepdims=True)
        o_win = o_win + _dot(vwt_ref[0, 0, jt + c], ew.astype(BF16))
    o_win = o_win / den

    gT = _sigmoid(gl_ref[0]).T
    for r in range(n_rep):
        hs = slice(r * tq, (r + 1) * tq)
        c3 = 3 * r
        oT = (gT[c3:c3 + 1, :] * o_cmp[:, hs] + gT[c3 + 1:c3 + 2, :] * o_sel[:, hs]
              + gT[c3 + 2:c3 + 3, :] * o_win[:, hs])
        o_ref[0, :, r * HEAD_DIM:(r + 1) * HEAD_DIM] = oT.T.astype(BF16)


def nsa_attention(qn, kc, vct, ks, vst, kw, vwt, gl3, tq=512):
    B, T, QW = qn.shape
    G = kc.shape[1]
    n_rep = QW // HEAD_DIM // G
    nrow = kc.shape[2]
    tk = KV_TILE
    n_cmp = (T - NSA_CMP_LEN) // NSA_CMP_STRIDE + 1
    n_sel = T // NSA_SEL_LEN
    assert T % tq == 0 and n_sel <= nrow and T >= NSA_WINDOW + tq and tq % tk == 0 and NSA_WINDOW % tk == 0
    assert (tq // tk) % SEL_UNROLL == 0
    starts = np.arange(nrow) * NSA_CMP_STRIDE
    sel_start = np.arange(nrow) * NSA_SEL_LEN
    nsr = -(-n_sel // BF16_SUBLANES) * BF16_SUBLANES
    ov = ((starts[None, :] <= sel_start[:, None] + NSA_SEL_LEN - 1)
          & (starts[None, :] + NSA_CMP_LEN - 1 >= sel_start[:, None])
          & (np.arange(nrow)[:, None] < n_sel) & (np.arange(nrow)[None, :] < n_cmp))[:nsr]
    ext = (np.arange(T)[:, None] // NSA_SEL_LEN == np.arange(nsr)[None, :]).reshape(T // tk, tk, nsr)
    k_spec = pl.BlockSpec((1, 1, T, HEAD_DIM), lambda b, g, i: (b, g, 0, 0))
    vt_spec = pl.BlockSpec((1, 1, T // tk, HEAD_DIM, tk), lambda b, g, i: (b, g, 0, 0, 0))
    c_spec = pl.BlockSpec((1, 1, nrow, HEAD_DIM), lambda b, g, i: (b, g, 0, 0))
    hw = n_rep * HEAD_DIM
    return pl.pallas_call(
        functools.partial(_nsa_kernel, n_rep=n_rep, n_cmp=n_cmp, n_sel=n_sel, n_top=min(NSA_TOPK, n_sel)),
        grid=(B, G, T // tq),
        in_specs=[pl.BlockSpec((1, tq, hw), lambda b, g, i: (b, i, g)),
                  c_spec, pl.BlockSpec((1, 1, HEAD_DIM, nrow), lambda b, g, i: (b, g, 0, 0)),
                  k_spec, vt_spec, k_spec, vt_spec,
                  pl.BlockSpec((1, tq, LANES), lambda b, g, i: (b, i, g)),
                  pl.BlockSpec((nsr, nrow), lambda b, g, i: (0, 0)),
                  pl.BlockSpec((T // tk, tk, nsr), lambda b, g, i: (0, 0, 0))],
        out_specs=pl.BlockSpec((1, tq, hw), lambda b, g, i: (b, i, g)),
        out_shape=jax.ShapeDtypeStruct((B, T, QW), BF16),
        scratch_shapes=[pltpu.VMEM((1, n_rep * tq), F32), pltpu.VMEM((1, n_rep * tq), F32),
                        pltpu.VMEM((HEAD_DIM, n_rep * tq), F32)],
        compiler_params=_params(("parallel", "parallel", "arbitrary")),
        name="nsa_attention",
    )(qn, kc, vct, ks, vst, kw, vwt, gl3, jnp.asarray(ov, BF16), jnp.asarray(ext, BF16))


def _swa_kernel(sink_ref, q_ref, k_ref, vt_ref, o_ref, *, n_rep):
    tq = q_ref.shape[1]
    tk = KV_TILE
    W = SWA_WINDOW
    n_win = (tk + tq) // tk
    g = pl.program_id(1)
    t0 = pl.program_id(2) * tq
    q4 = jnp.concatenate([q_ref[0, :, r * HEAD_DIM:(r + 1) * HEAD_DIM] for r in range(n_rep)], axis=0)
    sink = jnp.concatenate([jnp.full((1, tq), sink_ref[g * n_rep + r] * LOG2E, F32) for r in range(n_rep)], axis=1)
    kstart = pl.multiple_of(jnp.maximum(t0 - tk, 0), tk)
    jt = kstart // tk
    krow = lax.broadcasted_iota(jnp.int32, (tk, tq), 0)
    tq_col = t0 + lax.broadcasted_iota(jnp.int32, (tk, tq), 1)
    s = []
    for c in range(n_win):
        rel = tq_col - (kstart + c * tk) - krow
        bias = jnp.concatenate([jnp.where((rel >= 0) & (rel < W), 0.0, NEG)] * n_rep, axis=1)
        s.append(_dot_nt(k_ref[0, 0, pl.ds(kstart + c * tk, tk), :], q4) + bias)
    m = sink
    for c in range(n_win):
        m = jnp.maximum(m, jnp.max(s[c], axis=0, keepdims=True))
    den = jnp.exp2(sink - m)
    oT = jnp.zeros((HEAD_DIM, n_rep * tq), F32)
    for c in range(n_win):
        e = jnp.exp2(s[c] - m)
        den = den + jnp.sum(e, axis=0, keepdims=True)
        oT = oT + _dot(vt_ref[0, 0, jt + c], e.astype(BF16))
    oT = oT / den
    for r in range(n_rep):
        o_ref[0, :, r * HEAD_DIM:(r + 1) * HEAD_DIM] = oT[:, r * tq:(r + 1) * tq].T.astype(BF16)


def swa_attention(qn, kn, vt, sinks, tq=256):
    B, T, QW = qn.shape
    G = kn.shape[1]
    n_rep = QW // HEAD_DIM // G
    tk = KV_TILE
    assert T % tq == 0 and tq % tk == 0 and SWA_WINDOW <= tk and T >= tk + tq
    hw = n_rep * HEAD_DIM
    return pl.pallas_call(
        functools.partial(_swa_kernel, n_rep=n_rep),
        grid=(B, G, T // tq),
        in_specs=[pl.BlockSpec(memory_space=pltpu.SMEM),
                  pl.BlockSpec((1, tq, hw), lambda b, g, i: (b, i, g)),
                  pl.BlockSpec((1, 1, T, HEAD_DIM), lambda b, g, i: (b, g, 0, 0)),
                  pl.BlockSpec((1, 1, T // tk, HEAD_DIM, tk), lambda b, g, i: (b, g, 0, 0, 0))],
        out_specs=pl.BlockSpec((1, tq, hw), lambda b, g, i: (b, i, g)),
        out_shape=jax.ShapeDtypeStruct((B, T, QW), BF16),
        compiler_params=_params(("parallel", "parallel", "parallel")),
        name="swa_attention",
    )(sinks, qn, kn, vt)


CONF_HALO = 32


def _conformer_kernel(a1_ref, a2_ref, h1_ref, h2_ref, w_ref, b_ref, g_ref, be_ref, o_ref,
                      glu_ref, conv_ref, *, blocks_per_seq):
    tm, C = a1_ref.shape
    PRE = CONF_HALO
    nctx = CONV_WIDTH - 1
    first = (pl.program_id(0) % blocks_per_seq) == 0

    gh = jnp.where(first, 0.0, h1_ref[...].astype(F32) * _sigmoid(h2_ref[...].astype(F32)))
    sub0 = lax.broadcasted_iota(jnp.int32, (8 * nctx, LANES), 0) % 8 == 0
    for cs in range(C // LANES):
        sl = slice(cs * LANES, (cs + 1) * LANES)
        g = _to_tiles(a1_ref[:, sl].astype(F32) * _sigmoid(a2_ref[:, sl].astype(F32)))
        glu_ref[cs, 8 * PRE:, :] = g
        moved = pltpu.roll(g[tm - 8 * nctx:, :], 1, axis=0)
        halo = jnp.concatenate([jnp.broadcast_to(gh[r:r + 1, sl], (8, LANES)) for r in range(PRE - nctx, PRE)],
                               axis=0)
        glu_ref[cs, 8 * (PRE - nctx):8 * PRE, :] = jnp.where(sub0, halo, moved)

    rb = 128

    def conv_group(gi, carry):
        r0 = pl.multiple_of(gi * rb, rb)
        for cs in range(C // LANES):
            sl = slice(cs * LANES, (cs + 1) * LANES)
            acc = jnp.broadcast_to(b_ref[:, sl], (rb, LANES))
            for k in range(CONV_WIDTH):
                off = 8 * (PRE - nctx + k)
                acc = acc + w_ref[k:k + 1, sl] * glu_ref[cs, pl.ds(r0 + off, rb), :]
            conv_ref[cs, pl.ds(r0, rb), :] = acc
        return carry

    lax.fori_loop(0, tm // rb, conv_group, 0)

    ns = C // LANES
    for r0 in range(0, tm, rb):
        y = [conv_ref[cs, r0:r0 + rb, :] for cs in range(ns)]
        mu = sum(jnp.sum(v, axis=-1, keepdims=True) for v in y) / C
        d = [v - mu for v in y]
        var = sum(jnp.sum(v * v, axis=-1, keepdims=True) for v in d) / C
        rs = lax.rsqrt(var + EPS)
        for cs in range(ns):
            sl = slice(cs * LANES, (cs + 1) * LANES)
            conv_ref[cs, r0:r0 + rb, :] = _silu(d[cs] * rs * g_ref[:, sl] + be_ref[:, sl])
    for cs in range(ns):
        o_ref[:, cs * LANES:(cs + 1) * LANES] = _from_tiles(conv_ref[cs]).astype(BF16)


def conformer_conv(z, conv_w, conv_b, ln_g, ln_b, T, tm=512):
    N = z.shape[0]
    C = conv_w.shape[1]
    tm = min(tm, T)
    assert N % tm == 0 and T % tm == 0 and tm % CONF_HALO == 0 and CONV_WIDTH - 1 <= min(CONF_HALO, tm // 8)
    hb = tm // CONF_HALO
    w = jnp.zeros((CONF_HALO, C), F32).at[:CONV_WIDTH].set(conv_w)
    hmap = lambda c: (lambda i: (jnp.maximum(i * hb - 1, 0), c))
    vec = pl.BlockSpec((1, C), lambda i: (0, 0))
    return pl.pallas_call(
        functools.partial(_conformer_kernel, blocks_per_seq=T // tm),
        grid=(N // tm,),
        in_specs=[pl.BlockSpec((tm, C), lambda i: (i, 0)), pl.BlockSpec((tm, C), lambda i: (i, 1)),
                  pl.BlockSpec((CONF_HALO, C), hmap(0)), pl.BlockSpec((CONF_HALO, C), hmap(1)),
                  pl.BlockSpec((CONF_HALO, C), lambda i: (0, 0)), vec, vec, vec],
        out_specs=pl.BlockSpec((tm, C), lambda i: (i, 0)),
        out_shape=jax.ShapeDtypeStruct((N, C), BF16),
        scratch_shapes=[pltpu.VMEM((C // LANES, tm + 8 * CONF_HALO, LANES), F32),
                        pltpu.VMEM((C // LANES, tm, LANES), F32)],
        compiler_params=_params(("parallel",)),
        name="conformer_conv",
    )(z, z, z, z, w, conv_b.reshape(1, C), ln_g.reshape(1, C), ln_b.reshape(1, C))


def _gmlp_kernel(u0_ref, u1_ref, v0_ref, v1_ref, g_ref, be_ref, ws_ref, bs_ref, o_ref, *, n_groups):
    tm, hw = v0_ref.shape
    C = 2 * hw
    ch = GMLP_CHUNK
    gv = [_gelu_tanh(v0_ref[...].astype(F32)), _gelu_tanh(v1_ref[...].astype(F32))]
    mu = (jnp.sum(gv[0], axis=-1, keepdims=True) + jnp.sum(gv[1], axis=-1, keepdims=True)) / C
    d = [gv[0] - mu, gv[1] - mu]
    var = (jnp.sum(d[0] * d[0], axis=-1, keepdims=True) + jnp.sum(d[1] * d[1], axis=-1, keepdims=True)) / C
    rs = lax.rsqrt(var + EPS)
    vn = [(d[hf] * rs * g_ref[:, hf * hw:(hf + 1) * hw] + be_ref[:, hf * hw:(hf + 1) * hw]).astype(BF16)
          for hf in range(2)]
    u_refs = [u0_ref, u1_ref]
    tril = lax.broadcasted_iota(jnp.int32, (ch, ch), 0) >= lax.broadcasted_iota(jnp.int32, (ch, ch), 1)
    gph = n_groups // 2
    for gi in range(n_groups):
        hf, col = gi // gph, (gi % gph) * HEAD_DIM
        w = jnp.where(tril, ws_ref[gi], 0.0).astype(BF16)
        for c in range(tm // ch):
            rows = slice(c * ch, (c + 1) * ch)
            sp = _dot(w, vn[hf][rows, col:col + HEAD_DIM]) + bs_ref[gi]
            u = _gelu_tanh(u_refs[hf][rows, col:col + HEAD_DIM].astype(F32))
            o_ref[rows, gi * HEAD_DIM:(gi + 1) * HEAD_DIM] = (u * sp).astype(BF16)


def chunked_gmlp(z, u_col, ln_g, ln_b, w_s, b_s, T, tm=512):
    N = z.shape[0]
    n_groups, ch, _ = w_s.shape
    C = n_groups * HEAD_DIM
    hw = C // 2
    tm = min(tm, T)
    assert N % tm == 0 and T % tm == 0 and tm % ch == 0 and u_col % hw == 0
    cb = u_col // hw
    bsb = jnp.broadcast_to(b_s[:, :, None], (n_groups, ch, HEAD_DIM))
    zspec = lambda k: pl.BlockSpec((tm, hw), lambda i: (i, cb + k))
    vec = pl.BlockSpec((1, C), lambda i: (0, 0))
    return pl.pallas_call(
        functools.partial(_gmlp_kernel, n_groups=n_groups),
        grid=(N // tm,),
        in_specs=[zspec(0), zspec(1), zspec(2), zspec(3), vec, vec,
                  pl.BlockSpec((n_groups, ch, ch), lambda i: (0, 0, 0)),
                  pl.BlockSpec((n_groups, ch, HEAD_DIM), lambda i: (0, 0, 0))],
        out_specs=pl.BlockSpec((tm, C), lambda i: (i, 0)),
        out_shape=jax.ShapeDtypeStruct((N, C), BF16),
        compiler_params=_params(("parallel",)),
        name="chunked_gmlp",
    )(z, z, z, z, ln_g.reshape(1, C), ln_b.reshape(1, C), w_s, bsb)


def _even_mixer(xf, B, T, g, sc, sh, cosf, sinf, w_in, w_out, conv_w, conv_b, conv_ln_g, conv_ln_b,
                q_norm, k_norm, cmp_k_pos, cmp_k_w1, cmp_k_w2, cmp_v_pos, cmp_v_w1, cmp_v_w2):
    D = xf.shape[1]
    G = NSA_KV_GROUPS
    C = conv_w.shape[1]
    QW = D // 2
    KW = G * HEAD_DIM
    n_rep = QW // HEAD_DIM // G
    main = 2 * C + QW + 6 * KW
    wg = w_in[:, main:].reshape(D, G, n_rep * 3)
    wg = jnp.pad(wg, ((0, 0), (0, 0), (0, LANES - n_rep * 3))).reshape(D, G * LANES)
    z, gl = in_proj(xf, g, sc, sh, w_in.astype(BF16), main, wg.astype(BF16), T)
    ya = conformer_conv(z, conv_w, conv_b, conv_ln_g, conv_ln_b, T)
    q_col = 2 * C
    kv_cols = [q_col + QW + k * KW for k in range(6)]
    qn, kcr, vcr, ks, vs, kw, vw = attn_prep(z.reshape(B, T, main), cosf, sinf, q_norm, k_norm,
                                             q_col, kv_cols,
                                             (KV_PLAIN, KV_PLAIN, 1, KV_TRANSPOSED, 2, KV_TRANSPOSED), G)
    half = NSA_CMP_STRIDE * HEAD_DIM
    hid = NSA_CMP_HIDDEN
    nch = T // NSA_CMP_STRIDE

    def w1cat(w1):
        return jnp.concatenate([w1[:half], w1[half:]], axis=1).astype(BF16)

    def posrows(pos):
        return jnp.zeros((8, half), F32).at[0].set(pos[:NSA_CMP_STRIDE].reshape(half)) \
                  .at[1].set(pos[NSA_CMP_STRIDE:].reshape(half)).astype(BF16)

    def at_block_ends(tab):
        e = tab[:, NSA_CMP_LEN - 1::NSA_CMP_STRIDE]
        return jnp.pad(e, ((0, 0), (0, nch - e.shape[1]), (0, 0)))

    kc, vc = nsa_compress(kcr, vcr, w1cat(cmp_k_w1), w1cat(cmp_v_w1), posrows(cmp_k_pos), posrows(cmp_v_pos),
                          cmp_k_w2.astype(BF16), cmp_v_w2.astype(BF16), k_norm[0:1],
                          at_block_ends(cosf), at_block_ends(sinf))
    yb = nsa_attention(qn, kc, vc, ks, vs, kw, vw, gl.reshape(B, T, G * LANES))
    return ya, yb.reshape(B * T, QW), w_out.astype(BF16)


def _odd_mixer(xf, B, T, g, sc, sh, cosf, sinf, w_in, w_out, q_norm, k_norm, sinks, ln_g, ln_b, w_s, b_s):
    D = xf.shape[1]
    G = SWA_KV_HEADS
    QW = D // 2
    KW = G * HEAD_DIM
    z, = in_proj(xf, g, sc, sh, w_in.astype(BF16), w_in.shape[1], None, T)
    qn, kn, v = attn_prep(z.reshape(B, T, z.shape[1]), cosf, sinf, q_norm, k_norm.reshape(1, HEAD_DIM),
                          0, [QW, QW + KW], (0, KV_TRANSPOSED), G)
    yc = swa_attention(qn, kn, v, sinks)
    yd = chunked_gmlp(z, QW + 2 * KW, ln_g, ln_b, w_s, b_s, T)
    return yc.reshape(B * T, QW), yd, w_out.astype(BF16)


def kernel(x, c, positions, ada_w, ada_b, norm_g, ffn_w_up, ffn_conv_w, ffn_conv_b, ffn_w_down, ev_w_in, ev_w_out, ev_conv_w, ev_conv_b, ev_conv_ln_g, ev_conv_ln_b, ev_q_norm, ev_k_norm, ev_cmp_k_pos, ev_cmp_k_w1, ev_cmp_k_w2, ev_cmp_v_pos, ev_cmp_v_w1, ev_cmp_v_w2, od_w_in, od_w_out, od_q_norm, od_k_norm, od_sinks, od_gmlp_ln_g, od_gmlp_ln_b, od_gmlp_w_s, od_gmlp_b_s):
    B, T, D = x.shape
    depth = ada_w.shape[0]
    cosf, sinf = rope_tables(positions)
    mod = adaln(c, ada_w, ada_b)
    xf = x.reshape(B * T, D)
    w_up_b = ffn_w_up.astype(BF16)
    w_down_b = ffn_w_down.astype(BF16)
    cw_all = jnp.zeros((depth, 8, ffn_conv_w.shape[2]), F32).at[:, :FFN_CONV_WIDTH].set(ffn_conv_w) \
                .at[:, FFN_CONV_WIDTH].set(ffn_conv_b)
    for i in range(depth):
        sh1, sc1, g1, sh2, sc2, g2 = [m.reshape(B, 1, D) for m in jnp.split(mod[i], 6, axis=-1)]
        j = i // 2
        if i % 2 == 0:
            ya, yb, w_out = _even_mixer(xf, B, T, norm_g[i, 0], sc1, sh1, cosf, sinf, ev_w_in[j], ev_w_out[j],
                                        ev_conv_w[j], ev_conv_b[j], ev_conv_ln_g[j], ev_conv_ln_b[j],
                                        ev_q_norm[j], ev_k_norm[j], ev_cmp_k_pos[j], ev_cmp_k_w1[j],
                                        ev_cmp_k_w2[j], ev_cmp_v_pos[j], ev_cmp_v_w1[j], ev_cmp_v_w2[j])
        else:
            ya, yb, w_out = _odd_mixer(xf, B, T, norm_g[i, 0], sc1, sh1, cosf, sinf, od_w_in[j], od_w_out[j],
                                       od_q_norm[j], od_k_norm[j], od_sinks[j], od_gmlp_ln_g[j],
                                       od_gmlp_ln_b[j], od_gmlp_w_s[j], od_gmlp_b_s[j])
        xf = out_proj(ya, yb, w_out, xf, g1, T)
        xf = conv_ffn(xf, norm_g[i, 1], sc2, sh2, g2, w_up_b, cw_all, w_down_b, i, T)
    return xf.reshape(B, T, D)
```

```python
import functools

import numpy as np
import jax
import jax.numpy as jnp
from jax import lax
from jax.experimental import pallas as pl
from jax.experimental.pallas import tpu as pltpu

F32 = jnp.float32
BF16 = jnp.bfloat16

HEAD_DIM = 128
ROPE_THETA = 10000.0
EPS = 1e-6
NEG = -1e30
LOG2E = 1.4426950408889634

CONV_WIDTH = 31
NSA_KV_GROUPS = 2
NSA_CMP_LEN = 32
NSA_CMP_STRIDE = 16
NSA_CMP_HIDDEN = 256
NSA_SEL_LEN = 64
NSA_TOPK = 8
NSA_WINDOW = 256
NSA_FORCE = 1e6
SWA_KV_HEADS = 2
SWA_WINDOW = 128
GMLP_CHUNK = 128
FFN_CONV_WIDTH = 3

V7X_VMEM_BYTES = 64 * 1024 * 1024
VMEM_LIMIT = V7X_VMEM_BYTES - 8 * 1024 * 1024
LANES = 128
BF16_SUBLANES = 16


def _params(sem):
    return pltpu.CompilerParams(dimension_semantics=sem, vmem_limit_bytes=VMEM_LIMIT)


def _dot(a, b):
    return jnp.dot(a, b, preferred_element_type=F32)


def _dot_nt(a, b):
    return lax.dot_general(a, b, (((1,), (1,)), ((), ())), preferred_element_type=F32)


def _sigmoid(x):
    return 1.0 / (1.0 + jnp.exp(-x))


def _silu(x):
    return x * _sigmoid(x)


def _gelu_tanh(x):
    return 0.5 * x * (1.0 + jnp.tanh(np.sqrt(2.0 / np.pi).astype(np.float32) * (x + 0.044715 * (x * x * x))))


def _rope_kernel(pos_ref, inv_ref, cos_ref, sin_ref):
    ang = pos_ref[0].astype(F32) * inv_ref[0:1, :]
    cos_ref[0] = jnp.cos(ang)
    sin_ref[0] = jnp.sin(ang) * inv_ref[1:2, :]


def rope_tables(positions):
    B, T = positions.shape
    inv = ROPE_THETA ** (-jnp.arange(0, HEAD_DIM, 2, dtype=F32) / HEAD_DIM)
    half = HEAD_DIM // 2
    sign = jnp.concatenate([-jnp.ones((half,), F32), jnp.ones((half,), F32)])
    tab = jnp.stack([jnp.concatenate([inv, inv]), sign])
    return pl.pallas_call(
        _rope_kernel,
        grid=(B,),
        in_specs=[pl.BlockSpec((1, T, 1), lambda b: (b, 0, 0)),
                  pl.BlockSpec((2, HEAD_DIM), lambda b: (0, 0))],
        out_specs=[pl.BlockSpec((1, T, HEAD_DIM), lambda b: (b, 0, 0))] * 2,
        out_shape=[jax.ShapeDtypeStruct((B, T, HEAD_DIM), F32)] * 2,
        compiler_params=_params(("parallel",)),
        name="rope_tables",
    )(positions.reshape(B, T, 1), tab)


def _adaln_kernel(c_ref, w_ref, b_ref, o_ref):
    ca = _silu(c_ref[...]).astype(BF16)
    o_ref[0] = _dot(ca, w_ref[0].astype(BF16)) + b_ref[0]


def adaln(c, ada_w, ada_b, tn=2048):
    L, D, N6 = ada_w.shape
    B = c.shape[0]
    return pl.pallas_call(
        _adaln_kernel,
        grid=(L, N6 // tn),
        in_specs=[pl.BlockSpec((B, D), lambda l, j: (0, 0)),
                  pl.BlockSpec((1, D, tn), lambda l, j: (l, 0, j)),
                  pl.BlockSpec((1, 1, tn), lambda l, j: (l, 0, j))],
        out_specs=pl.BlockSpec((1, B, tn), lambda l, j: (l, 0, j)),
        out_shape=jax.ShapeDtypeStruct((L, B, N6), F32),
        compiler_params=_params(("parallel", "parallel")),
        name="adaln",
    )(c, ada_w, ada_b.reshape(L, 1, N6))


def _norm_mod_rows(x_ref, g, scale1, shift, out_ref, out_row0, nrows, chunk):
    def body(ci, carry):
        r = pl.multiple_of(ci * chunk, chunk)
        x = x_ref[pl.ds(r, chunk), :]
        ms = jnp.mean(x * x, axis=-1, keepdims=True)
        y = x * lax.rsqrt(ms + EPS) * g
        out_ref[pl.ds(out_row0 + r, chunk), :] = (y * scale1 + shift).astype(BF16)
        return carry
    lax.fori_loop(0, nrows // chunk, body, 0)


def _inproj_kernel(x_ref, g_ref, sc_ref, sh_ref, scn_ref, shn_ref, w_ref, *rest, has_gate, rows_per, n_chunks):
    if has_gate:
        wg_ref, o_ref, og_ref, ha_ref, hb_ref = rest
    else:
        o_ref, ha_ref, hb_ref = rest
    i = pl.program_id(0)
    j = pl.program_id(1)
    tm = x_ref.shape[0]
    g = g_ref[...]

    def norm_rows(r0, nrows, dst_ref, s_ref, t_ref):
        x = x_ref[pl.ds(r0, nrows), :]
        ms = jnp.mean(x * x, axis=-1, keepdims=True)
        y = x * lax.rsqrt(ms + EPS) * g
        dst_ref[pl.ds(r0, nrows), :] = (y * (1.0 + s_ref[0]) + t_ref[0]).astype(BF16)

    @pl.when((i == 0) & (j == 0))
    def _():
        def body(ci, carry):
            norm_rows(pl.multiple_of(ci * rows_per, BF16_SUBLANES), rows_per, ha_ref, sc_ref, sh_ref)
            return carry
        lax.fori_loop(0, tm // rows_per, body, 0)
        if tm % rows_per:
            norm_rows(tm - rows_per, rows_per, ha_ref, sc_ref, sh_ref)

    c = jnp.clip(j - 1, 0, n_chunks - 1)
    r0 = pl.multiple_of(jnp.minimum(c * rows_per, tm - rows_per), BF16_SUBLANES)

    def step(cur_ref, nxt_ref):
        if has_gate:
            @pl.when(j == 0)
            def _():
                og_ref[...] = _dot(cur_ref[...], wg_ref[...])
        o_ref[...] = _dot(cur_ref[...], w_ref[...]).astype(o_ref.dtype)
        norm_rows(r0, rows_per, nxt_ref, scn_ref, shn_ref)

    @pl.when(i % 2 == 0)
    def _():
        step(ha_ref, hb_ref)

    @pl.when(i % 2 == 1)
    def _():
        step(hb_ref, ha_ref)


IN_PROJ_TN_CAP = 1792


def in_proj(x, g, sc, sh, w, n_out, wg, T, tm=1024):
    N, D = x.shape
    Nout = n_out
    tm = min(tm, T)
    tn = max(t for t in range(256, IN_PROJ_TN_CAP + 1, 256) if Nout % t == 0 and Nout // t >= 2)
    nb, nj = N // tm, Nout // tn
    assert N % tm == 0 and T % tm == 0 and Nout % tn == 0 and nj >= 2
    n_chunks = nj - 1
    rows_per = -(-tm // n_chunks)
    rows_per = -(-rows_per // BF16_SUBLANES) * BF16_SUBLANES
    nxt = lambda i: jnp.minimum(i + 1, nb - 1)
    xmap = lambda i, j: (jnp.where((i == 0) & (j == 0), 0, nxt(i)), 0)
    bmap = lambda i, j: ((i * tm) // T, 0, 0)
    nmap = lambda i, j: ((nxt(i) * tm) // T, 0, 0)
    in_specs = [pl.BlockSpec((tm, D), xmap),
                pl.BlockSpec((1, D), lambda i, j: (0, 0)),
                pl.BlockSpec((1, 1, D), bmap), pl.BlockSpec((1, 1, D), bmap),
                pl.BlockSpec((1, 1, D), nmap), pl.BlockSpec((1, 1, D), nmap),
                pl.BlockSpec((D, tn), lambda i, j: (0, j))]
    out_specs = [pl.BlockSpec((tm, tn), lambda i, j: (i, j))]
    out_shape = [jax.ShapeDtypeStruct((N, Nout), BF16)]
    args = [x, g.reshape(1, D), sc, sh, sc, sh, w]
    if wg is not None:
        ng = wg.shape[1]
        in_specs.append(pl.BlockSpec((D, ng), lambda i, j: (0, 0)))
        out_specs.append(pl.BlockSpec((tm, ng), lambda i, j: (i, 0)))
        out_shape.append(jax.ShapeDtypeStruct((N, ng), F32))
        args.append(wg)
    return pl.pallas_call(
        functools.partial(_inproj_kernel, has_gate=wg is not None, rows_per=rows_per, n_chunks=n_chunks),
        grid=(nb, nj),
        in_specs=in_specs, out_specs=out_specs, out_shape=out_shape,
        scratch_shapes=[pltpu.VMEM((tm, D), BF16), pltpu.VMEM((tm, D), BF16)],
        compiler_params=_params(("arbitrary", "arbitrary")),
        name="in_proj",
    )(*args)


def _outproj_kernel(ya_ref, yb_ref, wa_ref, wb_ref, x_ref, gate_ref, o_ref):
    y = _dot(ya_ref[...], wa_ref[...]) + _dot(yb_ref[...], wb_ref[...])
    o_ref[...] = x_ref[...] + gate_ref[0] * y


def out_proj(ya, yb, w_out, x, gate, T, tm=512, tn=2048):
    N, D = x.shape
    Ka, Kb = ya.shape[1], yb.shape[1]
    tm = min(tm, T)
    assert N % tm == 0 and T % tm == 0 and D % tn == 0 and Ka % Kb == 0
    return pl.pallas_call(
        _outproj_kernel,
        grid=(N // tm, D // tn),
        in_specs=[pl.BlockSpec((tm, Ka), lambda i, j: (i, 0)),
                  pl.BlockSpec((tm, Kb), lambda i, j: (i, 0)),
                  pl.BlockSpec((Ka, tn), lambda i, j: (0, j)),
                  pl.BlockSpec((Kb, tn), lambda i, j: (Ka // Kb, j)),
                  pl.BlockSpec((tm, tn), lambda i, j: (i, j)),
                  pl.BlockSpec((1, 1, tn), lambda i, j: ((i * tm) // T, 0, j))],
        out_specs=pl.BlockSpec((tm, tn), lambda i, j: (i, j)),
        out_shape=jax.ShapeDtypeStruct((N, D), F32),
        compiler_params=_params(("parallel", "parallel")),
        name="out_proj",
    )(ya, yb, w_out, w_out, x, gate)


FFN_HALO = BF16_SUBLANES


def _to_tiles(a):
    rows = a.shape[0]
    return jnp.swapaxes(a.reshape(8, rows // 8, LANES), 0, 1).reshape(rows, LANES)


def _from_tiles(a):
    rows = a.shape[0]
    return jnp.swapaxes(a.reshape(rows // 8, 8, LANES), 0, 1).reshape(rows, LANES)


def _ffn_kernel(x_ref, xh_ref, g_ref, sc_ref, sh_ref, gate_ref, wa_ref, wb_ref, cwa_ref, cwb_ref,
                wd_ref, o_ref, h_ref, xp_ref, acc_ref, ya_ref, yb_ref, *, blocks_per_seq, chunk, sub):
    i = pl.program_id(0)
    j = pl.program_id(1)
    tm = x_ref.shape[0]


    def norm_mod(xr, g, scale1, shift):
        ms = jnp.mean(xr * xr, axis=-1, keepdims=True)
        return (xr * lax.rsqrt(ms + EPS) * g) * scale1 + shift

    @pl.when(j == 0)
    def _():
        g = g_ref[...]
        scale1 = 1.0 + sc_ref[0]
        shift = sh_ref[0]
        for cb in range(x_ref.shape[1] // LANES):
            cols = slice(cb * LANES, (cb + 1) * LANES)
            xp_ref[:, cols] = _to_tiles(x_ref[:, cols])
        _norm_mod_rows(xp_ref, g, scale1, shift, h_ref, FFN_HALO, tm, chunk)
        first = (i % blocks_per_seq) == 0
        h_ref[0:FFN_HALO, :] = jnp.where(first, 0.0, norm_mod(xh_ref[...], g, scale1, shift)).astype(BF16)
        acc_ref[...] = jnp.zeros_like(acc_ref)

    h = h_ref[...]
    tf = wa_ref.shape[2]
    nsub = tf // sub
    sub0 = lax.broadcasted_iota(jnp.int32, (8, sub), 0) == 0

    def conv(y_ref, cw_ref, cs):
        H = FFN_HALO
        last = pltpu.roll(y_ref[H + tm - 8:H + tm, :], 1, axis=0)
        last2 = pltpu.roll(y_ref[H + tm - 16:H + tm - 8, :], 1, axis=0)
        m1 = jnp.where(sub0, y_ref[H - 1:H, :], last)
        m2 = jnp.where(sub0, y_ref[H - 2:H - 1, :], last2)
        y_ref[H - 8:H, :] = m1
        y_ref[H - 16:H - 8, :] = m2
        cw = cw_ref[0, :, cs]
        out = cw[FFN_CONV_WIDTH:FFN_CONV_WIDTH + 1, :]
        for k in range(FFN_CONV_WIDTH):
            off = H - 8 * (FFN_CONV_WIDTH - 1 - k)
            out = out + cw[k:k + 1, :] * y_ref[off:off + tm, :]
        return out

    for c in range(nsub):
        cs = slice(c * sub, (c + 1) * sub)
        ya_ref[c] = _dot(h, wa_ref[0, :, cs])
        yb_ref[c] = _dot(h, wb_ref[0, :, cs])
    for c in range(nsub):
        cs = slice(c * sub, (c + 1) * sub)
        act = (_silu(conv(ya_ref.at[c], cwa_ref, cs)) * conv(yb_ref.at[c], cwb_ref, cs)).astype(BF16)
        d = _dot(act, wd_ref[0, cs, :])
        for cb in range(acc_ref.shape[0]):
            acc_ref[cb] += d[:, cb * LANES:(cb + 1) * LANES]

    @pl.when(j == pl.num_programs(1) - 1)
    def _():
        for cb in range(acc_ref.shape[0]):
            cols = slice(cb * LANES, (cb + 1) * LANES)
            o_ref[:, cols] = x_ref[:, cols] + gate_ref[0, :, cols] * _from_tiles(acc_ref[cb])


def conv_ffn(x, g, sc, sh, gate, w_up, cw, w_down, layer, T, tm=512, tf=512, sub=256):
    N, D = x.shape
    DFF = w_down.shape[1]
    tm = min(tm, T)
    assert N % tm == 0 and T % tm == 0 and DFF % tf == 0 and tm % FFN_HALO == 0
    sub = min(sub, tf)
    nff = DFF // tf
    hb = tm // FFN_HALO
    bmap = lambda i, j: ((i * tm) // T, 0, 0)
    return pl.pallas_call(
        functools.partial(_ffn_kernel, blocks_per_seq=T // tm, chunk=min(128, tm), sub=min(sub, tf)),
        grid=(N // tm, nff),
        in_specs=[pl.BlockSpec((tm, D), lambda i, j: (i, 0)),
                  pl.BlockSpec((FFN_HALO, D), lambda i, j: (jnp.maximum(i * hb - 1, 0), 0)),
                  pl.BlockSpec((1, D), lambda i, j: (0, 0)),
                  pl.BlockSpec((1, 1, D), bmap),
                  pl.BlockSpec((1, 1, D), bmap),
                  pl.BlockSpec((1, 1, D), bmap),
                  pl.BlockSpec((1, D, tf), lambda i, j: (layer, 0, j)),
                  pl.BlockSpec((1, D, tf), lambda i, j: (layer, 0, j + nff)),
                  pl.BlockSpec((1, 8, tf), lambda i, j: (layer, 0, j)),
                  pl.BlockSpec((1, 8, tf), lambda i, j: (layer, 0, j + nff)),
                  pl.BlockSpec((1, tf, D), lambda i, j: (layer, j, 0))],
        out_specs=pl.BlockSpec((tm, D), lambda i, j: (i, 0)),
        out_shape=jax.ShapeDtypeStruct((N, D), F32),
        scratch_shapes=[pltpu.VMEM((tm + FFN_HALO, D), BF16), pltpu.VMEM((tm, D), F32),
                        pltpu.VMEM((D // LANES, tm, LANES), F32),
                        pltpu.VMEM((tf // sub, tm + FFN_HALO, sub), F32),
                        pltpu.VMEM((tf // sub, tm + FFN_HALO, sub), F32)],
        compiler_params=_params(("parallel", "arbitrary")),
        name="conv_ffn",
    )(x, x, g.reshape(1, D), sc, sh, gate, w_up, w_up, cw, cw, w_down)


KV_PLAIN = -1
KV_TRANSPOSED = -2
KV_TILE = 256
SEL_UNROLL = 2


def _rope(y, cos, sin):
    return y * cos + pltpu.roll(y, HEAD_DIM // 2, axis=1) * sin


def _head_norm(x, g):
    ms = jnp.mean(x * x, axis=-1, keepdims=True)
    return x * lax.rsqrt(ms + EPS) * g


def _prep_kernel(*refs, n_q_heads, kv_kinds, n_groups):
    nkv = len(kv_kinds)
    zq_ref = refs[0]
    kv_refs = refs[1:1 + nkv]
    cos_ref, sin_ref, qn_ref, kn_ref = refs[1 + nkv:5 + nkv]
    q_out = refs[5 + nkv]
    kv_out = refs[6 + nkv:]
    cos = cos_ref[0]
    sin = sin_ref[0]
    scale = HEAD_DIM ** -0.5 * LOG2E
    for hd in range(n_q_heads):
        sl = slice(hd * HEAD_DIM, (hd + 1) * HEAD_DIM)
        y = _rope(_head_norm(zq_ref[0, :, sl].astype(F32), qn_ref[...]), cos, sin)
        q_out[0, :, sl] = (y * scale).astype(BF16)
    for idx, kind in enumerate(kv_kinds):
        for gi in range(n_groups):
            sl = slice(gi * HEAD_DIM, (gi + 1) * HEAD_DIM)
            a = kv_refs[idx][0, :, sl]
            if kind >= 0:
                a = _rope(_head_norm(a.astype(F32), kn_ref[kind:kind + 1, :]), cos, sin)
            if kind == KV_TRANSPOSED:
                for c in range(a.shape[0] // KV_TILE):
                    kv_out[idx][0, gi, c] = a[c * KV_TILE:(c + 1) * KV_TILE, :].astype(F32).T.astype(BF16)
            else:
                kv_out[idx][0, gi] = a.astype(BF16)


def attn_prep(z3, cosf, sinf, q_norm, k_norm, q_col, kv_cols, kv_kinds, n_groups, tm=1024):
    B, T, _ = z3.shape
    tm = min(tm, T)
    QW = 8 * HEAD_DIM
    KW = n_groups * HEAD_DIM
    assert q_col % QW == 0 and all(c % KW == 0 for c in kv_cols) and T % tm == 0
    nkv = len(kv_cols)
    in_specs = [pl.BlockSpec((1, tm, QW), lambda b, t: (b, t, q_col // QW))]
    for c in kv_cols:
        in_specs.append(pl.BlockSpec((1, tm, KW), functools.partial(lambda b, t, cb: (b, t, cb), cb=c // KW)))
    in_specs += [pl.BlockSpec((1, tm, HEAD_DIM), lambda b, t: (b, t, 0))] * 2
    in_specs += [pl.BlockSpec((1, HEAD_DIM), lambda b, t: (0, 0)),
                 pl.BlockSpec(k_norm.shape, lambda b, t: (0, 0))]
    out_specs = [pl.BlockSpec((1, tm, QW), lambda b, t: (b, t, 0))]
    out_shape = [jax.ShapeDtypeStruct((B, T, QW), BF16)]
    for kind in kv_kinds:
        if kind == KV_TRANSPOSED:
            assert tm % KV_TILE == 0
            out_specs.append(pl.BlockSpec((1, n_groups, tm // KV_TILE, HEAD_DIM, KV_TILE),
                                          lambda b, t: (b, 0, t, 0, 0)))
            out_shape.append(jax.ShapeDtypeStruct((B, n_groups, T // KV_TILE, HEAD_DIM, KV_TILE), BF16))
        else:
            out_specs.append(pl.BlockSpec((1, n_groups, tm, HEAD_DIM), lambda b, t: (b, 0, t, 0)))
            out_shape.append(jax.ShapeDtypeStruct((B, n_groups, T, HEAD_DIM), BF16))
    return pl.pallas_call(
        functools.partial(_prep_kernel, n_q_heads=QW // HEAD_DIM, kv_kinds=tuple(kv_kinds), n_groups=n_groups),
        grid=(B, T // tm),
        in_specs=in_specs, out_specs=out_specs, out_shape=out_shape,
        compiler_params=_params(("parallel", "parallel")),
        name="attn_prep",
    )(*([z3] * (1 + nkv)), cosf, sinf, q_norm.reshape(1, HEAD_DIM), k_norm)


def _compress_kernel(ak_ref, av_ref, w1k_ref, w1v_ref, pk_ref, pv_ref, w2k_ref, w2v_ref,
                     kn_ref, cos_ref, sin_ref, kc_ref, vc_ref):
    hid = NSA_CMP_HIDDEN

    def mlp(a_ref, w1_ref, p_ref, w2_ref):
        P = _dot(a_ref[0, 0], w1_ref[...])
        Q = _dot(p_ref[...], w1_ref[...])
        pb = Q[0:1, :hid] + Q[1:2, hid:]
        nxt = pltpu.roll(P[:, hid:], P.shape[0] - 1, axis=0)
        hdn = _silu(P[:, :hid] + nxt + pb)
        return _dot(hdn.astype(BF16), w2_ref[...])

    kc = mlp(ak_ref, w1k_ref, pk_ref, w2k_ref)
    kc = _rope(_head_norm(kc, kn_ref[...]), cos_ref[0], sin_ref[0])
    kc_ref[0, 0] = kc.astype(BF16)
    vc_ref[0, 0] = mlp(av_ref, w1v_ref, pv_ref, w2v_ref).T.astype(BF16)


def nsa_compress(kcr, vcr, w1k, w1v, pk, pv, w2k, w2v, kn0, cos_end, sin_end):
    B, G, T, dh = kcr.shape
    nch = T // NSA_CMP_STRIDE
    cw = NSA_CMP_STRIDE * dh
    a_spec = pl.BlockSpec((1, 1, nch, cw), lambda b, g: (b, g, 0, 0))
    full = lambda arr: pl.BlockSpec(arr.shape, lambda b, g: (0,) * arr.ndim)
    tab_spec = pl.BlockSpec((1, nch, dh), lambda b, g: (b, 0, 0))
    o_spec = pl.BlockSpec((1, 1, nch, dh), lambda b, g: (b, g, 0, 0))
    return pl.pallas_call(
        _compress_kernel,
        grid=(B, G),
        in_specs=[a_spec, a_spec, full(w1k), full(w1v), full(pk), full(pv), full(w2k), full(w2v),
                  full(kn0), tab_spec, tab_spec],
        out_specs=[o_spec, pl.BlockSpec((1, 1, dh, nch), lambda b, g: (b, g, 0, 0))],
        out_shape=[jax.ShapeDtypeStruct((B, G, nch, dh), BF16), jax.ShapeDtypeStruct((B, G, dh, nch), BF16)],
        compiler_params=_params(("parallel", "parallel")),
        name="nsa_compress",
    )(kcr.reshape(B, G, nch, cw), vcr.reshape(B, G, nch, cw), w1k, w1v, pk, pv, w2k, w2v,
      kn0, cos_end, sin_end)


def _nsa_kernel(q_ref, kc_ref, vct_ref, ks_ref, vst_ref, kw_ref, vwt_ref, gl_ref, ov_ref, ext_ref,
                o_ref, m_ref, l_ref, acc_ref, *, n_rep, n_cmp, n_sel, n_top):
    tq = q_ref.shape[1]
    tk = KV_TILE
    nrow = kc_ref.shape[2]
    t0 = pl.program_id(2) * tq
    q4 = jnp.concatenate([q_ref[0, :, r * HEAD_DIM:(r + 1) * HEAD_DIM] for r in range(n_rep)], axis=0)

    def heads(a):
        return jnp.concatenate([a] * n_rep, axis=1)

    jb = lax.broadcasted_iota(jnp.int32, (nrow, tq), 0)
    tcol = t0 + lax.broadcasted_iota(jnp.int32, (nrow, tq), 1)
    valid_c = (jb * NSA_CMP_STRIDE + (NSA_CMP_LEN - 1) <= tcol) & (jb < n_cmp)
    validf = heads(jnp.where(valid_c, 1.0, 0.0))
    s = _dot_nt(kc_ref[0, 0], q4) + heads(jnp.where(valid_c, 0.0, NEG))
    e = jnp.exp2(s - jnp.max(s, axis=0, keepdims=True))
    p = e / jnp.sum(e, axis=0, keepdims=True) * validf
    o_cmp = _dot(vct_ref[0, 0], p.astype(BF16))
    psum = p[:, 0:tq]
    for r in range(1, n_rep):
        psum = psum + p[:, r * tq:(r + 1) * tq]

    hi = psum.astype(BF16)
    lo = (psum - hi.astype(F32)).astype(BF16)
    imp = _dot(ov_ref[...], hi) + _dot(ov_ref[...], lo)
    nsr = ov_ref.shape[0]
    jb = lax.broadcasted_iota(jnp.int32, (nsr, tq), 0)
    tcol = t0 + lax.broadcasted_iota(jnp.int32, (nsr, tq), 1)
    cur = lax.shift_right_logical(tcol, int(np.log2(NSA_SEL_LEN)))
    forced = (jb == 0) | (jb == cur) | (jb == cur - 1)
    score = jnp.where(forced, NSA_FORCE, jnp.where(jb * NSA_SEL_LEN <= tcol, imp, -1.0))
    score = jnp.where(jb < n_sel, score, -2.0)
    selT = jnp.zeros((nsr, tq), F32)
    for _ in range(n_top):
        best = jnp.max(score, axis=0, keepdims=True)
        idx = jnp.min(jnp.where(score == best, jb, nsr), axis=0, keepdims=True)
        hit = jb == idx
        selT = jnp.where(hit, 1.0, selT)
        score = jnp.where(hit, -3.0, score)
    sel = selT.astype(BF16)

    m_ref[...] = jnp.full(m_ref.shape, NEG, F32)
    l_ref[...] = jnp.zeros(l_ref.shape, F32)
    acc_ref[...] = jnp.zeros(acc_ref.shape, F32)
    krow = lax.broadcasted_iota(jnp.int32, (tk, tq), 0)
    tq_col = t0 + lax.broadcasted_iota(jnp.int32, (tk, tq), 1)

    def sel_body(it, carry):
        s = []
        for u in range(SEL_UNROLL):
            kt = it * SEL_UNROLL + u
            k0 = pl.multiple_of(kt * tk, tk)
            mask = (_dot(ext_ref[kt], sel) > 0.5) & (k0 + krow <= tq_col)
            s.append(_dot_nt(ks_ref[0, 0, pl.ds(k0, tk), :], q4) + heads(jnp.where(mask, 0.0, NEG)))
        m_old = m_ref[...]
        m_new = m_old
        for u in range(SEL_UNROLL):
            m_new = jnp.maximum(m_new, jnp.max(s[u], axis=0, keepdims=True))
        alpha = jnp.exp2(m_old - m_new)
        lsum = alpha * l_ref[...]
        pv = alpha * acc_ref[...]
        for u in range(SEL_UNROLL):
            p = jnp.exp2(s[u] - m_new)
            lsum = lsum + jnp.sum(p, axis=0, keepdims=True)
            pv = pv + _dot(vst_ref[0, 0, it * SEL_UNROLL + u], p.astype(BF16))
        l_ref[...] = lsum
        acc_ref[...] = pv
        m_ref[...] = m_new
        return carry

    n_tiles = (t0 + tq) // tk
    lax.fori_loop(0, (n_tiles + SEL_UNROLL - 1) // SEL_UNROLL, sel_body, 0)
    o_sel = acc_ref[...] / l_ref[...]

    W = NSA_WINDOW
    n_win = (W + tq) // tk
    kstart = pl.multiple_of(jnp.maximum(t0 - W, 0), tk)
    jt = kstart // tk
    sw = []
    for c in range(n_win):
        rel = tq_col - (kstart + c * tk) - krow
        bias = heads(jnp.where((rel >= 0) & (rel < W), 0.0, NEG))
        sw.append(_dot_nt(kw_ref[0, 0, pl.ds(kstart + c * tk, tk), :], q4) + bias)
    m = jnp.max(sw[0], axis=0, keepdims=True)
    for c in range(1, n_win):
        m = jnp.maximum(m, jnp.max(sw[c], axis=0, keepdims=True))
    den = jnp.zeros_like(m)
    o_win = jnp.zeros((HEAD_DIM, n_rep * tq), F32)
    for c in range(n_win):
        ew = jnp.exp2(sw[c] - m)
        den = den + jnp.sum(ew, axis=0, keepdims=True)
        o_win = o_win + _dot(vwt_ref[0, 0, jt + c], ew.astype(BF16))
    o_win = o_win / den

    gT = _sigmoid(gl_ref[0]).T
    for r in range(n_rep):
        hs = slice(r * tq, (r + 1) * tq)
        c3 = 3 * r
        oT = (gT[c3:c3 + 1, :] * o_cmp[:, hs] + gT[c3 + 1:c3 + 2, :] * o_sel[:, hs]
              + gT[c3 + 2:c3 + 3, :] * o_win[:, hs])
        o_ref[0, :, r * HEAD_DIM:(r + 1) * HEAD_DIM] = oT.T.astype(BF16)


def nsa_attention(qn, kc, vct, ks, vst, kw, vwt, gl3, tq=512):
    B, T, QW = qn.shape
    G = kc.shape[1]
    n_rep = QW // HEAD_DIM // G
    nrow = kc.shape[2]
    tk = KV_TILE
    n_cmp = (T - NSA_CMP_LEN) // NSA_CMP_STRIDE + 1
    n_sel = T // NSA_SEL_LEN
    assert T % tq == 0 and n_sel <= nrow and T >= NSA_WINDOW + tq and tq % tk == 0 and NSA_WINDOW % tk == 0
    assert (tq // tk) % SEL_UNROLL == 0
    starts = np.arange(nrow) * NSA_CMP_STRIDE
    sel_start = np.arange(nrow) * NSA_SEL_LEN
    nsr = -(-n_sel // BF16_SUBLANES) * BF16_SUBLANES
    ov = ((starts[None, :] <= sel_start[:, None] + NSA_SEL_LEN - 1)
          & (starts[None, :] + NSA_CMP_LEN - 1 >= sel_start[:, None])
          & (np.arange(nrow)[:, None] < n_sel) & (np.arange(nrow)[None, :] < n_cmp))[:nsr]
    ext = (np.arange(T)[:, None] // NSA_SEL_LEN == np.arange(nsr)[None, :]).reshape(T // tk, tk, nsr)
    k_spec = pl.BlockSpec((1, 1, T, HEAD_DIM), lambda b, g, i: (b, g, 0, 0))
    vt_spec = pl.BlockSpec((1, 1, T // tk, HEAD_DIM, tk), lambda b, g, i: (b, g, 0, 0, 0))
    c_spec = pl.BlockSpec((1, 1, nrow, HEAD_DIM), lambda b, g, i: (b, g, 0, 0))
    hw = n_rep * HEAD_DIM
    return pl.pallas_call(
        functools.partial(_nsa_kernel, n_rep=n_rep, n_cmp=n_cmp, n_sel=n_sel, n_top=min(NSA_TOPK, n_sel)),
        grid=(B, G, T // tq),
        in_specs=[pl.BlockSpec((1, tq, hw), lambda b, g, i: (b, i, g)),
                  c_spec, pl.BlockSpec((1, 1, HEAD_DIM, nrow), lambda b, g, i: (b, g, 0, 0)),
                  k_spec, vt_spec, k_spec, vt_spec,
                  pl.BlockSpec((1, tq, LANES), lambda b, g, i: (b, i, g)),
                  pl.BlockSpec((nsr, nrow), lambda b, g, i: (0, 0)),
                  pl.BlockSpec((T // tk, tk, nsr), lambda b, g, i: (0, 0, 0))],
        out_specs=pl.BlockSpec((1, tq, hw), lambda b, g, i: (b, i, g)),
        out_shape=jax.ShapeDtypeStruct((B, T, QW), BF16),
        scratch_shapes=[pltpu.VMEM((1, n_rep * tq), F32), pltpu.VMEM((1, n_rep * tq), F32),
                        pltpu.VMEM((HEAD_DIM, n_rep * tq), F32)],
        compiler_params=_params(("parallel", "parallel", "arbitrary")),
        name="nsa_attention",
    )(qn, kc, vct, ks, vst, kw, vwt, gl3, jnp.asarray(ov, BF16), jnp.asarray(ext, BF16))


def _swa_kernel(sink_ref, q_ref, k_ref, vt_ref, o_ref, *, n_rep):
    tq = q_ref.shape[1]
    tk = KV_TILE
    W = SWA_WINDOW
    n_win = (tk + tq) // tk
    g = pl.program_id(1)
    t0 = pl.program_id(2) * tq
    q4 = jnp.concatenate([q_ref[0, :, r * HEAD_DIM:(r + 1) * HEAD_DIM] for r in range(n_rep)], axis=0)
    sink = jnp.concatenate([jnp.full((1, tq), sink_ref[g * n_rep + r] * LOG2E, F32) for r in range(n_rep)], axis=1)
    kstart = pl.multiple_of(jnp.maximum(t0 - tk, 0), tk)
    jt = kstart // tk
    krow = lax.broadcasted_iota(jnp.int32, (tk, tq), 0)
    tq_col = t0 + lax.broadcasted_iota(jnp.int32, (tk, tq), 1)
    s = []
    for c in range(n_win):
        rel = tq_col - (kstart + c * tk) - krow
        bias = jnp.concatenate([jnp.where((rel >= 0) & (rel < W), 0.0, NEG)] * n_rep, axis=1)
        s.append(_dot_nt(k_ref[0, 0, pl.ds(kstart + c * tk, tk), :], q4) + bias)
    m = sink
    for c in range(n_win):
        m = jnp.maximum(m, jnp.max(s[c], axis=0, keepdims=True))
    den = jnp.exp2(sink - m)
    oT = jnp.zeros((HEAD_DIM, n_rep * tq), F32)
    for c in range(n_win):
        e = jnp.exp2(s[c] - m)
        den = den + jnp.sum(e, axis=0, keepdims=True)
        oT = oT + _dot(vt_ref[0, 0, jt + c], e.astype(BF16))
    oT = oT / den
    for r in range(n_rep):
        o_ref[0, :, r * HEAD_DIM:(r + 1) * HEAD_DIM] = oT[:, r * tq:(r + 1) * tq].T.astype(BF16)


def swa_attention(qn, kn, vt, sinks, tq=256):
    B, T, QW = qn.shape
    G = kn.shape[1]
    n_rep = QW // HEAD_DIM // G
    tk = KV_TILE
    assert T % tq == 0 and tq % tk == 0 and SWA_WINDOW <= tk and T >= tk + tq
    hw = n_rep * HEAD_DIM
    return pl.pallas_call(
        functools.partial(_swa_kernel, n_rep=n_rep),
        grid=(B, G, T // tq),
        in_specs=[pl.BlockSpec(memory_space=pltpu.SMEM),
                  pl.BlockSpec((1, tq, hw), lambda b, g, i: (b, i, g)),
                  pl.BlockSpec((1, 1, T, HEAD_DIM), lambda b, g, i: (b, g, 0, 0)),
                  pl.BlockSpec((1, 1, T // tk, HEAD_DIM, tk), lambda b, g, i: (b, g, 0, 0, 0))],
        out_specs=pl.BlockSpec((1, tq, hw), lambda b, g, i: (b, i, g)),
        out_shape=jax.ShapeDtypeStruct((B, T, QW), BF16),
        compiler_params=_params(("parallel", "parallel", "parallel")),
        name="swa_attention",
    )(sinks, qn, kn, vt)


CONF_HALO = 32


def _conformer_kernel(a1_ref, a2_ref, h1_ref, h2_ref, w_ref, b_ref, g_ref, be_ref, o_ref,
                      glu_ref, conv_ref, *, blocks_per_seq):
    tm, C = a1_ref.shape
    PRE = CONF_HALO
    nctx = CONV_WIDTH - 1
    first = (pl.program_id(0) % blocks_per_seq) == 0

    gh = jnp.where(first, 0.0, h1_ref[...].astype(F32) * _sigmoid(h2_ref[...].astype(F32)))
    sub0 = lax.broadcasted_iota(jnp.int32, (8 * nctx, LANES), 0) % 8 == 0
    for cs in range(C // LANES):
        sl = slice(cs * LANES, (cs + 1) * LANES)
        g = _to_tiles(a1_ref[:, sl].astype(F32) * _sigmoid(a2_ref[:, sl].astype(F32)))
        glu_ref[cs, 8 * PRE:, :] = g
        moved = pltpu.roll(g[tm - 8 * nctx:, :], 1, axis=0)
        halo = jnp.concatenate([jnp.broadcast_to(gh[r:r + 1, sl], (8, LANES)) for r in range(PRE - nctx, PRE)],
                               axis=0)
        glu_ref[cs, 8 * (PRE - nctx):8 * PRE, :] = jnp.where(sub0, halo, moved)

    rb = 128

    def conv_group(gi, carry):
        r0 = pl.multiple_of(gi * rb, rb)
        for cs in range(C // LANES):
            sl = slice(cs * LANES, (cs + 1) * LANES)
            acc = jnp.broadcast_to(b_ref[:, sl], (rb, LANES))
            for k in range(CONV_WIDTH):
                off = 8 * (PRE - nctx + k)
                acc = acc + w_ref[k:k + 1, sl] * glu_ref[cs, pl.ds(r0 + off, rb), :]
            conv_ref[cs, pl.ds(r0, rb), :] = acc
        return carry

    lax.fori_loop(0, tm // rb, conv_group, 0)

    ns = C // LANES
    for r0 in range(0, tm, rb):
        y = [conv_ref[cs, r0:r0 + rb, :] for cs in range(ns)]
        mu = sum(jnp.sum(v, axis=-1, keepdims=True) for v in y) / C
        d = [v - mu for v in y]
        var = sum(jnp.sum(v * v, axis=-1, keepdims=True) for v in d) / C
        rs = lax.rsqrt(var + EPS)
        for cs in range(ns):
            sl = slice(cs * LANES, (cs + 1) * LANES)
            conv_ref[cs, r0:r0 + rb, :] = _silu(d[cs] * rs * g_ref[:, sl] + be_ref[:, sl])
    for cs in range(ns):
        o_ref[:, cs * LANES:(cs + 1) * LANES] = _from_tiles(conv_ref[cs]).astype(BF16)


def conformer_conv(z, conv_w, conv_b, ln_g, ln_b, T, tm=512):
    N = z.shape[0]
    C = conv_w.shape[1]
    tm = min(tm, T)
    assert N % tm == 0 and T % tm == 0 and tm % CONF_HALO == 0 and CONV_WIDTH - 1 <= min(CONF_HALO, tm // 8)
    hb = tm // CONF_HALO
    w = jnp.zeros((CONF_HALO, C), F32).at[:CONV_WIDTH].set(conv_w)
    hmap = lambda c: (lambda i: (jnp.maximum(i * hb - 1, 0), c))
    vec = pl.BlockSpec((1, C), lambda i: (0, 0))
    return pl.pallas_call(
        functools.partial(_conformer_kernel, blocks_per_seq=T // tm),
        grid=(N // tm,),
        in_specs=[pl.BlockSpec((tm, C), lambda i: (i, 0)), pl.BlockSpec((tm, C), lambda i: (i, 1)),
                  pl.BlockSpec((CONF_HALO, C), hmap(0)), pl.BlockSpec((CONF_HALO, C), hmap(1)),
                  pl.BlockSpec((CONF_HALO, C), lambda i: (0, 0)), vec, vec, vec],
        out_specs=pl.BlockSpec((tm, C), lambda i: (i, 0)),
        out_shape=jax.ShapeDtypeStruct((N, C), BF16),
        scratch_shapes=[pltpu.VMEM((C // LANES, tm + 8 * CONF_HALO, LANES), F32),
                        pltpu.VMEM((C // LANES, tm, LANES), F32)],
        compiler_params=_params(("parallel",)),
        name="conformer_conv",
    )(z, z, z, z, w, conv_b.reshape(1, C), ln_g.reshape(1, C), ln_b.reshape(1, C))


def _gmlp_kernel(u0_ref, u1_ref, v0_ref, v1_ref, g_ref, be_ref, ws_ref, bs_ref, o_ref, *, n_groups):
    tm, hw = v0_ref.shape
    C = 2 * hw
    ch = GMLP_CHUNK
    gv = [_gelu_tanh(v0_ref[...].astype(F32)), _gelu_tanh(v1_ref[...].astype(F32))]
    mu = (jnp.sum(gv[0], axis=-1, keepdims=True) + jnp.sum(gv[1], axis=-1, keepdims=True)) / C
    d = [gv[0] - mu, gv[1] - mu]
    var = (jnp.sum(d[0] * d[0], axis=-1, keepdims=True) + jnp.sum(d[1] * d[1], axis=-1, keepdims=True)) / C
    rs = lax.rsqrt(var + EPS)
    vn = [(d[hf] * rs * g_ref[:, hf * hw:(hf + 1) * hw] + be_ref[:, hf * hw:(hf + 1) * hw]).astype(BF16)
          for hf in range(2)]
    u_refs = [u0_ref, u1_ref]
    tril = lax.broadcasted_iota(jnp.int32, (ch, ch), 0) >= lax.broadcasted_iota(jnp.int32, (ch, ch), 1)
    gph = n_groups // 2
    for gi in range(n_groups):
        hf, col = gi // gph, (gi % gph) * HEAD_DIM
        w = jnp.where(tril, ws_ref[gi], 0.0).astype(BF16)
        for c in range(tm // ch):
            rows = slice(c * ch, (c + 1) * ch)
            sp = _dot(w, vn[hf][rows, col:col + HEAD_DIM]) + bs_ref[gi]
            u = _gelu_tanh(u_refs[hf][rows, col:col + HEAD_DIM].astype(F32))
            o_ref[rows, gi * HEAD_DIM:(gi + 1) * HEAD_DIM] = (u * sp).astype(BF16)


def chunked_gmlp(z, u_col, ln_g, ln_b, w_s, b_s, T, tm=1024):
    N = z.shape[0]
    n_groups, ch, _ = w_s.shape
    C = n_groups * HEAD_DIM
    hw = C // 2
    tm = min(tm, T)
    assert N % tm == 0 and T % tm == 0 and tm % ch == 0 and u_col % hw == 0
    cb = u_col // hw
    bsb = jnp.broadcast_to(b_s[:, :, None], (n_groups, ch, HEAD_DIM))
    zspec = lambda k: pl.BlockSpec((tm, hw), lambda i: (i, cb + k))
    vec = pl.BlockSpec((1, C), lambda i: (0, 0))
    return pl.pallas_call(
        functools.partial(_gmlp_kernel, n_groups=n_groups),
        grid=(N // tm,),
        in_specs=[zspec(0), zspec(1), zspec(2), zspec(3), vec, vec,
                  pl.BlockSpec((n_groups, ch, ch), lambda i: (0, 0, 0)),
                  pl.BlockSpec((n_groups, ch, HEAD_DIM), lambda i: (0, 0, 0))],
        out_specs=pl.BlockSpec((tm, C), lambda i: (i, 0)),
        out_shape=jax.ShapeDtypeStruct((N, C), BF16),
        compiler_params=_params(("parallel",)),
        name="chunked_gmlp",
    )(z, z, z, z, ln_g.reshape(1, C), ln_b.reshape(1, C), w_s, bsb)


def _even_mixer(xf, B, T, g, sc, sh, cosf, sinf, w_in, w_out, conv_w, conv_b, conv_ln_g, conv_ln_b,
                q_norm, k_norm, cmp_k_pos, cmp_k_w1, cmp_k_w2, cmp_v_pos, cmp_v_w1, cmp_v_w2):
    D = xf.shape[1]
    G = NSA_KV_GROUPS
    C = conv_w.shape[1]
    QW = D // 2
    KW = G * HEAD_DIM
    n_rep = QW // HEAD_DIM // G
    main = 2 * C + QW + 6 * KW
    wg = w_in[:, main:].reshape(D, G, n_rep * 3)
    wg = jnp.pad(wg, ((0, 0), (0, 0), (0, LANES - n_rep * 3))).reshape(D, G * LANES)
    z, gl = in_proj(xf, g, sc, sh, w_in.astype(BF16), main, wg.astype(BF16), T)
    ya = conformer_conv(z, conv_w, conv_b, conv_ln_g, conv_ln_b, T)
    q_col = 2 * C
    kv_cols = [q_col + QW + k * KW for k in range(6)]
    qn, kcr, vcr, ks, vs, kw, vw = attn_prep(z.reshape(B, T, main), cosf, sinf, q_norm, k_norm,
                                             q_col, kv_cols,
                                             (KV_PLAIN, KV_PLAIN, 1, KV_TRANSPOSED, 2, KV_TRANSPOSED), G)
    half = NSA_CMP_STRIDE * HEAD_DIM
    hid = NSA_CMP_HIDDEN
    nch = T // NSA_CMP_STRIDE

    def w1cat(w1):
        return jnp.concatenate([w1[:half], w1[half:]], axis=1).astype(BF16)

    def posrows(pos):
        return jnp.zeros((8, half), F32).at[0].set(pos[:NSA_CMP_STRIDE].reshape(half)) \
                  .at[1].set(pos[NSA_CMP_STRIDE:].reshape(half)).astype(BF16)

    def at_block_ends(tab):
        e = tab[:, NSA_CMP_LEN - 1::NSA_CMP_STRIDE]
        return jnp.pad(e, ((0, 0), (0, nch - e.shape[1]), (0, 0)))

    kc, vc = nsa_compress(kcr, vcr, w1cat(cmp_k_w1), w1cat(cmp_v_w1), posrows(cmp_k_pos), posrows(cmp_v_pos),
                          cmp_k_w2.astype(BF16), cmp_v_w2.astype(BF16), k_norm[0:1],
                          at_block_ends(cosf), at_block_ends(sinf))
    yb = nsa_attention(qn, kc, vc, ks, vs, kw, vw, gl.reshape(B, T, G * LANES))
    return ya, yb.reshape(B * T, QW), w_out.astype(BF16)


def _odd_mixer(xf, B, T, g, sc, sh, cosf, sinf, w_in, w_out, q_norm, k_norm, sinks, ln_g, ln_b, w_s, b_s):
    D = xf.shape[1]
    G = SWA_KV_HEADS
    QW = D // 2
    KW = G * HEAD_DIM
    z, = in_proj(xf, g, sc, sh, w_in.astype(BF16), w_in.shape[1], None, T)
    qn, kn, v = attn_prep(z.reshape(B, T, z.shape[1]), cosf, sinf, q_norm, k_norm.reshape(1, HEAD_DIM),
                          0, [QW, QW + KW], (0, KV_TRANSPOSED), G)
    yc = swa_attention(qn, kn, v, sinks)
    yd = chunked_gmlp(z, QW + 2 * KW, ln_g, ln_b, w_s, b_s, T)
    return yc.reshape(B * T, QW), yd, w_out.astype(BF16)


def kernel(x, c, positions, ada_w, ada_b, norm_g, ffn_w_up, ffn_conv_w, ffn_conv_b, ffn_w_down, ev_w_in, ev_w_out, ev_conv_w, ev_conv_b, ev_conv_ln_g, ev_conv_ln_b, ev_q_norm, ev_k_norm, ev_cmp_k_pos, ev_cmp_k_w1, ev_cmp_k_w2, ev_cmp_v_pos, ev_cmp_v_w1, ev_cmp_v_w2, od_w_in, od_w_out, od_q_norm, od_k_norm, od_sinks, od_gmlp_ln_g, od_gmlp_ln_b, od_gmlp_w_s, od_gmlp_b_s):
    B, T, D = x.shape
    depth = ada_w.shape[0]
    cosf, sinf = rope_tables(positions)
    mod = adaln(c, ada_w, ada_b)
    xf = x.reshape(B * T, D)
    w_up_b = ffn_w_up.astype(BF16)
    w_down_b = ffn_w_down.astype(BF16)
    cw_all = jnp.zeros((depth, 8, ffn_conv_w.shape[2]), F32).at[:, :FFN_CONV_WIDTH].set(ffn_conv_w) \
                .at[:, FFN_CONV_WIDTH].set(ffn_conv_b)
    for i in range(depth):
        sh1, sc1, g1, sh2, sc2, g2 = [m.reshape(B, 1, D) for m in jnp.split(mod[i], 6, axis=-1)]
        j = i // 2
        if i % 2 == 0:
            ya, yb, w_out = _even_mixer(xf, B, T, norm_g[i, 0], sc1, sh1, cosf, sinf, ev_w_in[j], ev_w_out[j],
                                        ev_conv_w[j], ev_conv_b[j], ev_conv_ln_g[j], ev_conv_ln_b[j],
                                        ev_q_norm[j], ev_k_norm[j], ev_cmp_k_pos[j], ev_cmp_k_w1[j],
                                        ev_cmp_k_w2[j], ev_cmp_v_pos[j], ev_cmp_v_w1[j], ev_cmp_v_w2[j])
        else:
            ya, yb, w_out = _odd_mixer(xf, B, T, norm_g[i, 0], sc1, sh1, cosf, sinf, od_w_in[j], od_w_out[j],
                                       od_q_norm[j], od_k_norm[j], od_sinks[j], od_gmlp_ln_g[j],
                                       od_gmlp_ln_b[j], od_gmlp_w_s[j], od_gmlp_b_s[j])
        xf = out_proj(ya, yb, w_out, xf, g1, T)
        xf = conv_ffn(xf, norm_g[i, 1], sc2, sh2, g2, w_up_b, cw_all, w_down_b, i, T)
    return xf.reshape(B, T, D)
```

```python
import functools

import numpy as np
import jax
import jax.numpy as jnp
from jax import lax
from jax.experimental import pallas as pl
from jax.experimental.pallas import tpu as pltpu

F32 = jnp.float32
BF16 = jnp.bfloat16

HEAD_DIM = 128
ROPE_THETA = 10000.0
EPS = 1e-6
NEG = -1e30
LOG2E = 1.4426950408889634

CONV_WIDTH = 31
NSA_KV_GROUPS = 2
NSA_CMP_LEN = 32
NSA_CMP_STRIDE = 16
NSA_CMP_HIDDEN = 256
NSA_SEL_LEN = 64
NSA_TOPK = 8
NSA_WINDOW = 256
NSA_FORCE = 1e6
SWA_KV_HEADS = 2
SWA_WINDOW = 128
GMLP_CHUNK = 128
FFN_CONV_WIDTH = 3

V7X_VMEM_BYTES = 64 * 1024 * 1024
VMEM_LIMIT = V7X_VMEM_BYTES - 8 * 1024 * 1024
LANES = 128
BF16_SUBLANES = 16


def _params(sem):
    return pltpu.CompilerParams(dimension_semantics=sem, vmem_limit_bytes=VMEM_LIMIT)


def _dot(a, b):
    return jnp.dot(a, b, preferred_element_type=F32)


def _dot_nt(a, b):
    return lax.dot_general(a, b, (((1,), (1,)), ((), ())), preferred_element_type=F32)


def _sigmoid(x):
    return 1.0 / (1.0 + jnp.exp(-x))


def _silu(x):
    return x * _sigmoid(x)


def _gelu_tanh(x):
    return 0.5 * x * (1.0 + jnp.tanh(np.sqrt(2.0 / np.pi).astype(np.float32) * (x + 0.044715 * (x * x * x))))


def _rope_kernel(pos_ref, inv_ref, cos_ref, sin_ref):
    ang = pos_ref[0].astype(F32) * inv_ref[0:1, :]
    cos_ref[0] = jnp.cos(ang)
    sin_ref[0] = jnp.sin(ang) * inv_ref[1:2, :]


def rope_tables(positions):
    B, T = positions.shape
    inv = ROPE_THETA ** (-jnp.arange(0, HEAD_DIM, 2, dtype=F32) / HEAD_DIM)
    half = HEAD_DIM // 2
    sign = jnp.concatenate([-jnp.ones((half,), F32), jnp.ones((half,), F32)])
    tab = jnp.stack([jnp.concatenate([inv, inv]), sign])
    return pl.pallas_call(
        _rope_kernel,
        grid=(B,),
        in_specs=[pl.BlockSpec((1, T, 1), lambda b: (b, 0, 0)),
                  pl.BlockSpec((2, HEAD_DIM), lambda b: (0, 0))],
        out_specs=[pl.BlockSpec((1, T, HEAD_DIM), lambda b: (b, 0, 0))] * 2,
        out_shape=[jax.ShapeDtypeStruct((B, T, HEAD_DIM), F32)] * 2,
        compiler_params=_params(("parallel",)),
        name="rope_tables",
    )(positions.reshape(B, T, 1), tab)


def _adaln_kernel(c_ref, w_ref, b_ref, o_ref):
    ca = _silu(c_ref[...]).astype(BF16)
    o_ref[0] = _dot(ca, w_ref[0].astype(BF16)) + b_ref[0]


def adaln(c, ada_w, ada_b, tn=2048):
    L, D, N6 = ada_w.shape
    B = c.shape[0]
    return pl.pallas_call(
        _adaln_kernel,
        grid=(L, N6 // tn),
        in_specs=[pl.BlockSpec((B, D), lambda l, j: (0, 0)),
                  pl.BlockSpec((1, D, tn), lambda l, j: (l, 0, j)),
                  pl.BlockSpec((1, 1, tn), lambda l, j: (l, 0, j))],
        out_specs=pl.BlockSpec((1, B, tn), lambda l, j: (l, 0, j)),
        out_shape=jax.ShapeDtypeStruct((L, B, N6), F32),
        compiler_params=_params(("parallel", "parallel")),
        name="adaln",
    )(c, ada_w, ada_b.reshape(L, 1, N6))


def _norm_mod_rows(x_ref, g, scale1, shift, out_ref, out_row0, nrows, chunk):
    def body(ci, carry):
        r = pl.multiple_of(ci * chunk, chunk)
        x = x_ref[pl.ds(r, chunk), :]
        ms = jnp.mean(x * x, axis=-1, keepdims=True)
        y = x * lax.rsqrt(ms + EPS) * g
        out_ref[pl.ds(out_row0 + r, chunk), :] = (y * scale1 + shift).astype(BF16)
        return carry
    lax.fori_loop(0, nrows // chunk, body, 0)


def _inproj_kernel(x_ref, g_ref, sc_ref, sh_ref, scn_ref, shn_ref, w_ref, *rest, has_gate, rows_per, n_chunks):
    if has_gate:
        wg_ref, o_ref, og_ref, ha_ref, hb_ref = rest
    else:
        o_ref, ha_ref, hb_ref = rest
    i = pl.program_id(0)
    j = pl.program_id(1)
    tm = x_ref.shape[0]
    g = g_ref[...]

    def norm_rows(r0, nrows, dst_ref, s_ref, t_ref):
        x = x_ref[pl.ds(r0, nrows), :]
        ms = jnp.mean(x * x, axis=-1, keepdims=True)
        y = x * lax.rsqrt(ms + EPS) * g
        dst_ref[pl.ds(r0, nrows), :] = (y * (1.0 + s_ref[0]) + t_ref[0]).astype(BF16)

    @pl.when((i == 0) & (j == 0))
    def _():
        def body(ci, carry):
            norm_rows(pl.multiple_of(ci * rows_per, BF16_SUBLANES), rows_per, ha_ref, sc_ref, sh_ref)
            return carry
        lax.fori_loop(0, tm // rows_per, body, 0)
        if tm % rows_per:
            norm_rows(tm - rows_per, rows_per, ha_ref, sc_ref, sh_ref)

    c = jnp.clip(j - 1, 0, n_chunks - 1)
    r0 = pl.multiple_of(jnp.minimum(c * rows_per, tm - rows_per), BF16_SUBLANES)

    def step(cur_ref, nxt_ref):
        if has_gate:
            @pl.when(j == 0)
            def _():
                og_ref[...] = _dot(cur_ref[...], wg_ref[...])
        o_ref[...] = _dot(cur_ref[...], w_ref[...]).astype(o_ref.dtype)
        norm_rows(r0, rows_per, nxt_ref, scn_ref, shn_ref)

    @pl.when(i % 2 == 0)
    def _():
        step(ha_ref, hb_ref)

    @pl.when(i % 2 == 1)
    def _():
        step(hb_ref, ha_ref)


IN_PROJ_TN_CAP = 1792


def in_proj(x, g, sc, sh, w, n_out, wg, T, tm=1024):
    N, D = x.shape
    Nout = n_out
    tm = min(tm, T)
    tn = max(t for t in range(256, IN_PROJ_TN_CAP + 1, 256) if Nout % t == 0 and Nout // t >= 2)
    nb, nj = N // tm, Nout // tn
    assert N % tm == 0 and T % tm == 0 and Nout % tn == 0 and nj >= 2
    n_chunks = nj - 1
    rows_per = -(-tm // n_chunks)
    rows_per = -(-rows_per // BF16_SUBLANES) * BF16_SUBLANES
    nxt = lambda i: jnp.minimum(i + 1, nb - 1)
    xmap = lambda i, j: (jnp.where((i == 0) & (j == 0), 0, nxt(i)), 0)
    bmap = lambda i, j: ((i * tm) // T, 0, 0)
    nmap = lambda i, j: ((nxt(i) * tm) // T, 0, 0)
    in_specs = [pl.BlockSpec((tm, D), xmap),
                pl.BlockSpec((1, D), lambda i, j: (0, 0)),
                pl.BlockSpec((1, 1, D), bmap), pl.BlockSpec((1, 1, D), bmap),
                pl.BlockSpec((1, 1, D), nmap), pl.BlockSpec((1, 1, D), nmap),
                pl.BlockSpec((D, tn), lambda i, j: (0, j))]
    out_specs = [pl.BlockSpec((tm, tn), lambda i, j: (i, j))]
    out_shape = [jax.ShapeDtypeStruct((N, Nout), BF16)]
    args = [x, g.reshape(1, D), sc, sh, sc, sh, w]
    if wg is not None:
        ng = wg.shape[1]
        in_specs.append(pl.BlockSpec((D, ng), lambda i, j: (0, 0)))
        out_specs.append(pl.BlockSpec((tm, ng), lambda i, j: (i, 0)))
        out_shape.append(jax.ShapeDtypeStruct((N, ng), F32))
        args.append(wg)
    return pl.pallas_call(
        functools.partial(_inproj_kernel, has_gate=wg is not None, rows_per=rows_per, n_chunks=n_chunks),
        grid=(nb, nj),
        in_specs=in_specs, out_specs=out_specs, out_shape=out_shape,
        scratch_shapes=[pltpu.VMEM((tm, D), BF16), pltpu.VMEM((tm, D), BF16)],
        compiler_params=_params(("arbitrary", "arbitrary")),
        name="in_proj",
    )(*args)


def _outproj_kernel(ya_ref, yb_ref, wa_ref, wb_ref, x_ref, gate_ref, o_ref):
    y = _dot(ya_ref[...], wa_ref[...]) + _dot(yb_ref[...], wb_ref[...])
    o_ref[...] = x_ref[...] + gate_ref[0] * y


def out_proj(ya, yb, w_out, x, gate, T, tm=512, tn=2048):
    N, D = x.shape
    Ka, Kb = ya.shape[1], yb.shape[1]
    tm = min(tm, T)
    assert N % tm == 0 and T % tm == 0 and D % tn == 0 and Ka % Kb == 0
    return pl.pallas_call(
        _outproj_kernel,
        grid=(N // tm, D // tn),
        in_specs=[pl.BlockSpec((tm, Ka), lambda i, j: (i, 0)),
                  pl.BlockSpec((tm, Kb), lambda i, j: (i, 0)),
                  pl.BlockSpec((Ka, tn), lambda i, j: (0, j)),
                  pl.BlockSpec((Kb, tn), lambda i, j: (Ka // Kb, j)),
                  pl.BlockSpec((tm, tn), lambda i, j: (i, j)),
                  pl.BlockSpec((1, 1, tn), lambda i, j: ((i * tm) // T, 0, j))],
        out_specs=pl.BlockSpec((tm, tn), lambda i, j: (i, j)),
        out_shape=jax.ShapeDtypeStruct((N, D), F32),
        compiler_params=_params(("parallel", "parallel")),
        name="out_proj",
    )(ya, yb, w_out, w_out, x, gate)


FFN_HALO = BF16_SUBLANES


def _to_tiles(a):
    rows = a.shape[0]
    return jnp.swapaxes(a.reshape(8, rows // 8, LANES), 0, 1).reshape(rows, LANES)


def _from_tiles(a):
    rows = a.shape[0]
    return jnp.swapaxes(a.reshape(rows // 8, 8, LANES), 0, 1).reshape(rows, LANES)


def _ffn_kernel(x_ref, xh_ref, g_ref, sc_ref, sh_ref, gate_ref, wa_ref, wb_ref, cwa_ref, cwb_ref,
                wd_ref, o_ref, h_ref, xp_ref, acc_ref, ya_ref, yb_ref, *, blocks_per_seq, chunk, sub):
    i = pl.program_id(0)
    j = pl.program_id(1)
    tm = x_ref.shape[0]


    def norm_mod(xr, g, scale1, shift):
        ms = jnp.mean(xr * xr, axis=-1, keepdims=True)
        return (xr * lax.rsqrt(ms + EPS) * g) * scale1 + shift

    @pl.when(j == 0)
    def _():
        g = g_ref[...]
        scale1 = 1.0 + sc_ref[0]
        shift = sh_ref[0]
        for cb in range(x_ref.shape[1] // LANES):
            cols = slice(cb * LANES, (cb + 1) * LANES)
            xp_ref[:, cols] = _to_tiles(x_ref[:, cols])
        _norm_mod_rows(xp_ref, g, scale1, shift, h_ref, FFN_HALO, tm, chunk)
        first = (i % blocks_per_seq) == 0
        h_ref[0:FFN_HALO, :] = jnp.where(first, 0.0, norm_mod(xh_ref[...], g, scale1, shift)).astype(BF16)
        acc_ref[...] = jnp.zeros_like(acc_ref)

    h = h_ref[...]
    tf = wa_ref.shape[2]
    nsub = tf // sub
    sub0 = lax.broadcasted_iota(jnp.int32, (8, sub), 0) == 0

    def conv(y_ref, cw_ref, cs):
        H = FFN_HALO
        last = pltpu.roll(y_ref[H + tm - 8:H + tm, :], 1, axis=0)
        last2 = pltpu.roll(y_ref[H + tm - 16:H + tm - 8, :], 1, axis=0)
        m1 = jnp.where(sub0, y_ref[H - 1:H, :], last)
        m2 = jnp.where(sub0, y_ref[H - 2:H - 1, :], last2)
        y_ref[H - 8:H, :] = m1
        y_ref[H - 16:H - 8, :] = m2
        cw = cw_ref[0, :, cs]
        out = cw[FFN_CONV_WIDTH:FFN_CONV_WIDTH + 1, :]
        for k in range(FFN_CONV_WIDTH):
            off = H - 8 * (FFN_CONV_WIDTH - 1 - k)
            out = out + cw[k:k + 1, :] * y_ref[off:off + tm, :]
        return out

    for c in range(nsub):
        cs = slice(c * sub, (c + 1) * sub)
        ya_ref[c] = _dot(h, wa_ref[0, :, cs])
        yb_ref[c] = _dot(h, wb_ref[0, :, cs])
    for c in range(nsub):
        cs = slice(c * sub, (c + 1) * sub)
        act = (_silu(conv(ya_ref.at[c], cwa_ref, cs)) * conv(yb_ref.at[c], cwb_ref, cs)).astype(BF16)
        d = _dot(act, wd_ref[0, cs, :])
        for cb in range(acc_ref.shape[0]):
            acc_ref[cb] += d[:, cb * LANES:(cb + 1) * LANES]

    @pl.when(j == pl.num_programs(1) - 1)
    def _():
        for cb in range(acc_ref.shape[0]):
            cols = slice(cb * LANES, (cb + 1) * LANES)
            o_ref[:, cols] = x_ref[:, cols] + gate_ref[0, :, cols] * _from_tiles(acc_ref[cb])


def conv_ffn(x, g, sc, sh, gate, w_up, cw, w_down, layer, T, tm=512, tf=512, sub=256):
    N, D = x.shape
    DFF = w_down.shape[1]
    tm = min(tm, T)
    assert N % tm == 0 and T % tm == 0 and DFF % tf == 0 and tm % FFN_HALO == 0
    sub = min(sub, tf)
    nff = DFF // tf
    hb = tm // FFN_HALO
    bmap = lambda i, j: ((i * tm) // T, 0, 0)
    return pl.pallas_call(
        functools.partial(_ffn_kernel, blocks_per_seq=T // tm, chunk=min(128, tm), sub=min(sub, tf)),
        grid=(N // tm, nff),
        in_specs=[pl.BlockSpec((tm, D), lambda i, j: (i, 0)),
                  pl.BlockSpec((FFN_HALO, D), lambda i, j: (jnp.maximum(i * hb - 1, 0), 0)),
                  pl.BlockSpec((1, D), lambda i, j: (0, 0)),
                  pl.BlockSpec((1, 1, D), bmap),
                  pl.BlockSpec((1, 1, D), bmap),
                  pl.BlockSpec((1, 1, D), bmap),
                  pl.BlockSpec((1, D, tf), lambda i, j: (layer, 0, j)),
                  pl.BlockSpec((1, D, tf), lambda i, j: (layer, 0, j + nff)),
                  pl.BlockSpec((1, 8, tf), lambda i, j: (layer, 0, j)),
                  pl.BlockSpec((1, 8, tf), lambda i, j: (layer, 0, j + nff)),
                  pl.BlockSpec((1, tf, D), lambda i, j: (layer, j, 0))],
        out_specs=pl.BlockSpec((tm, D), lambda i, j: (i, 0)),
        out_shape=jax.ShapeDtypeStruct((N, D), F32),
        scratch_shapes=[pltpu.VMEM((tm + FFN_HALO, D), BF16), pltpu.VMEM((tm, D), F32),
                        pltpu.VMEM((D // LANES, tm, LANES), F32),
                        pltpu.VMEM((tf // sub, tm + FFN_HALO, sub), F32),
                        pltpu.VMEM((tf // sub, tm + FFN_HALO, sub), F32)],
        compiler_params=_params(("parallel", "arbitrary")),
        name="conv_ffn",
    )(x, x, g.reshape(1, D), sc, sh, gate, w_up, w_up, cw, cw, w_down)


KV_PLAIN = -1
KV_TRANSPOSED = -2
KV_TILE = 256
SEL_UNROLL = 2


def _rope(y, cos, sin):
    return y * cos + pltpu.roll(y, HEAD_DIM // 2, axis=1) * sin


def _head_norm(x, g):
    ms = jnp.mean(x * x, axis=-1, keepdims=True)
    return x * lax.rsqrt(ms + EPS) * g


def _prep_kernel(*refs, n_q_heads, kv_kinds, n_groups):
    nkv = len(kv_kinds)
    zq_ref = refs[0]
    kv_refs = refs[1:1 + nkv]
    cos_ref, sin_ref, qn_ref, kn_ref = refs[1 + nkv:5 + nkv]
    q_out = refs[5 + nkv]
    kv_out = refs[6 + nkv:]
    cos = cos_ref[0]
    sin = sin_ref[0]
    scale = HEAD_DIM ** -0.5 * LOG2E
    for hd in range(n_q_heads):
        sl = slice(hd * HEAD_DIM, (hd + 1) * HEAD_DIM)
        y = _rope(_head_norm(zq_ref[0, :, sl].astype(F32), qn_ref[...]), cos, sin)
        q_out[0, :, sl] = (y * scale).astype(BF16)
    for idx, kind in enumerate(kv_kinds):
        for gi in range(n_groups):
            sl = slice(gi * HEAD_DIM, (gi + 1) * HEAD_DIM)
            a = kv_refs[idx][0, :, sl]
            if kind >= 0:
                a = _rope(_head_norm(a.astype(F32), kn_ref[kind:kind + 1, :]), cos, sin)
            if kind == KV_TRANSPOSED:
                for c in range(a.shape[0] // KV_TILE):
                    kv_out[idx][0, gi, c] = a[c * KV_TILE:(c + 1) * KV_TILE, :].astype(F32).T.astype(BF16)
            else:
                kv_out[idx][0, gi] = a.astype(BF16)


def attn_prep(z3, cosf, sinf, q_norm, k_norm, q_col, kv_cols, kv_kinds, n_groups, tm=1024):
    B, T, _ = z3.shape
    tm = min(tm, T)
    QW = 8 * HEAD_DIM
    KW = n_groups * HEAD_DIM
    assert q_col % QW == 0 and all(c % KW == 0 for c in kv_cols) and T % tm == 0
    nkv = len(kv_cols)
    in_specs = [pl.BlockSpec((1, tm, QW), lambda b, t: (b, t, q_col // QW))]
    for c in kv_cols:
        in_specs.append(pl.BlockSpec((1, tm, KW), functools.partial(lambda b, t, cb: (b, t, cb), cb=c // KW)))
    in_specs += [pl.BlockSpec((1, tm, HEAD_DIM), lambda b, t: (b, t, 0))] * 2
    in_specs += [pl.BlockSpec((1, HEAD_DIM), lambda b, t: (0, 0)),
                 pl.BlockSpec(k_norm.shape, lambda b, t: (0, 0))]
    out_specs = [pl.BlockSpec((1, tm, QW), lambda b, t: (b, t, 0))]
    out_shape = [jax.ShapeDtypeStruct((B, T, QW), BF16)]
    for kind in kv_kinds:
        if kind == KV_TRANSPOSED:
            assert tm % KV_TILE == 0
            out_specs.append(pl.BlockSpec((1, n_groups, tm // KV_TILE, HEAD_DIM, KV_TILE),
                                          lambda b, t: (b, 0, t, 0, 0)))
            out_shape.append(jax.ShapeDtypeStruct((B, n_groups, T // KV_TILE, HEAD_DIM, KV_TILE), BF16))
        else:
            out_specs.append(pl.BlockSpec((1, n_groups, tm, HEAD_DIM), lambda b, t: (b, 0, t, 0)))
            out_shape.append(jax.ShapeDtypeStruct((B, n_groups, T, HEAD_DIM), BF16))
    return pl.pallas_call(
        functools.partial(_prep_kernel, n_q_heads=QW // HEAD_DIM, kv_kinds=tuple(kv_kinds), n_groups=n_groups),
        grid=(B, T // tm),
        in_specs=in_specs, out_specs=out_specs, out_shape=out_shape,
        compiler_params=_params(("parallel", "parallel")),
        name="attn_prep",
    )(*([z3] * (1 + nkv)), cosf, sinf, q_norm.reshape(1, HEAD_DIM), k_norm)


def _compress_kernel(ak_ref, av_ref, w1k_ref, w1v_ref, pk_ref, pv_ref, w2k_ref, w2v_ref,
                     kn_ref, cos_ref, sin_ref, kc_ref, vc_ref):
    hid = NSA_CMP_HIDDEN

    def mlp(a_ref, w1_ref, p_ref, w2_ref):
        P = _dot(a_ref[0, 0], w1_ref[...])
        Q = _dot(p_ref[...], w1_ref[...])
        pb = Q[0:1, :hid] + Q[1:2, hid:]
        nxt = pltpu.roll(P[:, hid:], P.shape[0] - 1, axis=0)
        hdn = _silu(P[:, :hid] + nxt + pb)
        return _dot(hdn.astype(BF16), w2_ref[...])

    kc = mlp(ak_ref, w1k_ref, pk_ref, w2k_ref)
    kc = _rope(_head_norm(kc, kn_ref[...]), cos_ref[0], sin_ref[0])
    kc_ref[0, 0] = kc.astype(BF16)
    vc_ref[0, 0] = mlp(av_ref, w1v_ref, pv_ref, w2v_ref).T.astype(BF16)


def nsa_compress(kcr, vcr, w1k, w1v, pk, pv, w2k, w2v, kn0, cos_end, sin_end):
    B, G, T, dh = kcr.shape
    nch = T // NSA_CMP_STRIDE
    cw = NSA_CMP_STRIDE * dh
    a_spec = pl.BlockSpec((1, 1, nch, cw), lambda b, g: (b, g, 0, 0))
    full = lambda arr: pl.BlockSpec(arr.shape, lambda b, g: (0,) * arr.ndim)
    tab_spec = pl.BlockSpec((1, nch, dh), lambda b, g: (b, 0, 0))
    o_spec = pl.BlockSpec((1, 1, nch, dh), lambda b, g: (b, g, 0, 0))
    return pl.pallas_call(
        _compress_kernel,
        grid=(B, G),
        in_specs=[a_spec, a_spec, full(w1k), full(w1v), full(pk), full(pv), full(w2k), full(w2v),
                  full(kn0), tab_spec, tab_spec],
        out_specs=[o_spec, pl.BlockSpec((1, 1, dh, nch), lambda b, g: (b, g, 0, 0))],
        out_shape=[jax.ShapeDtypeStruct((B, G, nch, dh), BF16), jax.ShapeDtypeStruct((B, G, dh, nch), BF16)],
        compiler_params=_params(("parallel", "parallel")),
        name="nsa_compress",
    )(kcr.reshape(B, G, nch, cw), vcr.reshape(B, G, nch, cw), w1k, w1v, pk, pv, w2k, w2v,
      kn0, cos_end, sin_end)


def _nsa_kernel(q_ref, kc_ref, vct_ref, ks_ref, vst_ref, kw_ref, vwt_ref, gl_ref, ov_ref, ext_ref,
                o_ref, m_ref, l_ref, acc_ref, *, n_rep, n_cmp, n_sel, n_top):
    tq = q_ref.shape[1]
    tk = KV_TILE
    nrow = kc_ref.shape[2]
    t0 = pl.program_id(2) * tq
    q4 = jnp.concatenate([q_ref[0, :, r * HEAD_DIM:(r + 1) * HEAD_DIM] for r in range(n_rep)], axis=0)

    def heads(a):
        return jnp.concatenate([a] * n_rep, axis=1)

    jb = lax.broadcasted_iota(jnp.int32, (nrow, tq), 0)
    tcol = t0 + lax.broadcasted_iota(jnp.int32, (nrow, tq), 1)
    valid_c = (jb * NSA_CMP_STRIDE + (NSA_CMP_LEN - 1) <= tcol) & (jb < n_cmp)
    validf = heads(jnp.where(valid_c, 1.0, 0.0))
    s = _dot_nt(kc_ref[0, 0], q4) + heads(jnp.where(valid_c, 0.0, NEG))
    e = jnp.exp2(s - jnp.max(s, axis=0, keepdims=True))
    p = e / jnp.sum(e, axis=0, keepdims=True) * validf
    o_cmp = _dot(vct_ref[0, 0], p.astype(BF16))
    psum = p[:, 0:tq]
    for r in range(1, n_rep):
        psum = psum + p[:, r * tq:(r + 1) * tq]

    hi = psum.astype(BF16)
    lo = (psum - hi.astype(F32)).astype(BF16)
    imp = _dot(ov_ref[...], hi) + _dot(ov_ref[...], lo)
    nsr = ov_ref.shape[0]
    jb = lax.broadcasted_iota(jnp.int32, (nsr, tq), 0)
    tcol = t0 + lax.broadcasted_iota(jnp.int32, (nsr, tq), 1)
    cur = lax.shift_right_logical(tcol, int(np.log2(NSA_SEL_LEN)))
    forced = (jb == 0) | (jb == cur) | (jb == cur - 1)
    score = jnp.where(forced, NSA_FORCE, jnp.where(jb * NSA_SEL_LEN <= tcol, imp, -1.0))
    score = jnp.where(jb < n_sel, score, -2.0)
    selT = jnp.zeros((nsr, tq), F32)
    for _ in range(n_top):
        best = jnp.max(score, axis=0, keepdims=True)
        idx = jnp.min(jnp.where(score == best, jb, nsr), axis=0, keepdims=True)
        hit = jb == idx
        selT = jnp.where(hit, 1.0, selT)
        score = jnp.where(hit, -3.0, score)
    sel = selT.astype(BF16)

    m_ref[...] = jnp.full(m_ref.shape, NEG, F32)
    l_ref[...] = jnp.zeros(l_ref.shape, F32)
    acc_ref[...] = jnp.zeros(acc_ref.shape, F32)
    krow = lax.broadcasted_iota(jnp.int32, (tk, tq), 0)
    tq_col = t0 + lax.broadcasted_iota(jnp.int32, (tk, tq), 1)

    def sel_body(it, carry):
        s = []
        for u in range(SEL_UNROLL):
            kt = it * SEL_UNROLL + u
            k0 = pl.multiple_of(kt * tk, tk)
            mask = (_dot(ext_ref[kt], sel) > 0.5) & (k0 + krow <= tq_col)
            s.append(_dot_nt(ks_ref[0, 0, pl.ds(k0, tk), :], q4) + heads(jnp.where(mask, 0.0, NEG)))
        m_old = m_ref[...]
        m_new = m_old
        for u in range(SEL_UNROLL):
            m_new = jnp.maximum(m_new, jnp.max(s[u], axis=0, keepdims=True))
        alpha = jnp.exp2(m_old - m_new)
        lsum = alpha * l_ref[...]
        pv = alpha * acc_ref[...]
        for u in range(SEL_UNROLL):
            p = jnp.exp2(s[u] - m_new)
            lsum = lsum + jnp.sum(p, axis=0, keepdims=True)
            pv = pv + _dot(vst_ref[0, 0, it * SEL_UNROLL + u], p.astype(BF16))
        l_ref[...] = lsum
        acc_ref[...] = pv
        m_ref[...] = m_new
        return carry

    n_tiles = (t0 + tq) // tk
    lax.fori_loop(0, (n_tiles + SEL_UNROLL - 1) // SEL_UNROLL, sel_body, 0)
    o_sel = acc_ref[...] / l_ref[...]

    W = NSA_WINDOW
    n_win = (W + tq) // tk
    kstart = pl.multiple_of(jnp.maximum(t0 - W, 0), tk)
    jt = kstart // tk
    sw = []
    for c in range(n_win):
        rel = tq_col - (kstart + c * tk) - krow
        bias = heads(jnp.where((rel >= 0) & (rel < W), 0.0, NEG))
        sw.append(_dot_nt(kw_ref[0, 0, pl.ds(kstart + c * tk, tk), :], q4) + bias)
    m = jnp.max(sw[0], axis=0, keepdims=True)
    for c in range(1, n_win):
        m = jnp.maximum(m, jnp.max(sw[c], axis=0, keepdims=True))
    den = jnp.zeros_like(m)
    o_win = jnp.zeros((HEAD_DIM, n_rep * tq), F32)
    for c in range(n_win):
        ew = jnp.exp2(sw[c] - m)
        den = den + jnp.sum(ew, axis=0, keepdims=True)
        o_win = o_win + _dot(vwt_ref[0, 0, jt + c], ew.astype(BF16))
    o_win = o_win / den

    gT = _sigmoid(gl_ref[0]).T
    for r in range(n_rep):
        hs = slice(r * tq, (r + 1) * tq)
        c3 = 3 * r
        oT = (gT[c3:c3 + 1, :] * o_cmp[:, hs] + gT[c3 + 1:c3 + 2, :] * o_sel[:, hs]
              + gT[c3 + 2:c3 + 3, :] * o_win[:, hs])
        o_ref[0, :, r * HEAD_DIM:(r + 1) * HEAD_DIM] = oT.T.astype(BF16)


def nsa_attention(qn, kc, vct, ks, vst, kw, vwt, gl3, tq=512):
    B, T, QW = qn.shape
    G = kc.shape[1]
    n_rep = QW // HEAD_DIM // G
    nrow = kc.shape[2]
    tk = KV_TILE
    n_cmp = (T - NSA_CMP_LEN) // NSA_CMP_STRIDE + 1
    n_sel = T // NSA_SEL_LEN
    assert T % tq == 0 and n_sel <= nrow and T >= NSA_WINDOW + tq and tq % tk == 0 and NSA_WINDOW % tk == 0
    assert (tq // tk) % SEL_UNROLL == 0
    starts = np.arange(nrow) * NSA_CMP_STRIDE
    sel_start = np.arange(nrow) * NSA_SEL_LEN
    nsr = -(-n_sel // BF16_SUBLANES) * BF16_SUBLANES
    ov = ((starts[None, :] <= sel_start[:, None] + NSA_SEL_LEN - 1)
          & (starts[None, :] + NSA_CMP_LEN - 1 >= sel_start[:, None])
          & (np.arange(nrow)[:, None] < n_sel) & (np.arange(nrow)[None, :] < n_cmp))[:nsr]
    ext = (np.arange(T)[:, None] // NSA_SEL_LEN == np.arange(nsr)[None, :]).reshape(T // tk, tk, nsr)
    k_spec = pl.BlockSpec((1, 1, T, HEAD_DIM), lambda b, g, i: (b, g, 0, 0))
    vt_spec = pl.BlockSpec((1, 1, T // tk, HEAD_DIM, tk), lambda b, g, i: (b, g, 0, 0, 0))
    c_spec = pl.BlockSpec((1, 1, nrow, HEAD_DIM), lambda b, g, i: (b, g, 0, 0))
    hw = n_rep * HEAD_DIM
    return pl.pallas_call(
        functools.partial(_nsa_kernel, n_rep=n_rep, n_cmp=n_cmp, n_sel=n_sel, n_top=min(NSA_TOPK, n_sel)),
        grid=(B, G, T // tq),
        in_specs=[pl.BlockSpec((1, tq, hw), lambda b, g, i: (b, i, g)),
                  c_spec, pl.BlockSpec((1, 1, HEAD_DIM, nrow), lambda b, g, i: (b, g, 0, 0)),
                  k_spec, vt_spec, k_spec, vt_spec,
                  pl.BlockSpec((1, tq, LANES), lambda b, g, i: (b, i, g)),
                  pl.BlockSpec((nsr, nrow), lambda b, g, i: (0, 0)),
                  pl.BlockSpec((T // tk, tk, nsr), lambda b, g, i: (0, 0, 0))],
        out_specs=pl.BlockSpec((1, tq, hw), lambda b, g, i: (b, i, g)),
        out_shape=jax.ShapeDtypeStruct((B, T, QW), BF16),
        scratch_shapes=[pltpu.VMEM((1, n_rep * tq), F32), pltpu.VMEM((1, n_rep * tq), F32),
                        pltpu.VMEM((HEAD_DIM, n_rep * tq), F32)],
        compiler_params=_params(("parallel", "parallel", "arbitrary")),
        name="nsa_attention",
    )(qn, kc, vct, ks, vst, kw, vwt, gl3, jnp.asarray(ov, BF16), jnp.asarray(ext, BF16))


def _swa_kernel(sink_ref, q_ref, k_ref, vt_ref, o_ref, *, n_rep):
    tq = q_ref.shape[1]
    tk = KV_TILE
    W = SWA_WINDOW
    n_win = (tk + tq) // tk
    t0 = pl.program_id(1) * tq
    kstart = pl.multiple_of(jnp.maximum(t0 - tk, 0), tk)
    jt = kstart // tk
    krow = lax.broadcasted_iota(jnp.int32, (tk, tq), 0)
    tq_col = t0 + lax.broadcasted_iota(jnp.int32, (tk, tq), 1)
    bias = []
    for c in range(n_win):
        rel = tq_col - (kstart + c * tk) - krow
        bias.append(jnp.concatenate([jnp.where((rel >= 0) & (rel < W), 0.0, NEG)] * n_rep, axis=1))
    for g in range(k_ref.shape[1]):
        h0 = g * n_rep
        q4 = jnp.concatenate([q_ref[0, :, (h0 + r) * HEAD_DIM:(h0 + r + 1) * HEAD_DIM] for r in range(n_rep)],
                             axis=0)
        sink = jnp.concatenate([jnp.full((1, tq), sink_ref[h0 + r] * LOG2E, F32) for r in range(n_rep)], axis=1)
        s = [_dot_nt(k_ref[0, g, pl.ds(kstart + c * tk, tk), :], q4) + bias[c] for c in range(n_win)]
        m = sink
        for c in range(n_win):
            m = jnp.maximum(m, jnp.max(s[c], axis=0, keepdims=True))
        den = jnp.exp2(sink - m)
        oT = jnp.zeros((HEAD_DIM, n_rep * tq), F32)
        for c in range(n_win):
            e = jnp.exp2(s[c] - m)
            den = den + jnp.sum(e, axis=0, keepdims=True)
            oT = oT + _dot(vt_ref[0, g, jt + c], e.astype(BF16))
        oT = oT / den
        for r in range(n_rep):
            o_ref[0, :, (h0 + r) * HEAD_DIM:(h0 + r + 1) * HEAD_DIM] = oT[:, r * tq:(r + 1) * tq].T.astype(BF16)


def swa_attention(qn, kn, vt, sinks, tq=256):
    B, T, QW = qn.shape
    G = kn.shape[1]
    n_rep = QW // HEAD_DIM // G
    tk = KV_TILE
    assert T % tq == 0 and tq % tk == 0 and SWA_WINDOW <= tk and T >= tk + tq
    return pl.pallas_call(
        functools.partial(_swa_kernel, n_rep=n_rep),
        grid=(B, T // tq),
        in_specs=[pl.BlockSpec(memory_space=pltpu.SMEM),
                  pl.BlockSpec((1, tq, QW), lambda b, i: (b, i, 0)),
                  pl.BlockSpec((1, G, T, HEAD_DIM), lambda b, i: (b, 0, 0, 0)),
                  pl.BlockSpec((1, G, T // tk, HEAD_DIM, tk), lambda b, i: (b, 0, 0, 0, 0))],
        out_specs=pl.BlockSpec((1, tq, QW), lambda b, i: (b, i, 0)),
        out_shape=jax.ShapeDtypeStruct((B, T, QW), BF16),
        compiler_params=_params(("parallel", "parallel")),
        name="swa_attention",
    )(sinks, qn, kn, vt)


CONF_HALO = 32


def _conformer_kernel(a1_ref, a2_ref, h1_ref, h2_ref, w_ref, b_ref, g_ref, be_ref, o_ref,
                      glu_ref, conv_ref, *, blocks_per_seq):
    tm, C = a1_ref.shape
    PRE = CONF_HALO
    nctx = CONV_WIDTH - 1
    first = (pl.program_id(0) % blocks_per_seq) == 0

    gh = jnp.where(first, 0.0, h1_ref[...].astype(F32) * _sigmoid(h2_ref[...].astype(F32)))
    sub0 = lax.broadcasted_iota(jnp.int32, (8 * nctx, LANES), 0) % 8 == 0
    for cs in range(C // LANES):
        sl = slice(cs * LANES, (cs + 1) * LANES)
        g = _to_tiles(a1_ref[:, sl].astype(F32) * _sigmoid(a2_ref[:, sl].astype(F32)))
        glu_ref[cs, 8 * PRE:, :] = g
        moved = pltpu.roll(g[tm - 8 * nctx:, :], 1, axis=0)
        halo = jnp.concatenate([jnp.broadcast_to(gh[r:r + 1, sl], (8, LANES)) for r in range(PRE - nctx, PRE)],
                               axis=0)
        glu_ref[cs, 8 * (PRE - nctx):8 * PRE, :] = jnp.where(sub0, halo, moved)

    rb = 128

    def conv_group(gi, carry):
        r0 = pl.multiple_of(gi * rb, rb)
        for cs in range(C // LANES):
            sl = slice(cs * LANES, (cs + 1) * LANES)
            acc = jnp.broadcast_to(b_ref[:, sl], (rb, LANES))
            for k in range(CONV_WIDTH):
                off = 8 * (PRE - nctx + k)
                acc = acc + w_ref[k:k + 1, sl] * glu_ref[cs, pl.ds(r0 + off, rb), :]
            conv_ref[cs, pl.ds(r0, rb), :] = acc
        return carry

    lax.fori_loop(0, tm // rb, conv_group, 0)

    ns = C // LANES
    for r0 in range(0, tm, rb):
        y = [conv_ref[cs, r0:r0 + rb, :] for cs in range(ns)]
        mu = sum(jnp.sum(v, axis=-1, keepdims=True) for v in y) / C
        d = [v - mu for v in y]
        var = sum(jnp.sum(v * v, axis=-1, keepdims=True) for v in d) / C
        rs = lax.rsqrt(var + EPS)
        for cs in range(ns):
            sl = slice(cs * LANES, (cs + 1) * LANES)
            conv_ref[cs, r0:r0 + rb, :] = _silu(d[cs] * rs * g_ref[:, sl] + be_ref[:, sl])
    for cs in range(ns):
        o_ref[:, cs * LANES:(cs + 1) * LANES] = _from_tiles(conv_ref[cs]).astype(BF16)


def conformer_conv(z, conv_w, conv_b, ln_g, ln_b, T, tm=512):
    N = z.shape[0]
    C = conv_w.shape[1]
    tm = min(tm, T)
    assert N % tm == 0 and T % tm == 0 and tm % CONF_HALO == 0 and CONV_WIDTH - 1 <= min(CONF_HALO, tm // 8)
    hb = tm // CONF_HALO
    w = jnp.zeros((CONF_HALO, C), F32).at[:CONV_WIDTH].set(conv_w)
    hmap = lambda c: (lambda i: (jnp.maximum(i * hb - 1, 0), c))
    vec = pl.BlockSpec((1, C), lambda i: (0, 0))
    return pl.pallas_call(
        functools.partial(_conformer_kernel, blocks_per_seq=T // tm),
        grid=(N // tm,),
        in_specs=[pl.BlockSpec((tm, C), lambda i: (i, 0)), pl.BlockSpec((tm, C), lambda i: (i, 1)),
                  pl.BlockSpec((CONF_HALO, C), hmap(0)), pl.BlockSpec((CONF_HALO, C), hmap(1)),
                  pl.BlockSpec((CONF_HALO, C), lambda i: (0, 0)), vec, vec, vec],
        out_specs=pl.BlockSpec((tm, C), lambda i: (i, 0)),
        out_shape=jax.ShapeDtypeStruct((N, C), BF16),
        scratch_shapes=[pltpu.VMEM((C // LANES, tm + 8 * CONF_HALO, LANES), F32),
                        pltpu.VMEM((C // LANES, tm, LANES), F32)],
        compiler_params=_params(("parallel",)),
        name="conformer_conv",
    )(z, z, z, z, w, conv_b.reshape(1, C), ln_g.reshape(1, C), ln_b.reshape(1, C))


def _gmlp_kernel(u0_ref, u1_ref, v0_ref, v1_ref, g_ref, be_ref, ws_ref, bs_ref, o_ref, *, n_groups):
    tm, hw = v0_ref.shape
    C = 2 * hw
    ch = GMLP_CHUNK
    gv = [_gelu_tanh(v0_ref[...].astype(F32)), _gelu_tanh(v1_ref[...].astype(F32))]
    mu = (jnp.sum(gv[0], axis=-1, keepdims=True) + jnp.sum(gv[1], axis=-1, keepdims=True)) / C
    d = [gv[0] - mu, gv[1] - mu]
    var = (jnp.sum(d[0] * d[0], axis=-1, keepdims=True) + jnp.sum(d[1] * d[1], axis=-1, keepdims=True)) / C
    rs = lax.rsqrt(var + EPS)
    vn = [(d[hf] * rs * g_ref[:, hf * hw:(hf + 1) * hw] + be_ref[:, hf * hw:(hf + 1) * hw]).astype(BF16)
          for hf in range(2)]
    u_refs = [u0_ref, u1_ref]
    tril = lax.broadcasted_iota(jnp.int32, (ch, ch), 0) >= lax.broadcasted_iota(jnp.int32, (ch, ch), 1)
    gph = n_groups // 2
    for gi in range(n_groups):
        hf, col = gi // gph, (gi % gph) * HEAD_DIM
        w = jnp.where(tril, ws_ref[gi], 0.0).astype(BF16)
        for c in range(tm // ch):
            rows = slice(c * ch, (c + 1) * ch)
            sp = _dot(w, vn[hf][rows, col:col + HEAD_DIM]) + bs_ref[gi]
            u = _gelu_tanh(u_refs[hf][rows, col:col + HEAD_DIM].astype(F32))
            o_ref[rows, gi * HEAD_DIM:(gi + 1) * HEAD_DIM] = (u * sp).astype(BF16)


def chunked_gmlp(z, u_col, ln_g, ln_b, w_s, b_s, T, tm=1024):
    N = z.shape[0]
    n_groups, ch, _ = w_s.shape
    C = n_groups * HEAD_DIM
    hw = C // 2
    tm = min(tm, T)
    assert N % tm == 0 and T % tm == 0 and tm % ch == 0 and u_col % hw == 0
    cb = u_col // hw
    bsb = jnp.broadcast_to(b_s[:, :, None], (n_groups, ch, HEAD_DIM))
    zspec = lambda k: pl.BlockSpec((tm, hw), lambda i: (i, cb + k))
    vec = pl.BlockSpec((1, C), lambda i: (0, 0))
    return pl.pallas_call(
        functools.partial(_gmlp_kernel, n_groups=n_groups),
        grid=(N // tm,),
        in_specs=[zspec(0), zspec(1), zspec(2), zspec(3), vec, vec,
                  pl.BlockSpec((n_groups, ch, ch), lambda i: (0, 0, 0)),
                  pl.BlockSpec((n_groups, ch, HEAD_DIM), lambda i: (0, 0, 0))],
        out_specs=pl.BlockSpec((tm, C), lambda i: (i, 0)),
        out_shape=jax.ShapeDtypeStruct((N, C), BF16),
        compiler_params=_params(("parallel",)),
        name="chunked_gmlp",
    )(z, z, z, z, ln_g.reshape(1, C), ln_b.reshape(1, C), w_s, bsb)


def _even_mixer(xf, B, T, g, sc, sh, cosf, sinf, w_in, w_out, conv_w, conv_b, conv_ln_g, conv_ln_b,
                q_norm, k_norm, cmp_k_pos, cmp_k_w1, cmp_k_w2, cmp_v_pos, cmp_v_w1, cmp_v_w2):
    D = xf.shape[1]
    G = NSA_KV_GROUPS
    C = conv_w.shape[1]
    QW = D // 2
    KW = G * HEAD_DIM
    n_rep = QW // HEAD_DIM // G
    main = 2 * C + QW + 6 * KW
    wg = w_in[:, main:].reshape(D, G, n_rep * 3)
    wg = jnp.pad(wg, ((0, 0), (0, 0), (0, LANES - n_rep * 3))).reshape(D, G * LANES)
    z, gl = in_proj(xf, g, sc, sh, w_in.astype(BF16), main, wg.astype(BF16), T)
    ya = conformer_conv(z, conv_w, conv_b, conv_ln_g, conv_ln_b, T)
    q_col = 2 * C
    kv_cols = [q_col + QW + k * KW for k in range(6)]
    qn, kcr, vcr, ks, vs, kw, vw = attn_prep(z.reshape(B, T, main), cosf, sinf, q_norm, k_norm,
                                             q_col, kv_cols,
                                             (KV_PLAIN, KV_PLAIN, 1, KV_TRANSPOSED, 2, KV_TRANSPOSED), G)
    half = NSA_CMP_STRIDE * HEAD_DIM
    hid = NSA_CMP_HIDDEN
    nch = T // NSA_CMP_STRIDE

    def w1cat(w1):
        return jnp.concatenate([w1[:half], w1[half:]], axis=1).astype(BF16)

    def posrows(pos):
        return jnp.zeros((8, half), F32).at[0].set(pos[:NSA_CMP_STRIDE].reshape(half)) \
                  .at[1].set(pos[NSA_CMP_STRIDE:].reshape(half)).astype(BF16)

    def at_block_ends(tab):
        e = tab[:, NSA_CMP_LEN - 1::NSA_CMP_STRIDE]
        return jnp.pad(e, ((0, 0), (0, nch - e.shape[1]), (0, 0)))

    kc, vc = nsa_compress(kcr, vcr, w1cat(cmp_k_w1), w1cat(cmp_v_w1), posrows(cmp_k_pos), posrows(cmp_v_pos),
                          cmp_k_w2.astype(BF16), cmp_v_w2.astype(BF16), k_norm[0:1],
                          at_block_ends(cosf), at_block_ends(sinf))
    yb = nsa_attention(qn, kc, vc, ks, vs, kw, vw, gl.reshape(B, T, G * LANES))
    return ya, yb.reshape(B * T, QW), w_out.astype(BF16)


def _odd_mixer(xf, B, T, g, sc, sh, cosf, sinf, w_in, w_out, q_norm, k_norm, sinks, ln_g, ln_b, w_s, b_s):
    D = xf.shape[1]
    G = SWA_KV_HEADS
    QW = D // 2
    KW = G * HEAD_DIM
    z, = in_proj(xf, g, sc, sh, w_in.astype(BF16), w_in.shape[1], None, T)
    qn, kn, v = attn_prep(z.reshape(B, T, z.shape[1]), cosf, sinf, q_norm, k_norm.reshape(1, HEAD_DIM),
                          0, [QW, QW + KW], (0, KV_TRANSPOSED), G)
    yc = swa_attention(qn, kn, v, sinks)
    yd = chunked_gmlp(z, QW + 2 * KW, ln_g, ln_b, w_s, b_s, T)
    return yc.reshape(B * T, QW), yd, w_out.astype(BF16)


def kernel(x, c, positions, ada_w, ada_b, norm_g, ffn_w_up, ffn_conv_w, ffn_conv_b, ffn_w_down, ev_w_in, ev_w_out, ev_conv_w, ev_conv_b, ev_conv_ln_g, ev_conv_ln_b, ev_q_norm, ev_k_norm, ev_cmp_k_pos, ev_cmp_k_w1, ev_cmp_k_w2, ev_cmp_v_pos, ev_cmp_v_w1, ev_cmp_v_w2, od_w_in, od_w_out, od_q_norm, od_k_norm, od_sinks, od_gmlp_ln_g, od_gmlp_ln_b, od_gmlp_w_s, od_gmlp_b_s):
    B, T, D = x.shape
    depth = ada_w.shape[0]
    cosf, sinf = rope_tables(positions)
    mod = adaln(c, ada_w, ada_b)
    xf = x.reshape(B * T, D)
    w_up_b = ffn_w_up.astype(BF16)
    w_down_b = ffn_w_down.astype(BF16)
    cw_all = jnp.zeros((depth, 8, ffn_conv_w.shape[2]), F32).at[:, :FFN_CONV_WIDTH].set(ffn_conv_w) \
                .at[:, FFN_CONV_WIDTH].set(ffn_conv_b)
    for i in range(depth):
        sh1, sc1, g1, sh2, sc2, g2 = [m.reshape(B, 1, D) for m in jnp.split(mod[i], 6, axis=-1)]
        j = i // 2
        if i % 2 == 0:
            ya, yb, w_out = _even_mixer(xf, B, T, norm_g[i, 0], sc1, sh1, cosf, sinf, ev_w_in[j], ev_w_out[j],
                                        ev_conv_w[j], ev_conv_b[j], ev_conv_ln_g[j], ev_conv_ln_b[j],
                                        ev_q_norm[j], ev_k_norm[j], ev_cmp_k_pos[j], ev_cmp_k_w1[j],
                                        ev_cmp_k_w2[j], ev_cmp_v_pos[j], ev_cmp_v_w1[j], ev_cmp_v_w2[j])
        else:
            ya, yb, w_out = _odd_mixer(xf, B, T, norm_g[i, 0], sc1, sh1, cosf, sinf, od_w_in[j], od_w_out[j],
                                       od_q_norm[j], od_k_norm[j], od_sinks[j], od_gmlp_ln_g[j],
                                       od_gmlp_ln_b[j], od_gmlp_w_s[j], od_gmlp_b_s[j])
        xf = out_proj(ya, yb, w_out, xf, g1, T)
        xf = conv_ffn(xf, norm_g[i, 1], sc2, sh2, g2, w_up_b, cw_all, w_down_b, i, T)
    return xf.reshape(B, T, D)
```

```python
import functools

import numpy as np
import jax
import jax.numpy as jnp
from jax import lax
from jax.experimental import pallas as pl
from jax.experimental.pallas import tpu as pltpu

F32 = jnp.float32
BF16 = jnp.bfloat16

HEAD_DIM = 128
ROPE_THETA = 10000.0
EPS = 1e-6
NEG = -1e30
LOG2E = 1.4426950408889634

CONV_WIDTH = 31
NSA_KV_GROUPS = 2
NSA_CMP_LEN = 32
NSA_CMP_STRIDE = 16
NSA_CMP_HIDDEN = 256
NSA_SEL_LEN = 64
NSA_TOPK = 8
NSA_WINDOW = 256
NSA_FORCE = 1e6
SWA_KV_HEADS = 2
SWA_WINDOW = 128
GMLP_CHUNK = 128
FFN_CONV_WIDTH = 3

V7X_VMEM_BYTES = 64 * 1024 * 1024
VMEM_LIMIT = V7X_VMEM_BYTES - 8 * 1024 * 1024
LANES = 128
BF16_SUBLANES = 16


def _params(sem):
    return pltpu.CompilerParams(dimension_semantics=sem, vmem_limit_bytes=VMEM_LIMIT)


def _dot(a, b):
    return jnp.dot(a, b, preferred_element_type=F32)


def _dot_nt(a, b):
    return lax.dot_general(a, b, (((1,), (1,)), ((), ())), preferred_element_type=F32)


def _sigmoid(x):
    return 1.0 / (1.0 + jnp.exp(-x))


def _silu(x):
    return x * _sigmoid(x)


def _gelu_tanh(x):
    return 0.5 * x * (1.0 + jnp.tanh(np.sqrt(2.0 / np.pi).astype(np.float32) * (x + 0.044715 * (x * x * x))))


def _rope_kernel(pos_ref, inv_ref, cos_ref, sin_ref):
    ang = pos_ref[0].astype(F32) * inv_ref[0:1, :]
    cos_ref[0] = jnp.cos(ang)
    sin_ref[0] = jnp.sin(ang) * inv_ref[1:2, :]


def rope_tables(positions):
    B, T = positions.shape
    inv = ROPE_THETA ** (-jnp.arange(0, HEAD_DIM, 2, dtype=F32) / HEAD_DIM)
    half = HEAD_DIM // 2
    sign = jnp.concatenate([-jnp.ones((half,), F32), jnp.ones((half,), F32)])
    tab = jnp.stack([jnp.concatenate([inv, inv]), sign])
    return pl.pallas_call(
        _rope_kernel,
        grid=(B,),
        in_specs=[pl.BlockSpec((1, T, 1), lambda b: (b, 0, 0)),
                  pl.BlockSpec((2, HEAD_DIM), lambda b: (0, 0))],
        out_specs=[pl.BlockSpec((1, T, HEAD_DIM), lambda b: (b, 0, 0))] * 2,
        out_shape=[jax.ShapeDtypeStruct((B, T, HEAD_DIM), F32)] * 2,
        compiler_params=_params(("parallel",)),
        name="rope_tables",
    )(positions.reshape(B, T, 1), tab)


def _adaln_kernel(c_ref, w_ref, b_ref, o_ref):
    ca = _silu(c_ref[...]).astype(BF16)
    o_ref[0] = _dot(ca, w_ref[0].astype(BF16)) + b_ref[0]


def adaln(c, ada_w, ada_b, tn=2048):
    L, D, N6 = ada_w.shape
    B = c.shape[0]
    return pl.pallas_call(
        _adaln_kernel,
        grid=(L, N6 // tn),
        in_specs=[pl.BlockSpec((B, D), lambda l, j: (0, 0)),
                  pl.BlockSpec((1, D, tn), lambda l, j: (l, 0, j)),
                  pl.BlockSpec((1, 1, tn), lambda l, j: (l, 0, j))],
        out_specs=pl.BlockSpec((1, B, tn), lambda l, j: (l, 0, j)),
        out_shape=jax.ShapeDtypeStruct((L, B, N6), F32),
        compiler_params=_params(("parallel", "parallel")),
        name="adaln",
    )(c, ada_w, ada_b.reshape(L, 1, N6))


def _norm_mod_rows(x_ref, g, scale1, shift, out_ref, out_row0, nrows, chunk):
    def body(ci, carry):
        r = pl.multiple_of(ci * chunk, chunk)
        x = x_ref[pl.ds(r, chunk), :]
        ms = jnp.mean(x * x, axis=-1, keepdims=True)
        y = x * lax.rsqrt(ms + EPS) * g
        out_ref[pl.ds(out_row0 + r, chunk), :] = (y * scale1 + shift).astype(BF16)
        return carry
    lax.fori_loop(0, nrows // chunk, body, 0)


def _inproj_kernel(x_ref, g_ref, sc_ref, sh_ref, scn_ref, shn_ref, w_ref, *rest, has_gate, rows_per, n_chunks):
    if has_gate:
        wg_ref, o_ref, og_ref, ha_ref, hb_ref = rest
    else:
        o_ref, ha_ref, hb_ref = rest
    i = pl.program_id(0)
    j = pl.program_id(1)
    tm = x_ref.shape[0]
    g = g_ref[...]

    def norm_rows(r0, nrows, dst_ref, s_ref, t_ref):
        x = x_ref[pl.ds(r0, nrows), :]
        ms = jnp.mean(x * x, axis=-1, keepdims=True)
        y = x * lax.rsqrt(ms + EPS) * g
        dst_ref[pl.ds(r0, nrows), :] = (y * (1.0 + s_ref[0]) + t_ref[0]).astype(BF16)

    @pl.when((i == 0) & (j == 0))
    def _():
        def body(ci, carry):
            norm_rows(pl.multiple_of(ci * rows_per, BF16_SUBLANES), rows_per, ha_ref, sc_ref, sh_ref)
            return carry
        lax.fori_loop(0, tm // rows_per, body, 0)
        if tm % rows_per:
            norm_rows(tm - rows_per, rows_per, ha_ref, sc_ref, sh_ref)

    c = jnp.clip(j - 1, 0, n_chunks - 1)
    r0 = pl.multiple_of(jnp.minimum(c * rows_per, tm - rows_per), BF16_SUBLANES)

    def step(cur_ref, nxt_ref):
        if has_gate:
            @pl.when(j == 0)
            def _():
                og_ref[...] = _dot(cur_ref[...], wg_ref[...])
        o_ref[...] = _dot(cur_ref[...], w_ref[...]).astype(o_ref.dtype)
        norm_rows(r0, rows_per, nxt_ref, scn_ref, shn_ref)

    @pl.when(i % 2 == 0)
    def _():
        step(ha_ref, hb_ref)

    @pl.when(i % 2 == 1)
    def _():
        step(hb_ref, ha_ref)


IN_PROJ_TN_CAP = 1792


def in_proj(x, g, sc, sh, w, n_out, wg, T, tm=1024):
    N, D = x.shape
    Nout = n_out
    tm = min(tm, T)
    tn = max(t for t in range(256, IN_PROJ_TN_CAP + 1, 256) if Nout % t == 0 and Nout // t >= 2)
    nb, nj = N // tm, Nout // tn
    assert N % tm == 0 and T % tm == 0 and Nout % tn == 0 and nj >= 2
    n_chunks = nj - 1
    rows_per = -(-tm // n_chunks)
    rows_per = -(-rows_per // BF16_SUBLANES) * BF16_SUBLANES
    nxt = lambda i: jnp.minimum(i + 1, nb - 1)
    xmap = lambda i, j: (jnp.where((i == 0) & (j == 0), 0, nxt(i)), 0)
    bmap = lambda i, j: ((i * tm) // T, 0, 0)
    nmap = lambda i, j: ((nxt(i) * tm) // T, 0, 0)
    in_specs = [pl.BlockSpec((tm, D), xmap),
                pl.BlockSpec((1, D), lambda i, j: (0, 0)),
                pl.BlockSpec((1, 1, D), bmap), pl.BlockSpec((1, 1, D), bmap),
                pl.BlockSpec((1, 1, D), nmap), pl.BlockSpec((1, 1, D), nmap),
                pl.BlockSpec((D, tn), lambda i, j: (0, j))]
    out_specs = [pl.BlockSpec((tm, tn), lambda i, j: (i, j))]
    out_shape = [jax.ShapeDtypeStruct((N, Nout), BF16)]
    args = [x, g.reshape(1, D), sc, sh, sc, sh, w]
    if wg is not None:
        ng = wg.shape[1]
        in_specs.append(pl.BlockSpec((D, ng), lambda i, j: (0, 0)))
        out_specs.append(pl.BlockSpec((tm, ng), lambda i, j: (i, 0)))
        out_shape.append(jax.ShapeDtypeStruct((N, ng), F32))
        args.append(wg)
    return pl.pallas_call(
        functools.partial(_inproj_kernel, has_gate=wg is not None, rows_per=rows_per, n_chunks=n_chunks),
        grid=(nb, nj),
        in_specs=in_specs, out_specs=out_specs, out_shape=out_shape,
        scratch_shapes=[pltpu.VMEM((tm, D), BF16), pltpu.VMEM((tm, D), BF16)],
        compiler_params=_params(("arbitrary", "arbitrary")),
        name="in_proj",
    )(*args)


def _outproj_kernel(ya_ref, yb_ref, wa_ref, wb_ref, x_ref, gate_ref, o_ref):
    y = _dot(ya_ref[...], wa_ref[...]) + _dot(yb_ref[...], wb_ref[...])
    o_ref[...] = x_ref[...] + gate_ref[0] * y


def out_proj(ya, yb, w_out, x, gate, T, tm=512, tn=2048):
    N, D = x.shape
    Ka, Kb = ya.shape[1], yb.shape[1]
    tm = min(tm, T)
    assert N % tm == 0 and T % tm == 0 and D % tn == 0 and Ka % Kb == 0
    return pl.pallas_call(
        _outproj_kernel,
        grid=(N // tm, D // tn),
        in_specs=[pl.BlockSpec((tm, Ka), lambda i, j: (i, 0)),
                  pl.BlockSpec((tm, Kb), lambda i, j: (i, 0)),
                  pl.BlockSpec((Ka, tn), lambda i, j: (0, j)),
                  pl.BlockSpec((Kb, tn), lambda i, j: (Ka // Kb, j)),
                  pl.BlockSpec((tm, tn), lambda i, j: (i, j)),
                  pl.BlockSpec((1, 1, tn), lambda i, j: ((i * tm) // T, 0, j))],
        out_specs=pl.BlockSpec((tm, tn), lambda i, j: (i, j)),
        out_shape=jax.ShapeDtypeStruct((N, D), F32),
        compiler_params=_params(("parallel", "parallel")),
        name="out_proj",
    )(ya, yb, w_out, w_out, x, gate)


FFN_HALO = BF16_SUBLANES


def _to_tiles(a):
    rows = a.shape[0]
    return jnp.swapaxes(a.reshape(8, rows // 8, LANES), 0, 1).reshape(rows, LANES)


def _from_tiles(a):
    rows = a.shape[0]
    return jnp.swapaxes(a.reshape(rows // 8, 8, LANES), 0, 1).reshape(rows, LANES)


def _ffn_kernel(x_ref, xh_ref, g_ref, sc_ref, sh_ref, gate_ref, wa_ref, wb_ref, cwa_ref, cwb_ref,
                wd_ref, o_ref, h_ref, xp_ref, acc_ref, ya_ref, yb_ref, *, blocks_per_seq, chunk, sub):
    i = pl.program_id(0)
    j = pl.program_id(1)
    tm = x_ref.shape[0]


    def norm_mod(xr, g, scale1, shift):
        ms = jnp.mean(xr * xr, axis=-1, keepdims=True)
        return (xr * lax.rsqrt(ms + EPS) * g) * scale1 + shift

    @pl.when(j == 0)
    def _():
        g = g_ref[...]
        scale1 = 1.0 + sc_ref[0]
        shift = sh_ref[0]
        for cb in range(x_ref.shape[1] // LANES):
            cols = slice(cb * LANES, (cb + 1) * LANES)
            xp_ref[:, cols] = _to_tiles(x_ref[:, cols])
        _norm_mod_rows(xp_ref, g, scale1, shift, h_ref, FFN_HALO, tm, chunk)
        first = (i % blocks_per_seq) == 0
        h_ref[0:FFN_HALO, :] = jnp.where(first, 0.0, norm_mod(xh_ref[...], g, scale1, shift)).astype(BF16)
        acc_ref[...] = jnp.zeros_like(acc_ref)

    h = h_ref[...]
    tf = wa_ref.shape[2]
    nsub = tf // sub
    sub0 = lax.broadcasted_iota(jnp.int32, (8, sub), 0) == 0

    def conv(y_ref, cw_ref, cs):
        H = FFN_HALO
        last = pltpu.roll(y_ref[H + tm - 8:H + tm, :], 1, axis=0)
        last2 = pltpu.roll(y_ref[H + tm - 16:H + tm - 8, :], 1, axis=0)
        m1 = jnp.where(sub0, y_ref[H - 1:H, :], last)
        m2 = jnp.where(sub0, y_ref[H - 2:H - 1, :], last2)
        y_ref[H - 8:H, :] = m1
        y_ref[H - 16:H - 8, :] = m2
        cw = cw_ref[0, :, cs]
        out = cw[FFN_CONV_WIDTH:FFN_CONV_WIDTH + 1, :]
        for k in range(FFN_CONV_WIDTH):
            off = H - 8 * (FFN_CONV_WIDTH - 1 - k)
            out = out + cw[k:k + 1, :] * y_ref[off:off + tm, :]
        return out

    for c in range(nsub):
        cs = slice(c * sub, (c + 1) * sub)
        ya_ref[c] = _dot(h, wa_ref[0, :, cs])
        yb_ref[c] = _dot(h, wb_ref[0, :, cs])
    for c in range(nsub):
        cs = slice(c * sub, (c + 1) * sub)
        act = (_silu(conv(ya_ref.at[c], cwa_ref, cs)) * conv(yb_ref.at[c], cwb_ref, cs)).astype(BF16)
        d = _dot(act, wd_ref[0, cs, :])
        for cb in range(acc_ref.shape[0]):
            acc_ref[cb] += d[:, cb * LANES:(cb + 1) * LANES]

    @pl.when(j == pl.num_programs(1) - 1)
    def _():
        for cb in range(acc_ref.shape[0]):
            cols = slice(cb * LANES, (cb + 1) * LANES)
            o_ref[:, cols] = x_ref[:, cols] + gate_ref[0, :, cols] * _from_tiles(acc_ref[cb])


def conv_ffn(x, g, sc, sh, gate, w_up, cw, w_down, layer, T, tm=512, tf=512, sub=256):
    N, D = x.shape
    DFF = w_down.shape[1]
    tm = min(tm, T)
    assert N % tm == 0 and T % tm == 0 and DFF % tf == 0 and tm % FFN_HALO == 0
    sub = min(sub, tf)
    nff = DFF // tf
    hb = tm // FFN_HALO
    bmap = lambda i, j: ((i * tm) // T, 0, 0)
    return pl.pallas_call(
        functools.partial(_ffn_kernel, blocks_per_seq=T // tm, chunk=min(128, tm), sub=min(sub, tf)),
        grid=(N // tm, nff),
        in_specs=[pl.BlockSpec((tm, D), lambda i, j: (i, 0)),
                  pl.BlockSpec((FFN_HALO, D), lambda i, j: (jnp.maximum(i * hb - 1, 0), 0)),
                  pl.BlockSpec((1, D), lambda i, j: (0, 0)),
                  pl.BlockSpec((1, 1, D), bmap),
                  pl.BlockSpec((1, 1, D), bmap),
                  pl.BlockSpec((1, 1, D), bmap),
                  pl.BlockSpec((1, D, tf), lambda i, j: (layer, 0, j)),
                  pl.BlockSpec((1, D, tf), lambda i, j: (layer, 0, j + nff)),
                  pl.BlockSpec((1, 8, tf), lambda i, j: (layer, 0, j)),
                  pl.BlockSpec((1, 8, tf), lambda i, j: (layer, 0, j + nff)),
                  pl.BlockSpec((1, tf, D), lambda i, j: (layer, j, 0))],
        out_specs=pl.BlockSpec((tm, D), lambda i, j: (i, 0)),
        out_shape=jax.ShapeDtypeStruct((N, D), F32),
        scratch_shapes=[pltpu.VMEM((tm + FFN_HALO, D), BF16), pltpu.VMEM((tm, D), F32),
                        pltpu.VMEM((D // LANES, tm, LANES), F32),
                        pltpu.VMEM((tf // sub, tm + FFN_HALO, sub), F32),
                        pltpu.VMEM((tf // sub, tm + FFN_HALO, sub), F32)],
        compiler_params=_params(("parallel", "arbitrary")),
        name="conv_ffn",
    )(x, x, g.reshape(1, D), sc, sh, gate, w_up, w_up, cw, cw, w_down)


KV_PLAIN = -1
KV_TRANSPOSED = -2
KV_TILE = 256
SEL_UNROLL = 2


def _rope(y, cos, sin):
    return y * cos + pltpu.roll(y, HEAD_DIM // 2, axis=1) * sin


def _head_norm(x, g):
    ms = jnp.mean(x * x, axis=-1, keepdims=True)
    return x * lax.rsqrt(ms + EPS) * g


def _prep_kernel(*refs, n_q_heads, kv_kinds, n_groups):
    nkv = len(kv_kinds)
    zq_ref = refs[0]
    kv_refs = refs[1:1 + nkv]
    cos_ref, sin_ref, qn_ref, kn_ref = refs[1 + nkv:5 + nkv]
    q_out = refs[5 + nkv]
    kv_out = refs[6 + nkv:]
    cos = cos_ref[0]
    sin = sin_ref[0]
    scale = HEAD_DIM ** -0.5 * LOG2E
    for hd in range(n_q_heads):
        sl = slice(hd * HEAD_DIM, (hd + 1) * HEAD_DIM)
        y = _rope(_head_norm(zq_ref[0, :, sl].astype(F32), qn_ref[...]), cos, sin)
        q_out[0, :, sl] = (y * scale).astype(BF16)
    for idx, kind in enumerate(kv_kinds):
        for gi in range(n_groups):
            sl = slice(gi * HEAD_DIM, (gi + 1) * HEAD_DIM)
            a = kv_refs[idx][0, :, sl]
            if kind >= 0:
                a = _rope(_head_norm(a.astype(F32), kn_ref[kind:kind + 1, :]), cos, sin)
            if kind == KV_TRANSPOSED:
                for c in range(a.shape[0] // KV_TILE):
                    kv_out[idx][0, gi, c] = a[c * KV_TILE:(c + 1) * KV_TILE, :].astype(F32).T.astype(BF16)
            else:
                kv_out[idx][0, gi] = a.astype(BF16)


def attn_prep(z3, cosf, sinf, q_norm, k_norm, q_col, kv_cols, kv_kinds, n_groups, tm=1024):
    B, T, _ = z3.shape
    tm = min(tm, T)
    QW = 8 * HEAD_DIM
    KW = n_groups * HEAD_DIM
    assert q_col % QW == 0 and all(c % KW == 0 for c in kv_cols) and T % tm == 0
    nkv = len(kv_cols)
    in_specs = [pl.BlockSpec((1, tm, QW), lambda b, t: (b, t, q_col // QW))]
    for c in kv_cols:
        in_specs.append(pl.BlockSpec((1, tm, KW), functools.partial(lambda b, t, cb: (b, t, cb), cb=c // KW)))
    in_specs += [pl.BlockSpec((1, tm, HEAD_DIM), lambda b, t: (b, t, 0))] * 2
    in_specs += [pl.BlockSpec((1, HEAD_DIM), lambda b, t: (0, 0)),
                 pl.BlockSpec(k_norm.shape, lambda b, t: (0, 0))]
    out_specs = [pl.BlockSpec((1, tm, QW), lambda b, t: (b, t, 0))]
    out_shape = [jax.ShapeDtypeStruct((B, T, QW), BF16)]
    for kind in kv_kinds:
        if kind == KV_TRANSPOSED:
            assert tm % KV_TILE == 0
            out_specs.append(pl.BlockSpec((1, n_groups, tm // KV_TILE, HEAD_DIM, KV_TILE),
                                          lambda b, t: (b, 0, t, 0, 0)))
            out_shape.append(jax.ShapeDtypeStruct((B, n_groups, T // KV_TILE, HEAD_DIM, KV_TILE), BF16))
        else:
            out_specs.append(pl.BlockSpec((1, n_groups, tm, HEAD_DIM), lambda b, t: (b, 0, t, 0)))
            out_shape.append(jax.ShapeDtypeStruct((B, n_groups, T, HEAD_DIM), BF16))
    return pl.pallas_call(
        functools.partial(_prep_kernel, n_q_heads=QW // HEAD_DIM, kv_kinds=tuple(kv_kinds), n_groups=n_groups),
        grid=(B, T // tm),
        in_specs=in_specs, out_specs=out_specs, out_shape=out_shape,
        compiler_params=_params(("parallel", "parallel")),
        name="attn_prep",
    )(*([z3] * (1 + nkv)), cosf, sinf, q_norm.reshape(1, HEAD_DIM), k_norm)


def _compress_kernel(ak_ref, av_ref, w1k_ref, w1v_ref, pk_ref, pv_ref, w2k_ref, w2v_ref,
                     kn_ref, cos_ref, sin_ref, kc_ref, vc_ref):
    hid = NSA_CMP_HIDDEN

    def mlp(a_ref, w1_ref, p_ref, w2_ref):
        P = _dot(a_ref[0, 0], w1_ref[...])
        Q = _dot(p_ref[...], w1_ref[...])
        pb = Q[0:1, :hid] + Q[1:2, hid:]
        nxt = pltpu.roll(P[:, hid:], P.shape[0] - 1, axis=0)
        hdn = _silu(P[:, :hid] + nxt + pb)
        return _dot(hdn.astype(BF16), w2_ref[...])

    kc = mlp(ak_ref, w1k_ref, pk_ref, w2k_ref)
    kc = _rope(_head_norm(kc, kn_ref[...]), cos_ref[0], sin_ref[0])
    kc_ref[0, 0] = kc.astype(BF16)
    vc_ref[0, 0] = mlp(av_ref, w1v_ref, pv_ref, w2v_ref).T.astype(BF16)


def nsa_compress(kcr, vcr, w1k, w1v, pk, pv, w2k, w2v, kn0, cos_end, sin_end):
    B, G, T, dh = kcr.shape
    nch = T // NSA_CMP_STRIDE
    cw = NSA_CMP_STRIDE * dh
    a_spec = pl.BlockSpec((1, 1, nch, cw), lambda b, g: (b, g, 0, 0))
    full = lambda arr: pl.BlockSpec(arr.shape, lambda b, g: (0,) * arr.ndim)
    tab_spec = pl.BlockSpec((1, nch, dh), lambda b, g: (b, 0, 0))
    o_spec = pl.BlockSpec((1, 1, nch, dh), lambda b, g: (b, g, 0, 0))
    return pl.pallas_call(
        _compress_kernel,
        grid=(B, G),
        in_specs=[a_spec, a_spec, full(w1k), full(w1v), full(pk), full(pv), full(w2k), full(w2v),
                  full(kn0), tab_spec, tab_spec],
        out_specs=[o_spec, pl.BlockSpec((1, 1, dh, nch), lambda b, g: (b, g, 0, 0))],
        out_shape=[jax.ShapeDtypeStruct((B, G, nch, dh), BF16), jax.ShapeDtypeStruct((B, G, dh, nch), BF16)],
        compiler_params=_params(("parallel", "parallel")),
        name="nsa_compress",
    )(kcr.reshape(B, G, nch, cw), vcr.reshape(B, G, nch, cw), w1k, w1v, pk, pv, w2k, w2v,
      kn0, cos_end, sin_end)


def _nsa_group(g, q_ref, kc_ref, vct_ref, ks_ref, vst_ref, kw_ref, vwt_ref, gl_ref, ov_ref, ext_ref,
               o_ref, m_ref, l_ref, acc_ref, *, n_rep, n_cmp, n_sel, n_top):
    tq = q_ref.shape[1]
    tk = KV_TILE
    nrow = kc_ref.shape[2]
    t0 = pl.program_id(1) * tq
    h0 = g * n_rep
    q4 = jnp.concatenate([q_ref[0, :, (h0 + r) * HEAD_DIM:(h0 + r + 1) * HEAD_DIM] for r in range(n_rep)], axis=0)

    def heads(a):
        return jnp.concatenate([a] * n_rep, axis=1)

    jb = lax.broadcasted_iota(jnp.int32, (nrow, tq), 0)
    tcol = t0 + lax.broadcasted_iota(jnp.int32, (nrow, tq), 1)
    valid_c = (jb * NSA_CMP_STRIDE + (NSA_CMP_LEN - 1) <= tcol) & (jb < n_cmp)
    validf = heads(jnp.where(valid_c, 1.0, 0.0))
    s = _dot_nt(kc_ref[0, g], q4) + heads(jnp.where(valid_c, 0.0, NEG))
    e = jnp.exp2(s - jnp.max(s, axis=0, keepdims=True))
    p = e / jnp.sum(e, axis=0, keepdims=True) * validf
    o_cmp = _dot(vct_ref[0, g], p.astype(BF16))
    psum = p[:, 0:tq]
    for r in range(1, n_rep):
        psum = psum + p[:, r * tq:(r + 1) * tq]

    hi = psum.astype(BF16)
    lo = (psum - hi.astype(F32)).astype(BF16)
    imp = _dot(ov_ref[...], hi) + _dot(ov_ref[...], lo)
    nsr = ov_ref.shape[0]
    jb = lax.broadcasted_iota(jnp.int32, (nsr, tq), 0)
    tcol = t0 + lax.broadcasted_iota(jnp.int32, (nsr, tq), 1)
    cur = lax.shift_right_logical(tcol, int(np.log2(NSA_SEL_LEN)))
    forced = (jb == 0) | (jb == cur) | (jb == cur - 1)
    score = jnp.where(forced, NSA_FORCE, jnp.where(jb * NSA_SEL_LEN <= tcol, imp, -1.0))
    score = jnp.where(jb < n_sel, score, -2.0)
    selT = jnp.zeros((nsr, tq), F32)
    for _ in range(n_top):
        best = jnp.max(score, axis=0, keepdims=True)
        idx = jnp.min(jnp.where(score == best, jb, nsr), axis=0, keepdims=True)
        hit = jb == idx
        selT = jnp.where(hit, 1.0, selT)
        score = jnp.where(hit, -3.0, score)
    sel = selT.astype(BF16)

    m_ref[...] = jnp.full(m_ref.shape, NEG, F32)
    l_ref[...] = jnp.zeros(l_ref.shape, F32)
    acc_ref[...] = jnp.zeros(acc_ref.shape, F32)
    krow = lax.broadcasted_iota(jnp.int32, (tk, tq), 0)
    tq_col = t0 + lax.broadcasted_iota(jnp.int32, (tk, tq), 1)

    def sel_body(it, carry):
        s = []
        for u in range(SEL_UNROLL):
            kt = it * SEL_UNROLL + u
            k0 = pl.multiple_of(kt * tk, tk)
            mask = (_dot(ext_ref[kt], sel) > 0.5) & (k0 + krow <= tq_col)
            s.append(_dot_nt(ks_ref[0, g, pl.ds(k0, tk), :], q4) + heads(jnp.where(mask, 0.0, NEG)))
        m_old = m_ref[...]
        m_new = m_old
        for u in range(SEL_UNROLL):
            m_new = jnp.maximum(m_new, jnp.max(s[u], axis=0, keepdims=True))
        alpha = jnp.exp2(m_old - m_new)
        lsum = alpha * l_ref[...]
        pv = alpha * acc_ref[...]
        for u in range(SEL_UNROLL):
            p = jnp.exp2(s[u] - m_new)
            lsum = lsum + jnp.sum(p, axis=0, keepdims=True)
            pv = pv + _dot(vst_ref[0, g, it * SEL_UNROLL + u], p.astype(BF16))
        l_ref[...] = lsum
        acc_ref[...] = pv
        m_ref[...] = m_new
        return carry

    n_tiles = (t0 + tq) // tk
    lax.fori_loop(0, (n_tiles + SEL_UNROLL - 1) // SEL_UNROLL, sel_body, 0)
    o_sel = acc_ref[...] / l_ref[...]

    W = NSA_WINDOW
    n_win = (W + tq) // tk
    kstart = pl.multiple_of(jnp.maximum(t0 - W, 0), tk)
    jt = kstart // tk
    sw = []
    for c in range(n_win):
        rel = tq_col - (kstart + c * tk) - krow
        bias = heads(jnp.where((rel >= 0) & (rel < W), 0.0, NEG))
        sw.append(_dot_nt(kw_ref[0, g, pl.ds(kstart + c * tk, tk), :], q4) + bias)
    m = jnp.max(sw[0], axis=0, keepdims=True)
    for c in range(1, n_win):
        m = jnp.maximum(m, jnp.max(sw[c], axis=0, keepdims=True))
    den = jnp.zeros_like(m)
    o_win = jnp.zeros((HEAD_DIM, n_rep * tq), F32)
    for c in range(n_win):
        ew = jnp.exp2(sw[c] - m)
        den = den + jnp.sum(ew, axis=0, keepdims=True)
        o_win = o_win + _dot(vwt_ref[0, g, jt + c], ew.astype(BF16))
    o_win = o_win / den

    gT = _sigmoid(gl_ref[0, :, g * LANES:(g + 1) * LANES]).T
    for r in range(n_rep):
        hs = slice(r * tq, (r + 1) * tq)
        c3 = 3 * r
        oT = (gT[c3:c3 + 1, :] * o_cmp[:, hs] + gT[c3 + 1:c3 + 2, :] * o_sel[:, hs]
              + gT[c3 + 2:c3 + 3, :] * o_win[:, hs])
        o_ref[0, :, (h0 + r) * HEAD_DIM:(h0 + r + 1) * HEAD_DIM] = oT.T.astype(BF16)


def _nsa_kernel(q_ref, kc_ref, *refs, **static):
    for g in range(kc_ref.shape[1]):
        _nsa_group(g, q_ref, kc_ref, *refs, **static)


def nsa_attention(qn, kc, vct, ks, vst, kw, vwt, gl3, tq=512):
    B, T, QW = qn.shape
    G = kc.shape[1]
    n_rep = QW // HEAD_DIM // G
    nrow = kc.shape[2]
    tk = KV_TILE
    n_cmp = (T - NSA_CMP_LEN) // NSA_CMP_STRIDE + 1
    n_sel = T // NSA_SEL_LEN
    assert T % tq == 0 and n_sel <= nrow and T >= NSA_WINDOW + tq and tq % tk == 0 and NSA_WINDOW % tk == 0
    assert (tq // tk) % SEL_UNROLL == 0
    starts = np.arange(nrow) * NSA_CMP_STRIDE
    sel_start = np.arange(nrow) * NSA_SEL_LEN
    nsr = -(-n_sel // BF16_SUBLANES) * BF16_SUBLANES
    ov = ((starts[None, :] <= sel_start[:, None] + NSA_SEL_LEN - 1)
          & (starts[None, :] + NSA_CMP_LEN - 1 >= sel_start[:, None])
          & (np.arange(nrow)[:, None] < n_sel) & (np.arange(nrow)[None, :] < n_cmp))[:nsr]
    ext = (np.arange(T)[:, None] // NSA_SEL_LEN == np.arange(nsr)[None, :]).reshape(T // tk, tk, nsr)
    k_spec = pl.BlockSpec((1, G, T, HEAD_DIM), lambda b, i: (b, 0, 0, 0))
    vt_spec = pl.BlockSpec((1, G, T // tk, HEAD_DIM, tk), lambda b, i: (b, 0, 0, 0, 0))
    c_spec = pl.BlockSpec((1, G, nrow, HEAD_DIM), lambda b, i: (b, 0, 0, 0))
    return pl.pallas_call(
        functools.partial(_nsa_kernel, n_rep=n_rep, n_cmp=n_cmp, n_sel=n_sel, n_top=min(NSA_TOPK, n_sel)),
        grid=(B, T // tq),
        in_specs=[pl.BlockSpec((1, tq, QW), lambda b, i: (b, i, 0)),
                  c_spec, pl.BlockSpec((1, G, HEAD_DIM, nrow), lambda b, i: (b, 0, 0, 0)),
                  k_spec, vt_spec, k_spec, vt_spec,
                  pl.BlockSpec((1, tq, G * LANES), lambda b, i: (b, i, 0)),
                  pl.BlockSpec((nsr, nrow), lambda b, i: (0, 0)),
                  pl.BlockSpec((T // tk, tk, nsr), lambda b, i: (0, 0, 0))],
        out_specs=pl.BlockSpec((1, tq, QW), lambda b, i: (b, i, 0)),
        out_shape=jax.ShapeDtypeStruct((B, T, QW), BF16),
        scratch_shapes=[pltpu.VMEM((1, n_rep * tq), F32), pltpu.VMEM((1, n_rep * tq), F32),
                        pltpu.VMEM((HEAD_DIM, n_rep * tq), F32)],
        compiler_params=_params(("parallel", "arbitrary")),
        name="nsa_attention",
    )(qn, kc, vct, ks, vst, kw, vwt, gl3, jnp.asarray(ov, BF16), jnp.asarray(ext, BF16))


def _swa_kernel(sink_ref, q_ref, k_ref, vt_ref, o_ref, *, n_rep):
    tq = q_ref.shape[1]
    tk = KV_TILE
    W = SWA_WINDOW
    n_win = (tk + tq) // tk
    t0 = pl.program_id(1) * tq
    kstart = pl.multiple_of(jnp.maximum(t0 - tk, 0), tk)
    jt = kstart // tk
    krow = lax.broadcasted_iota(jnp.int32, (tk, tq), 0)
    tq_col = t0 + lax.broadcasted_iota(jnp.int32, (tk, tq), 1)
    bias = []
    for c in range(n_win):
        rel = tq_col - (kstart + c * tk) - krow
        bias.append(jnp.concatenate([jnp.where((rel >= 0) & (rel < W), 0.0, NEG)] * n_rep, axis=1))
    for g in range(k_ref.shape[1]):
        h0 = g * n_rep
        q4 = jnp.concatenate([q_ref[0, :, (h0 + r) * HEAD_DIM:(h0 + r + 1) * HEAD_DIM] for r in range(n_rep)],
                             axis=0)
        sink = jnp.concatenate([jnp.full((1, tq), sink_ref[h0 + r] * LOG2E, F32) for r in range(n_rep)], axis=1)
        s = [_dot_nt(k_ref[0, g, pl.ds(kstart + c * tk, tk), :], q4) + bias[c] for c in range(n_win)]
        m = sink
        for c in range(n_win):
            m = jnp.maximum(m, jnp.max(s[c], axis=0, keepdims=True))
        den = jnp.exp2(sink - m)
        oT = jnp.zeros((HEAD_DIM, n_rep * tq), F32)
        for c in range(n_win):
            e = jnp.exp2(s[c] - m)
            den = den + jnp.sum(e, axis=0, keepdims=True)
            oT = oT + _dot(vt_ref[0, g, jt + c], e.astype(BF16))
        oT = oT / den
        for r in range(n_rep):
            o_ref[0, :, (h0 + r) * HEAD_DIM:(h0 + r + 1) * HEAD_DIM] = oT[:, r * tq:(r + 1) * tq].T.astype(BF16)


def swa_attention(qn, kn, vt, sinks, tq=256):
    B, T, QW = qn.shape
    G = kn.shape[1]
    n_rep = QW // HEAD_DIM // G
    tk = KV_TILE
    assert T % tq == 0 and tq % tk == 0 and SWA_WINDOW <= tk and T >= tk + tq
    return pl.pallas_call(
        functools.partial(_swa_kernel, n_rep=n_rep),
        grid=(B, T // tq),
        in_specs=[pl.BlockSpec(memory_space=pltpu.SMEM),
                  pl.BlockSpec((1, tq, QW), lambda b, i: (b, i, 0)),
                  pl.BlockSpec((1, G, T, HEAD_DIM), lambda b, i: (b, 0, 0, 0)),
                  pl.BlockSpec((1, G, T // tk, HEAD_DIM, tk), lambda b, i: (b, 0, 0, 0, 0))],
        out_specs=pl.BlockSpec((1, tq, QW), lambda b, i: (b, i, 0)),
        out_shape=jax.ShapeDtypeStruct((B, T, QW), BF16),
        compiler_params=_params(("parallel", "parallel")),
        name="swa_attention",
    )(sinks, qn, kn, vt)


CONF_HALO = 32


def _conformer_kernel(a1_ref, a2_ref, h1_ref, h2_ref, w_ref, b_ref, g_ref, be_ref, o_ref,
                      glu_ref, conv_ref, *, blocks_per_seq):
    tm, C = a1_ref.shape
    PRE = CONF_HALO
    nctx = CONV_WIDTH - 1
    first = (pl.program_id(0) % blocks_per_seq) == 0

    gh = jnp.where(first, 0.0, h1_ref[...].astype(F32) * _sigmoid(h2_ref[...].astype(F32)))
    sub0 = lax.broadcasted_iota(jnp.int32, (8 * nctx, LANES), 0) % 8 == 0
    for cs in range(C // LANES):
        sl = slice(cs * LANES, (cs + 1) * LANES)
        g = _to_tiles(a1_ref[:, sl].astype(F32) * _sigmoid(a2_ref[:, sl].astype(F32)))
        glu_ref[cs, 8 * PRE:, :] = g
        moved = pltpu.roll(g[tm - 8 * nctx:, :], 1, axis=0)
        halo = jnp.concatenate([jnp.broadcast_to(gh[r:r + 1, sl], (8, LANES)) for r in range(PRE - nctx, PRE)],
                               axis=0)
        glu_ref[cs, 8 * (PRE - nctx):8 * PRE, :] = jnp.where(sub0, halo, moved)

    rb = 128

    def conv_group(gi, carry):
        r0 = pl.multiple_of(gi * rb, rb)
        for cs in range(C // LANES):
            sl = slice(cs * LANES, (cs + 1) * LANES)
            acc = jnp.broadcast_to(b_ref[:, sl], (rb, LANES))
            for k in range(CONV_WIDTH):
                off = 8 * (PRE - nctx + k)
                acc = acc + w_ref[k:k + 1, sl] * glu_ref[cs, pl.ds(r0 + off, rb), :]
            conv_ref[cs, pl.ds(r0, rb), :] = acc
        return carry

    lax.fori_loop(0, tm // rb, conv_group, 0)

    ns = C // LANES
    for r0 in range(0, tm, rb):
        y = [conv_ref[cs, r0:r0 + rb, :] for cs in range(ns)]
        mu = sum(jnp.sum(v, axis=-1, keepdims=True) for v in y) / C
        d = [v - mu for v in y]
        var = sum(jnp.sum(v * v, axis=-1, keepdims=True) for v in d) / C
        rs = lax.rsqrt(var + EPS)
        for cs in range(ns):
            sl = slice(cs * LANES, (cs + 1) * LANES)
            conv_ref[cs, r0:r0 + rb, :] = _silu(d[cs] * rs * g_ref[:, sl] + be_ref[:, sl])
    for cs in range(ns):
        o_ref[:, cs * LANES:(cs + 1) * LANES] = _from_tiles(conv_ref[cs]).astype(BF16)


def conformer_conv(z, conv_w, conv_b, ln_g, ln_b, T, tm=512):
    N = z.shape[0]
    C = conv_w.shape[1]
    tm = min(tm, T)
    assert N % tm == 0 and T % tm == 0 and tm % CONF_HALO == 0 and CONV_WIDTH - 1 <= min(CONF_HALO, tm // 8)
    hb = tm // CONF_HALO
    w = jnp.zeros((CONF_HALO, C), F32).at[:CONV_WIDTH].set(conv_w)
    hmap = lambda c: (lambda i: (jnp.maximum(i * hb - 1, 0), c))
    vec = pl.BlockSpec((1, C), lambda i: (0, 0))
    return pl.pallas_call(
        functools.partial(_conformer_kernel, blocks_per_seq=T // tm),
        grid=(N // tm,),
        in_specs=[pl.BlockSpec((tm, C), lambda i: (i, 0)), pl.BlockSpec((tm, C), lambda i: (i, 1)),
                  pl.BlockSpec((CONF_HALO, C), hmap(0)), pl.BlockSpec((CONF_HALO, C), hmap(1)),
                  pl.BlockSpec((CONF_HALO, C), lambda i: (0, 0)), vec, vec, vec],
        out_specs=pl.BlockSpec((tm, C), lambda i: (i, 0)),
        out_shape=jax.ShapeDtypeStruct((N, C), BF16),
        scratch_shapes=[pltpu.VMEM((C // LANES, tm + 8 * CONF_HALO, LANES), F32),
                        pltpu.VMEM((C // LANES, tm, LANES), F32)],
        compiler_params=_params(("parallel",)),
        name="conformer_conv",
    )(z, z, z, z, w, conv_b.reshape(1, C), ln_g.reshape(1, C), ln_b.reshape(1, C))


def _gmlp_kernel(u0_ref, u1_ref, v0_ref, v1_ref, g_ref, be_ref, ws_ref, bs_ref, o_ref, *, n_groups):
    tm, hw = v0_ref.shape
    C = 2 * hw
    ch = GMLP_CHUNK
    gv = [_gelu_tanh(v0_ref[...].astype(F32)), _gelu_tanh(v1_ref[...].astype(F32))]
    mu = (jnp.sum(gv[0], axis=-1, keepdims=True) + jnp.sum(gv[1], axis=-1, keepdims=True)) / C
    d = [gv[0] - mu, gv[1] - mu]
    var = (jnp.sum(d[0] * d[0], axis=-1, keepdims=True) + jnp.sum(d[1] * d[1], axis=-1, keepdims=True)) / C
    rs = lax.rsqrt(var + EPS)
    vn = [(d[hf] * rs * g_ref[:, hf * hw:(hf + 1) * hw] + be_ref[:, hf * hw:(hf + 1) * hw]).astype(BF16)
          for hf in range(2)]
    u_refs = [u0_ref, u1_ref]
    tril = lax.broadcasted_iota(jnp.int32, (ch, ch), 0) >= lax.broadcasted_iota(jnp.int32, (ch, ch), 1)
    gph = n_groups // 2
    for gi in range(n_groups):
        hf, col = gi // gph, (gi % gph) * HEAD_DIM
        w = jnp.where(tril, ws_ref[gi], 0.0).astype(BF16)
        for c in range(tm // ch):
            rows = slice(c * ch, (c + 1) * ch)
            sp = _dot(w, vn[hf][rows, col:col + HEAD_DIM]) + bs_ref[gi]
            u = _gelu_tanh(u_refs[hf][rows, col:col + HEAD_DIM].astype(F32))
            o_ref[rows, gi * HEAD_DIM:(gi + 1) * HEAD_DIM] = (u * sp).astype(BF16)


def chunked_gmlp(z, u_col, ln_g, ln_b, w_s, b_s, T, tm=1024):
    N = z.shape[0]
    n_groups, ch, _ = w_s.shape
    C = n_groups * HEAD_DIM
    hw = C // 2
    tm = min(tm, T)
    assert N % tm == 0 and T % tm == 0 and tm % ch == 0 and u_col % hw == 0
    cb = u_col // hw
    bsb = jnp.broadcast_to(b_s[:, :, None], (n_groups, ch, HEAD_DIM))
    zspec = lambda k: pl.BlockSpec((tm, hw), lambda i: (i, cb + k))
    vec = pl.BlockSpec((1, C), lambda i: (0, 0))
    return pl.pallas_call(
        functools.partial(_gmlp_kernel, n_groups=n_groups),
        grid=(N // tm,),
        in_specs=[zspec(0), zspec(1), zspec(2), zspec(3), vec, vec,
                  pl.BlockSpec((n_groups, ch, ch), lambda i: (0, 0, 0)),
                  pl.BlockSpec((n_groups, ch, HEAD_DIM), lambda i: (0, 0, 0))],
        out_specs=pl.BlockSpec((tm, C), lambda i: (i, 0)),
        out_shape=jax.ShapeDtypeStruct((N, C), BF16),
        compiler_params=_params(("parallel",)),
        name="chunked_gmlp",
    )(z, z, z, z, ln_g.reshape(1, C), ln_b.reshape(1, C), w_s, bsb)


def _even_mixer(xf, B, T, g, sc, sh, cosf, sinf, w_in, w_out, conv_w, conv_b, conv_ln_g, conv_ln_b,
                q_norm, k_norm, cmp_k_pos, cmp_k_w1, cmp_k_w2, cmp_v_pos, cmp_v_w1, cmp_v_w2):
    D = xf.shape[1]
    G = NSA_KV_GROUPS
    C = conv_w.shape[1]
    QW = D // 2
    KW = G * HEAD_DIM
    n_rep = QW // HEAD_DIM // G
    main = 2 * C + QW + 6 * KW
    wg = w_in[:, main:].reshape(D, G, n_rep * 3)
    wg = jnp.pad(wg, ((0, 0), (0, 0), (0, LANES - n_rep * 3))).reshape(D, G * LANES)
    z, gl = in_proj(xf, g, sc, sh, w_in.astype(BF16), main, wg.astype(BF16), T)
    ya = conformer_conv(z, conv_w, conv_b, conv_ln_g, conv_ln_b, T)
    q_col = 2 * C
    kv_cols = [q_col + QW + k * KW for k in range(6)]
    qn, kcr, vcr, ks, vs, kw, vw = attn_prep(z.reshape(B, T, main), cosf, sinf, q_norm, k_norm,
                                             q_col, kv_cols,
                                             (KV_PLAIN, KV_PLAIN, 1, KV_TRANSPOSED, 2, KV_TRANSPOSED), G)
    half = NSA_CMP_STRIDE * HEAD_DIM
    hid = NSA_CMP_HIDDEN
    nch = T // NSA_CMP_STRIDE

    def w1cat(w1):
        return jnp.concatenate([w1[:half], w1[half:]], axis=1).astype(BF16)

    def posrows(pos):
        return jnp.zeros((8, half), F32).at[0].set(pos[:NSA_CMP_STRIDE].reshape(half)) \
                  .at[1].set(pos[NSA_CMP_STRIDE:].reshape(half)).astype(BF16)

    def at_block_ends(tab):
        e = tab[:, NSA_CMP_LEN - 1::NSA_CMP_STRIDE]
        return jnp.pad(e, ((0, 0), (0, nch - e.shape[1]), (0, 0)))

    kc, vc = nsa_compress(kcr, vcr, w1cat(cmp_k_w1), w1cat(cmp_v_w1), posrows(cmp_k_pos), posrows(cmp_v_pos),
                          cmp_k_w2.astype(BF16), cmp_v_w2.astype(BF16), k_norm[0:1],
                          at_block_ends(cosf), at_block_ends(sinf))
    yb = nsa_attention(qn, kc, vc, ks, vs, kw, vw, gl.reshape(B, T, G * LANES))
    return ya, yb.reshape(B * T, QW), w_out.astype(BF16)


def _odd_mixer(xf, B, T, g, sc, sh, cosf, sinf, w_in, w_out, q_norm, k_norm, sinks, ln_g, ln_b, w_s, b_s):
    D = xf.shape[1]
    G = SWA_KV_HEADS
    QW = D // 2
    KW = G * HEAD_DIM
    z, = in_proj(xf, g, sc, sh, w_in.astype(BF16), w_in.shape[1], None, T)
    qn, kn, v = attn_prep(z.reshape(B, T, z.shape[1]), cosf, sinf, q_norm, k_norm.reshape(1, HEAD_DIM),
                          0, [QW, QW + KW], (0, KV_TRANSPOSED), G)
    yc = swa_attention(qn, kn, v, sinks)
    yd = chunked_gmlp(z, QW + 2 * KW, ln_g, ln_b, w_s, b_s, T)
    return yc.reshape(B * T, QW), yd, w_out.astype(BF16)


def kernel(x, c, positions, ada_w, ada_b, norm_g, ffn_w_up, ffn_conv_w, ffn_conv_b, ffn_w_down, ev_w_in, ev_w_out, ev_conv_w, ev_conv_b, ev_conv_ln_g, ev_conv_ln_b, ev_q_norm, ev_k_norm, ev_cmp_k_pos, ev_cmp_k_w1, ev_cmp_k_w2, ev_cmp_v_pos, ev_cmp_v_w1, ev_cmp_v_w2, od_w_in, od_w_out, od_q_norm, od_k_norm, od_sinks, od_gmlp_ln_g, od_gmlp_ln_b, od_gmlp_w_s, od_gmlp_b_s):
    B, T, D = x.shape
    depth = ada_w.shape[0]
    cosf, sinf = rope_tables(positions)
    mod = adaln(c, ada_w, ada_b)
    xf = x.reshape(B * T, D)
    w_up_b = ffn_w_up.astype(BF16)
    w_down_b = ffn_w_down.astype(BF16)
    cw_all = jnp.zeros((depth, 8, ffn_conv_w.shape[2]), F32).at[:, :FFN_CONV_WIDTH].set(ffn_conv_w) \
                .at[:, FFN_CONV_WIDTH].set(ffn_conv_b)
    for i in range(depth):
        sh1, sc1, g1, sh2, sc2, g2 = [m.reshape(B, 1, D) for m in jnp.split(mod[i], 6, axis=-1)]
        j = i // 2
        if i % 2 == 0:
            ya, yb, w_out = _even_mixer(xf, B, T, norm_g[i, 0], sc1, sh1, cosf, sinf, ev_w_in[j], ev_w_out[j],
                                        ev_conv_w[j], ev_conv_b[j], ev_conv_ln_g[j], ev_conv_ln_b[j],
                                        ev_q_norm[j], ev_k_norm[j], ev_cmp_k_pos[j], ev_cmp_k_w1[j],
                                        ev_cmp_k_w2[j], ev_cmp_v_pos[j], ev_cmp_v_w1[j], ev_cmp_v_w2[j])
        else:
            ya, yb, w_out = _odd_mixer(xf, B, T, norm_g[i, 0], sc1, sh1, cosf, sinf, od_w_in[j], od_w_out[j],
                                       od_q_norm[j], od_k_norm[j], od_sinks[j], od_gmlp_ln_g[j],
                                       od_gmlp_ln_b[j], od_gmlp_w_s[j], od_gmlp_b_s[j])
        xf = out_proj(ya, yb, w_out, xf, g1, T)
        xf = conv_ffn(xf, norm_g[i, 1], sc2, sh2, g2, w_up_b, cw_all, w_down_b, i, T)
    return xf.reshape(B, T, D)
```
